```python
import jax, jax.numpy as jnp
from jax import lax
import numpy as np

D_MODEL = 2048
BATCH = 4
SEQ = 2048
DEPTH = 1

D_MIX = D_MODEL
D_POOL = D_MIX // 2
POOL_WINDOWS = (2, 4, 8, 16)
N_POOL_GROUPS = len(POOL_WINDOWS)
POOL_GROUP_DIM = D_POOL // N_POOL_GROUPS
D_GLA = D_MIX - D_POOL
GLA_HEADS = 4
GLA_DV = D_GLA // GLA_HEADS
GLA_DK_TOTAL = D_GLA // 2
GLA_DK = GLA_DK_TOTAL // GLA_HEADS
GLA_GATE_RANK = 16
GATE_LOGIT_NORMALIZER = 16.0
CHUNK = 64
D_IN = D_POOL + 2 * GLA_DK_TOTAL + 2 * D_GLA + GLA_GATE_RANK
D_FF = 5632
EPS = 1e-6

kernel_name = "hymba_pool_gla_macaron_block"


def rmsnorm(x, g):
    xf = x.astype(jnp.float32)
    y = xf * lax.rsqrt(jnp.mean(xf * xf, axis=-1, keepdims=True) + EPS)
    return (y * g.astype(jnp.float32)).astype(x.dtype)


def swiglu(h, w_in, w_out):
    gu = h @ w_in
    gate, up = gu[..., :D_FF], gu[..., D_FF:]
    return (jax.nn.silu(gate) * up) @ w_out


def pool_mixer(u, w_pool, pool_scale):
    B, S, _ = u.shape
    uf = u.astype(jnp.float32).reshape(B, S, N_POOL_GROUPS, POOL_GROUP_DIM)
    cs = jnp.cumsum(uf, axis=1)
    pos1 = jnp.arange(1, S + 1, dtype=jnp.int32)
    means = []
    for gi, w in enumerate(POOL_WINDOWS):
        c = cs[:, :, gi]
        shifted = jnp.pad(c, ((0, 0), (w, 0), (0, 0)))[:, :S]
        cnt = jnp.minimum(pos1, w).astype(jnp.float32)[None, :, None]
        means.append((c - shifted) / cnt)
    pooled = jnp.stack(means, axis=2) - uf
    y = jnp.einsum('bsgc,gcd->bsgd', pooled.astype(u.dtype), w_pool)
    return y.reshape(B, S, D_POOL) * pool_scale


def gla_mixer(q, k, v, g_out, gate_lr, w_alpha, b_alpha, gla_norm):
    B, S, _ = q.shape
    N = S // CHUNK
    log_alpha = jax.nn.log_sigmoid((gate_lr @ w_alpha + b_alpha).astype(jnp.float32)) / GATE_LOGIT_NORMALIZER

    def heads(t, d):
        return t.astype(jnp.float32).reshape(B, N, CHUNK, GLA_HEADS, d).transpose(0, 3, 1, 2, 4)

    qh = heads(q, GLA_DK) * (GLA_DK ** -0.5)
    kh = heads(k, GLA_DK)
    vh = heads(v, GLA_DV)
    bcum = jnp.cumsum(heads(log_alpha, GLA_DK), axis=3)
    b_last = bcum[:, :, :, -1:]
    q_dec = qh * jnp.exp(bcum)
    k_inv = kh * jnp.exp(-bcum)
    k_tail = kh * jnp.exp(b_last - bcum)

    mask = jnp.tril(jnp.ones((CHUNK, CHUNK), dtype=bool))
    scores = jnp.where(mask, jnp.einsum('bhnid,bhnjd->bhnij', q_dec, k_inv), 0.0)
    o_intra = jnp.einsum('bhnij,bhnjv->bhniv', scores, vh)

    kv_chunk = jnp.einsum('bhncd,bhncv->bhndv', k_tail, vh)
    decay_chunk = jnp.exp(b_last[:, :, :, 0])

    def step(state, inp):
        dec, kv = inp
        return state * dec[..., None] + kv, state

    init = jnp.zeros((B, GLA_HEADS, GLA_DK, GLA_DV), jnp.float32)
    _, states = lax.scan(step, init, (decay_chunk.transpose(2, 0, 1, 3), kv_chunk.transpose(2, 0, 1, 3, 4)))
    states = states.transpose(1, 2, 0, 3, 4)
    o = o_intra + jnp.einsum('bhncd,bhndv->bhncv', q_dec, states)

    o = o * lax.rsqrt(jnp.mean(o * o, axis=-1, keepdims=True) + EPS) * gla_norm.astype(jnp.float32)
    o = o.transpose(0, 2, 3, 1, 4).reshape(B, S, D_GLA)
    return (o * jax.nn.silu(g_out.astype(jnp.float32))).astype(q.dtype)


def setup_inputs(seed: int = 0) -> dict:
    key = jax.random.key(seed)
    ks = jax.random.split(key, 16)
    f32 = jnp.float32

    def nrm(k, shape, fan_in):
        return jax.random.normal(k, shape, f32) * (fan_in ** -0.5)

    def gain(k, shape):
        return 1.0 + 0.02 * jax.random.normal(k, shape, f32)

    L = DEPTH
    return {
        "x": jax.random.normal(ks[0], (BATCH, SEQ, D_MODEL), f32),
        "ffn1_norm": gain(ks[1], (L, D_MODEL)),
        "ffn1_w_in": nrm(ks[2], (L, D_MODEL, 2 * D_FF), D_MODEL),
        "ffn1_w_out": nrm(ks[3], (L, D_FF, D_MODEL), D_FF),
        "mix_norm": gain(ks[4], (L, D_MODEL)),
        "w_in_mix": nrm(ks[5], (L, D_MODEL, D_IN), D_MODEL),
        "w_pool": nrm(ks[6], (L, N_POOL_GROUPS, POOL_GROUP_DIM, POOL_GROUP_DIM), POOL_GROUP_DIM),
        "pool_scale": 1.0 + 0.1 * jax.random.normal(ks[7], (L, D_POOL), f32),
        "w_alpha": nrm(ks[8], (L, GLA_GATE_RANK, GLA_DK_TOTAL), GLA_GATE_RANK),
        "b_alpha": 0.01 * jax.random.normal(ks[9], (L, GLA_DK_TOTAL), f32),
        "gla_norm": gain(ks[10], (L, GLA_DV)),
        "w_out_mix": nrm(ks[11], (L, D_MIX, D_MODEL), D_MIX),
        "ffn2_norm": gain(ks[12], (L, D_MODEL)),
        "ffn2_w_in": nrm(ks[13], (L, D_MODEL, 2 * D_FF), D_MODEL),
        "ffn2_w_out": nrm(ks[14], (L, D_FF, D_MODEL), D_FF),
        "final_norm": gain(ks[15], (D_MODEL,)),
    }


def reference(x, ffn1_norm, ffn1_w_in, ffn1_w_out, mix_norm, w_in_mix, w_pool, pool_scale,
              w_alpha, b_alpha, gla_norm, w_out_mix, ffn2_norm, ffn2_w_in, ffn2_w_out, final_norm):
    h = x
    o_q = D_POOL
    o_k = o_q + GLA_DK_TOTAL
    o_v = o_k + GLA_DK_TOTAL
    o_g = o_v + D_GLA
    o_r = o_g + D_GLA
    for l in range(DEPTH):
        h = h + 0.5 * swiglu(rmsnorm(h, ffn1_norm[l]), ffn1_w_in[l], ffn1_w_out[l])
        u = rmsnorm(h, mix_norm[l]) @ w_in_mix[l]
        y_pool = pool_mixer(u[..., :o_q], w_pool[l], pool_scale[l])
        y_gla = gla_mixer(u[..., o_q:o_k], u[..., o_k:o_v], u[..., o_v:o_g], u[..., o_g:o_r],
                          u[..., o_r:], w_alpha[l], b_alpha[l], gla_norm[l])
        h = h + jnp.concatenate([y_pool.astype(h.dtype), y_gla.astype(h.dtype)], axis=-1) @ w_out_mix[l]
        h = h + 0.5 * swiglu(rmsnorm(h, ffn2_norm[l]), ffn2_w_in[l], ffn2_w_out[l])
    return rmsnorm(h, final_norm)
```

```python
import functools

import jax
import jax.numpy as jnp
from jax import lax
from jax.experimental import pallas as pl
from jax.experimental.pallas import tpu as pltpu

F32 = jnp.float32
BF16 = jnp.bfloat16

EPS = 1e-6
POOL_WINDOWS = (2, 4, 8, 16)
N_POOL_GROUPS = len(POOL_WINDOWS)
GLA_HEADS = 4
GLA_GATE_RANK = 16
GATE_LOGIT_NORMALIZER = 16.0
CHUNK = 64

LANES = 128
VMEM_LIMIT_BYTES = 56 * 1024 * 1024


def _rmsnorm(x, gain):
    ms = jnp.mean(x * x, axis=-1, keepdims=True)
    return x * lax.rsqrt(ms + EPS) * gain


def _silu(x):
    return x * (1.0 / (1.0 + jnp.exp(-x)))


def _params(semantics):
    return pltpu.CompilerParams(dimension_semantics=semantics, vmem_limit_bytes=VMEM_LIMIT_BYTES)


def _ffn_body(x_ref, gain_ref, wg_ref, wu_ref, wo_ref, fgain_ref, o_ref, n_ref, *, final_norm):
    j = pl.program_id(1)

    @pl.when(j == 0)
    def _init():
        x = x_ref[...]
        n_ref[...] = _rmsnorm(x, gain_ref[...]).astype(BF16)
        o_ref[...] = x

    n = n_ref[...]
    gate = jnp.dot(n, wg_ref[...], preferred_element_type=F32)
    up = jnp.dot(n, wu_ref[...], preferred_element_type=F32)
    act = (0.5 * _silu(gate) * up).astype(BF16)
    o_ref[...] += jnp.dot(act, wo_ref[...], preferred_element_type=F32)

    if final_norm:

        @pl.when(j == pl.num_programs(1) - 1)
        def _finish():
            o_ref[...] = _rmsnorm(o_ref[...], fgain_ref[...])


def _ffn(x, gain, w_in, w_out, final_gain, *, final_norm, tm=512, tf=512):
    t, d = x.shape
    d_ff = w_out.shape[0]
    nj = d_ff // tf
    assert t % tm == 0 and d_ff % tf == 0 and w_in.shape == (d, 2 * d_ff)
    return pl.pallas_call(
        functools.partial(_ffn_body, final_norm=final_norm),
        grid=(t // tm, nj),
        in_specs=[
            pl.BlockSpec((tm, d), lambda i, j: (i, 0)),
            pl.BlockSpec((1, d), lambda i, j: (0, 0)),
            pl.BlockSpec((d, tf), lambda i, j: (0, j)),
            pl.BlockSpec((d, tf), lambda i, j: (0, j + nj)),
            pl.BlockSpec((tf, d), lambda i, j: (j, 0)),
            pl.BlockSpec((1, d), lambda i, j: (0, 0)),
        ],
        out_specs=pl.BlockSpec((tm, d), lambda i, j: (i, 0)),
        out_shape=jax.ShapeDtypeStruct((t, d), F32),
        scratch_shapes=[pltpu.VMEM((tm, d), BF16)],
        compiler_params=_params(("parallel", "arbitrary")),
        name="ffn_final" if final_norm else "ffn",
    )(x, gain, w_in, w_in, w_out, final_gain)


def _mix_in_body(h_ref, gain_ref, w_ref, wgate_ref, u_ref, gate_ref, n_ref):
    j = pl.program_id(1)

    @pl.when(j == 0)
    def _init():
        n = _rmsnorm(h_ref[...], gain_ref[...]).astype(BF16)
        n_ref[...] = n
        gate_ref[...] = jnp.dot(n, wgate_ref[...], preferred_element_type=F32)

    u_ref[...] = jnp.dot(n_ref[...], w_ref[...], preferred_element_type=F32)


def _mix_in(h, gain, w_main, w_gate, *, tm=512, tn=1024):
    t, d = h.shape
    n_main = w_main.shape[1]
    assert t % tm == 0 and n_main % tn == 0 and w_gate.shape == (d, LANES)
    return pl.pallas_call(
        _mix_in_body,
        grid=(t // tm, n_main // tn),
        in_specs=[
            pl.BlockSpec((tm, d), lambda i, j: (i, 0)),
            pl.BlockSpec((1, d), lambda i, j: (0, 0)),
            pl.BlockSpec((d, tn), lambda i, j: (0, j)),
            pl.BlockSpec((d, LANES), lambda i, j: (0, 0)),
        ],
        out_specs=[
            pl.BlockSpec((tm, tn), lambda i, j: (i, j)),
            pl.BlockSpec((tm, LANES), lambda i, j: (i, 0)),
        ],
        out_shape=[
            jax.ShapeDtypeStruct((t, n_main), F32),
            jax.ShapeDtypeStruct((t, LANES), F32),
        ],
        scratch_shapes=[pltpu.VMEM((tm, d), BF16)],
        compiler_params=_params(("parallel", "arbitrary")),
        name="mix_in",
    )(h, gain, w_main, w_gate)


def _shift_rows(x, k, row):
    return jnp.where(row >= k, pltpu.roll(x, k, axis=0), 0.0)


def _pool_body(u_ref, w_ref, scale_ref, y_ref, *, group_dim):
    s = u_ref.shape[1]
    row = lax.broadcasted_iota(jnp.int32, (s, group_dim), 0)
    rowf = (row + 1).astype(F32)
    for gi, window in enumerate(POOL_WINDOWS):
        cols = slice(gi * group_dim, (gi + 1) * group_dim)
        x = u_ref[0, :, cols]
        acc = x
        span = 1
        while span < window:
            acc = acc + _shift_rows(acc, span, row)
            span *= 2
        cnt = jnp.minimum(rowf, float(window))
        pooled = (acc / cnt - x).astype(BF16)
        y = jnp.dot(pooled, w_ref[gi], preferred_element_type=F32)
        y_ref[0, :, cols] = (y * scale_ref[:, cols]).astype(y_ref.dtype)


def _pool(u3, w_pool, pool_scale, *, d_pool):
    b, s, _ = u3.shape
    group_dim = d_pool // N_POOL_GROUPS
    return pl.pallas_call(
        functools.partial(_pool_body, group_dim=group_dim),
        grid=(b,),
        in_specs=[
            pl.BlockSpec((1, s, d_pool), lambda i: (i, 0, 0)),
            pl.BlockSpec((N_POOL_GROUPS, group_dim, group_dim), lambda i: (0, 0, 0)),
            pl.BlockSpec((1, d_pool), lambda i: (0, 0)),
        ],
        out_specs=pl.BlockSpec((1, s, d_pool), lambda i: (i, 0, 0)),
        out_shape=jax.ShapeDtypeStruct((b, s, d_pool), BF16),
        compiler_params=_params(("parallel",)),
        name="pool",
    )(u3, w_pool, pool_scale)


def _gla_body(q_ref, k_ref, v_ref, g_ref, gate_ref, wa_ref, ba_ref, gn_ref, y_ref,
              qd_ref, ki_ref, kt_ref, bc_ref, st_ref, *, dk):
    s = q_ref.shape[1]
    n_chunks = s // CHUNK

    logits = jnp.dot(gate_ref[0].astype(BF16), wa_ref[...], preferred_element_type=F32) + ba_ref[...]
    log_alpha = (jnp.minimum(logits, 0.0) - jnp.log1p(jnp.exp(-jnp.abs(logits)))) * (
        1.0 / GATE_LOGIT_NORMALIZER)

    pos = lax.broadcasted_iota(jnp.int32, (s, dk), 0) & (CHUNK - 1)
    bcum = log_alpha
    span = 1
    while span < CHUNK:
        bcum = bcum + jnp.where(pos >= span, pltpu.roll(bcum, span, axis=0), 0.0)
        span *= 2
    bc_ref[...] = bcum
    b3 = bcum.reshape(n_chunks, CHUNK, dk)
    b_last = jnp.broadcast_to(b3[:, CHUNK - 1:CHUNK, :], b3.shape).reshape(s, dk)

    q = q_ref[0]
    k = k_ref[0]
    qd_ref[...] = (q * (dk ** -0.5) * jnp.exp(bcum)).astype(BF16)
    ki_ref[...] = (k * jnp.exp(-bcum)).astype(BF16)
    kt_ref[...] = (k * jnp.exp(b_last - bcum)).astype(BF16)

    st_ref[...] = jnp.zeros_like(st_ref)
    tril = (lax.broadcasted_iota(jnp.int32, (CHUNK, CHUNK), 0)
            >= lax.broadcasted_iota(jnp.int32, (CHUNK, CHUNK), 1))
    nt_dims = (((1,), (1,)), ((), ()))
    tn_dims = (((0,), (0,)), ((), ()))

    def chunk_step(n, carry):
        rows = pl.ds(pl.multiple_of(n * CHUNK, CHUNK), CHUNK)
        qd = qd_ref[rows, :]
        v = v_ref[0, rows, :].astype(BF16)
        scores = lax.dot_general(qd, ki_ref[rows, :], nt_dims, preferred_element_type=F32)
        scores = jnp.where(tril, scores, 0.0).astype(BF16)
        state_t = st_ref[...]
        o = jnp.dot(scores, v, preferred_element_type=F32)
        o = o + lax.dot_general(qd, state_t.astype(BF16), nt_dims, preferred_element_type=F32)
        kv_t = lax.dot_general(v, kt_ref[rows, :], tn_dims, preferred_element_type=F32)
        decay = jnp.exp(bc_ref[pl.ds(n * CHUNK + (CHUNK - 1), 1), :])
        st_ref[...] = state_t * decay + kv_t
        o = _rmsnorm(o, gn_ref[...])
        y_ref[0, rows, :] = (o * _silu(g_ref[0, rows, :])).astype(y_ref.dtype)
        return carry

    lax.fori_loop(0, n_chunks, chunk_step, 0, unroll=2)


def _gla(u3, gate3, w_alpha, b_alpha, gla_norm, *, d_pool, dk, dv):
    b, s, _ = u3.shape
    h = GLA_HEADS
    q_blk = d_pool // dk
    k_blk = q_blk + h
    v_blk = (d_pool + 2 * h * dk) // dv
    g_blk = v_blk + h
    return pl.pallas_call(
        functools.partial(_gla_body, dk=dk),
        grid=(b, h),
        in_specs=[
            pl.BlockSpec((1, s, dk), lambda i, j: (i, 0, q_blk + j)),
            pl.BlockSpec((1, s, dk), lambda i, j: (i, 0, k_blk + j)),
            pl.BlockSpec((1, s, dv), lambda i, j: (i, 0, v_blk + j)),
            pl.BlockSpec((1, s, dv), lambda i, j: (i, 0, g_blk + j)),
            pl.BlockSpec((1, s, LANES), lambda i, j: (i, 0, 0)),
            pl.BlockSpec((LANES, dk), lambda i, j: (0, j)),
            pl.BlockSpec((1, dk), lambda i, j: (0, j)),
            pl.BlockSpec((1, dv), lambda i, j: (0, 0)),
        ],
        out_specs=pl.BlockSpec((1, s, dv), lambda i, j: (i, 0, j)),
        out_shape=jax.ShapeDtypeStruct((b, s, h * dv), BF16),
        scratch_shapes=[
            pltpu.VMEM((s, dk), BF16),
            pltpu.VMEM((s, dk), BF16),
            pltpu.VMEM((s, dk), BF16),
            pltpu.VMEM((s, dk), F32),
            pltpu.VMEM((dv, dk), F32),
        ],
        compiler_params=_params(("parallel", "parallel")),
        name="gla",
    )(u3, u3, u3, u3, gate3, w_alpha, b_alpha, gla_norm)


def _mix_out_body(h_ref, yp_ref, yg_ref, wp_ref, wg_ref, o_ref):
    acc = jnp.dot(yp_ref[...], wp_ref[...], preferred_element_type=F32)
    acc = acc + jnp.dot(yg_ref[...], wg_ref[...], preferred_element_type=F32)
    o_ref[...] = h_ref[...] + acc


def _mix_out(h, y_pool, y_gla, w_out, *, tm=512, tn=1024):
    t, d = h.shape
    d_pool = y_pool.shape[1]
    d_gla = y_gla.shape[1]
    assert t % tm == 0 and d % tn == 0 and d_pool % tn == 0
    gla_row_blk = d_pool // d_gla if d_pool % d_gla == 0 else None
    assert gla_row_blk is not None
    return pl.pallas_call(
        _mix_out_body,
        grid=(t // tm, d // tn),
        in_specs=[
            pl.BlockSpec((tm, tn), lambda i, j: (i, j)),
            pl.BlockSpec((tm, d_pool), lambda i, j: (i, 0)),
            pl.BlockSpec((tm, d_gla), lambda i, j: (i, 0)),
            pl.BlockSpec((d_pool, tn), lambda i, j: (0, j)),
            pl.BlockSpec((d_gla, tn), lambda i, j: (gla_row_blk, j)),
        ],
        out_specs=pl.BlockSpec((tm, tn), lambda i, j: (i, j)),
        out_shape=jax.ShapeDtypeStruct((t, d), F32),
        compiler_params=_params(("parallel", "arbitrary")),
        name="mix_out",
    )(h, y_pool, y_gla, w_out, w_out)


def kernel(x, ffn1_norm, ffn1_w_in, ffn1_w_out, mix_norm, w_in_mix, w_pool, pool_scale,
           w_alpha, b_alpha, gla_norm, w_out_mix, ffn2_norm, ffn2_w_in, ffn2_w_out, final_norm):
    b, s, d = x.shape
    depth = ffn1_norm.shape[0]
    d_pool = pool_scale.shape[1]
    dk_total = w_alpha.shape[2]
    dk = dk_total // GLA_HEADS
    dv = gla_norm.shape[1]
    d_gla = GLA_HEADS * dv
    n_main = d_pool + 2 * dk_total + 2 * d_gla
    assert w_in_mix.shape[2] == n_main + GLA_GATE_RANK and s % CHUNK == 0 and depth >= 1

    h = x.reshape(b * s, d)
    final_gain = final_norm.reshape(1, d)
    for l in range(depth):
        last = l == depth - 1
        h = _ffn(h, ffn1_norm[l].reshape(1, d), ffn1_w_in[l].astype(BF16),
                 ffn1_w_out[l].astype(BF16), final_gain, final_norm=False)

        w_mix = w_in_mix[l]
        w_gate = jnp.pad(w_mix[:, n_main:], ((0, 0), (0, LANES - GLA_GATE_RANK))).astype(BF16)
        u, gate = _mix_in(h, mix_norm[l].reshape(1, d), w_mix[:, :n_main].astype(BF16), w_gate)
        u3 = u.reshape(b, s, n_main)
        gate3 = gate.reshape(b, s, LANES)

        y_pool = _pool(u3, w_pool[l].astype(BF16), pool_scale[l].reshape(1, d_pool), d_pool=d_pool)
        w_alpha_pad = jnp.pad(w_alpha[l], ((0, LANES - GLA_GATE_RANK), (0, 0))).astype(BF16)
        y_gla = _gla(u3, gate3, w_alpha_pad, b_alpha[l].reshape(1, dk_total),
                     gla_norm[l].reshape(1, dv), d_pool=d_pool, dk=dk, dv=dv)

        h = _mix_out(h, y_pool.reshape(b * s, d_pool), y_gla.reshape(b * s, d_gla),
                     w_out_mix[l].astype(BF16))
        h = _ffn(h, ffn2_norm[l].reshape(1, d), ffn2_w_in[l].astype(BF16),
                 ffn2_w_out[l].astype(BF16), final_gain, final_norm=last)
    return h.reshape(b, s, d)
```

```python
import functools

import jax
import jax.numpy as jnp
from jax import lax
from jax.experimental import pallas as pl
from jax.experimental.pallas import tpu as pltpu

F32 = jnp.float32
BF16 = jnp.bfloat16

EPS = 1e-6
POOL_WINDOWS = (2, 4, 8, 16)
N_POOL_GROUPS = len(POOL_WINDOWS)
GLA_HEADS = 4
GLA_GATE_RANK = 16
GATE_LOGIT_NORMALIZER = 16.0
CHUNK = 64

LANES = 128
VMEM_LIMIT_BYTES = 56 * 1024 * 1024


def _rmsnorm(x, gain):
    ms = jnp.mean(x * x, axis=-1, keepdims=True)
    return x * lax.rsqrt(ms + EPS) * gain


def _silu(x):
    return x * (1.0 / (1.0 + jnp.exp(-x)))


def _params(semantics):
    return pltpu.CompilerParams(dimension_semantics=semantics, vmem_limit_bytes=VMEM_LIMIT_BYTES)


def _ffn_body(x_ref, gain_ref, wg_ref, wu_ref, wo_ref, fgain_ref, o_ref, n_ref, *, final_norm):
    j = pl.program_id(1)

    @pl.when(j == 0)
    def _init():
        x = x_ref[...]
        n_ref[...] = _rmsnorm(x, gain_ref[...]).astype(BF16)
        o_ref[...] = x

    n = n_ref[...]
    gate = jnp.dot(n, wg_ref[...], preferred_element_type=F32)
    up = jnp.dot(n, wu_ref[...], preferred_element_type=F32)
    act = (0.5 * _silu(gate) * up).astype(BF16)
    o_ref[...] += jnp.dot(act, wo_ref[...], preferred_element_type=F32)

    if final_norm:

        @pl.when(j == pl.num_programs(1) - 1)
        def _finish():
            o_ref[...] = _rmsnorm(o_ref[...], fgain_ref[...])


def _ffn(x, gain, w_in, w_out, final_gain, *, final_norm, tm=1024, tf=512):
    t, d = x.shape
    d_ff = w_out.shape[0]
    nj = d_ff // tf
    assert t % tm == 0 and d_ff % tf == 0 and w_in.shape == (d, 2 * d_ff)
    return pl.pallas_call(
        functools.partial(_ffn_body, final_norm=final_norm),
        grid=(t // tm, nj),
        in_specs=[
            pl.BlockSpec((tm, d), lambda i, j: (i, 0)),
            pl.BlockSpec((1, d), lambda i, j: (0, 0)),
            pl.BlockSpec((d, tf), lambda i, j: (0, j)),
            pl.BlockSpec((d, tf), lambda i, j: (0, j + nj)),
            pl.BlockSpec((tf, d), lambda i, j: (j, 0)),
            pl.BlockSpec((1, d), lambda i, j: (0, 0)),
        ],
        out_specs=pl.BlockSpec((tm, d), lambda i, j: (i, 0)),
        out_shape=jax.ShapeDtypeStruct((t, d), F32),
        scratch_shapes=[pltpu.VMEM((tm, d), BF16)],
        compiler_params=_params(("parallel", "arbitrary")),
        name="ffn_final" if final_norm else "ffn",
    )(x, gain, w_in, w_in, w_out, final_gain)


def _mix_in_body(h_ref, gain_ref, w_ref, wgate_ref, u_ref, gate_ref, n_ref):
    j = pl.program_id(1)

    @pl.when(j == 0)
    def _init():
        n = _rmsnorm(h_ref[...], gain_ref[...]).astype(BF16)
        n_ref[...] = n
        gate_ref[...] = jnp.dot(n, wgate_ref[...], preferred_element_type=F32)

    u_ref[...] = jnp.dot(n_ref[...], w_ref[...], preferred_element_type=F32)


def _mix_in(h, gain, w_main, w_gate, *, tm=1024, tn=1024):
    t, d = h.shape
    n_main = w_main.shape[1]
    assert t % tm == 0 and n_main % tn == 0 and w_gate.shape == (d, LANES)
    return pl.pallas_call(
        _mix_in_body,
        grid=(t // tm, n_main // tn),
        in_specs=[
            pl.BlockSpec((tm, d), lambda i, j: (i, 0)),
            pl.BlockSpec((1, d), lambda i, j: (0, 0)),
            pl.BlockSpec((d, tn), lambda i, j: (0, j)),
            pl.BlockSpec((d, LANES), lambda i, j: (0, 0)),
        ],
        out_specs=[
            pl.BlockSpec((tm, tn), lambda i, j: (i, j)),
            pl.BlockSpec((tm, LANES), lambda i, j: (i, 0)),
        ],
        out_shape=[
            jax.ShapeDtypeStruct((t, n_main), F32),
            jax.ShapeDtypeStruct((t, LANES), F32),
        ],
        scratch_shapes=[pltpu.VMEM((tm, d), BF16)],
        compiler_params=_params(("parallel", "arbitrary")),
        name="mix_in",
    )(h, gain, w_main, w_gate)


def _shift_rows(x, k, row):
    return jnp.where(row >= k, pltpu.roll(x, k, axis=0), 0.0)


def _pool_body(u_ref, w_ref, scale_ref, y_ref, *, group_dim):
    s = u_ref.shape[1]
    row = lax.broadcasted_iota(jnp.int32, (s, group_dim), 0)
    rowf = (row + 1).astype(F32)
    for gi, window in enumerate(POOL_WINDOWS):
        cols = slice(gi * group_dim, (gi + 1) * group_dim)
        x = u_ref[0, :, cols]
        acc = x
        span = 1
        while span < window:
            acc = acc + _shift_rows(acc, span, row)
            span *= 2
        cnt = jnp.minimum(rowf, float(window))
        pooled = (acc / cnt - x).astype(BF16)
        y = jnp.dot(pooled, w_ref[gi], preferred_element_type=F32)
        y_ref[0, :, cols] = (y * scale_ref[:, cols]).astype(y_ref.dtype)


def _pool(u3, w_pool, pool_scale, *, d_pool):
    b, s, _ = u3.shape
    group_dim = d_pool // N_POOL_GROUPS
    return pl.pallas_call(
        functools.partial(_pool_body, group_dim=group_dim),
        grid=(b,),
        in_specs=[
            pl.BlockSpec((1, s, d_pool), lambda i: (i, 0, 0)),
            pl.BlockSpec((N_POOL_GROUPS, group_dim, group_dim), lambda i: (0, 0, 0)),
            pl.BlockSpec((1, d_pool), lambda i: (0, 0)),
        ],
        out_specs=pl.BlockSpec((1, s, d_pool), lambda i: (i, 0, 0)),
        out_shape=jax.ShapeDtypeStruct((b, s, d_pool), BF16),
        compiler_params=_params(("parallel",)),
        name="pool",
    )(u3, w_pool, pool_scale)


def _gla_body(q_ref, k_ref, v_ref, g_ref, gate_ref, wa_ref, ba_ref, gn_ref, y_ref,
              qd_ref, ki_ref, kt_ref, bc_ref, st_ref, *, dk):
    s = q_ref.shape[1]
    n_chunks = s // CHUNK

    logits = jnp.dot(gate_ref[0].astype(BF16), wa_ref[...], preferred_element_type=F32) + ba_ref[...]
    log_alpha = (jnp.minimum(logits, 0.0) - jnp.log1p(jnp.exp(-jnp.abs(logits)))) * (
        1.0 / GATE_LOGIT_NORMALIZER)

    pos = lax.broadcasted_iota(jnp.int32, (s, dk), 0) & (CHUNK - 1)
    bcum = log_alpha
    span = 1
    while span < CHUNK:
        bcum = bcum + jnp.where(pos >= span, pltpu.roll(bcum, span, axis=0), 0.0)
        span *= 2
    bc_ref[...] = bcum
    b3 = bcum.reshape(n_chunks, CHUNK, dk)
    b_last = jnp.broadcast_to(b3[:, CHUNK - 1:CHUNK, :], b3.shape).reshape(s, dk)

    q = q_ref[0]
    k = k_ref[0]
    qd_ref[...] = (q * (dk ** -0.5) * jnp.exp(bcum)).astype(BF16)
    ki_ref[...] = (k * jnp.exp(-bcum)).astype(BF16)
    kt_ref[...] = (k * jnp.exp(b_last - bcum)).astype(BF16)

    st_ref[...] = jnp.zeros_like(st_ref)
    tril = (lax.broadcasted_iota(jnp.int32, (CHUNK, CHUNK), 0)
            >= lax.broadcasted_iota(jnp.int32, (CHUNK, CHUNK), 1))
    nt_dims = (((1,), (1,)), ((), ()))
    tn_dims = (((0,), (0,)), ((), ()))

    def chunk_step(n, carry):
        rows = pl.ds(pl.multiple_of(n * CHUNK, CHUNK), CHUNK)
        qd = qd_ref[rows, :]
        v = v_ref[0, rows, :].astype(BF16)
        scores = lax.dot_general(qd, ki_ref[rows, :], nt_dims, preferred_element_type=F32)
        scores = jnp.where(tril, scores, 0.0).astype(BF16)
        state_t = st_ref[...]
        o = jnp.dot(scores, v, preferred_element_type=F32)
        o = o + lax.dot_general(qd, state_t.astype(BF16), nt_dims, preferred_element_type=F32)
        kv_t = lax.dot_general(v, kt_ref[rows, :], tn_dims, preferred_element_type=F32)
        decay = jnp.exp(bc_ref[pl.ds(n * CHUNK + (CHUNK - 1), 1), :])
        st_ref[...] = state_t * decay + kv_t
        o = _rmsnorm(o, gn_ref[...])
        y_ref[0, rows, :] = (o * _silu(g_ref[0, rows, :])).astype(y_ref.dtype)
        return carry

    lax.fori_loop(0, n_chunks, chunk_step, 0, unroll=2)


def _gla(u3, gate3, w_alpha, b_alpha, gla_norm, *, d_pool, dk, dv):
    b, s, _ = u3.shape
    h = GLA_HEADS
    q_blk = d_pool // dk
    k_blk = q_blk + h
    v_blk = (d_pool + 2 * h * dk) // dv
    g_blk = v_blk + h
    return pl.pallas_call(
        functools.partial(_gla_body, dk=dk),
        grid=(b, h),
        in_specs=[
            pl.BlockSpec((1, s, dk), lambda i, j: (i, 0, q_blk + j)),
            pl.BlockSpec((1, s, dk), lambda i, j: (i, 0, k_blk + j)),
            pl.BlockSpec((1, s, dv), lambda i, j: (i, 0, v_blk + j)),
            pl.BlockSpec((1, s, dv), lambda i, j: (i, 0, g_blk + j)),
            pl.BlockSpec((1, s, LANES), lambda i, j: (i, 0, 0)),
            pl.BlockSpec((LANES, dk), lambda i, j: (0, j)),
            pl.BlockSpec((1, dk), lambda i, j: (0, j)),
            pl.BlockSpec((1, dv), lambda i, j: (0, 0)),
        ],
        out_specs=pl.BlockSpec((1, s, dv), lambda i, j: (i, 0, j)),
        out_shape=jax.ShapeDtypeStruct((b, s, h * dv), BF16),
        scratch_shapes=[
            pltpu.VMEM((s, dk), BF16),
            pltpu.VMEM((s, dk), BF16),
            pltpu.VMEM((s, dk), BF16),
            pltpu.VMEM((s, dk), F32),
            pltpu.VMEM((dv, dk), F32),
        ],
        compiler_params=_params(("parallel", "parallel")),
        name="gla",
    )(u3, u3, u3, u3, gate3, w_alpha, b_alpha, gla_norm)


def _mix_out_body(h_ref, yp_ref, yg_ref, wp_ref, wg_ref, o_ref):
    acc = jnp.dot(yp_ref[...], wp_ref[...], preferred_element_type=F32)
    acc = acc + jnp.dot(yg_ref[...], wg_ref[...], preferred_element_type=F32)
    o_ref[...] = h_ref[...] + acc


def _mix_out(h, y_pool, y_gla, w_out, *, tm=1024, tn=1024):
    t, d = h.shape
    d_pool = y_pool.shape[1]
    d_gla = y_gla.shape[1]
    assert t % tm == 0 and d % tn == 0 and d_pool % tn == 0
    gla_row_blk = d_pool // d_gla if d_pool % d_gla == 0 else None
    assert gla_row_blk is not None
    return pl.pallas_call(
        _mix_out_body,
        grid=(t // tm, d // tn),
        in_specs=[
            pl.BlockSpec((tm, tn), lambda i, j: (i, j)),
            pl.BlockSpec((tm, d_pool), lambda i, j: (i, 0)),
            pl.BlockSpec((tm, d_gla), lambda i, j: (i, 0)),
            pl.BlockSpec((d_pool, tn), lambda i, j: (0, j)),
            pl.BlockSpec((d_gla, tn), lambda i, j: (gla_row_blk, j)),
        ],
        out_specs=pl.BlockSpec((tm, tn), lambda i, j: (i, j)),
        out_shape=jax.ShapeDtypeStruct((t, d), F32),
        compiler_params=_params(("parallel", "arbitrary")),
        name="mix_out",
    )(h, y_pool, y_gla, w_out, w_out)


def kernel(x, ffn1_norm, ffn1_w_in, ffn1_w_out, mix_norm, w_in_mix, w_pool, pool_scale,
           w_alpha, b_alpha, gla_norm, w_out_mix, ffn2_norm, ffn2_w_in, ffn2_w_out, final_norm):
    b, s, d = x.shape
    depth = ffn1_norm.shape[0]
    d_pool = pool_scale.shape[1]
    dk_total = w_alpha.shape[2]
    dk = dk_total // GLA_HEADS
    dv = gla_norm.shape[1]
    d_gla = GLA_HEADS * dv
    n_main = d_pool + 2 * dk_total + 2 * d_gla
    assert w_in_mix.shape[2] == n_main + GLA_GATE_RANK and s % CHUNK == 0 and depth >= 1

    h = x.reshape(b * s, d)
    final_gain = final_norm.reshape(1, d)
    for l in range(depth):
        last = l == depth - 1
        h = _ffn(h, ffn1_norm[l].reshape(1, d), ffn1_w_in[l].astype(BF16),
                 ffn1_w_out[l].astype(BF16), final_gain, final_norm=False)

        w_mix = w_in_mix[l]
        w_gate = jnp.pad(w_mix[:, n_main:], ((0, 0), (0, LANES - GLA_GATE_RANK))).astype(BF16)
        u, gate = _mix_in(h, mix_norm[l].reshape(1, d), w_mix[:, :n_main].astype(BF16), w_gate)
        u3 = u.reshape(b, s, n_main)
        gate3 = gate.reshape(b, s, LANES)

        y_pool = _pool(u3, w_pool[l].astype(BF16), pool_scale[l].reshape(1, d_pool), d_pool=d_pool)
        w_alpha_pad = jnp.pad(w_alpha[l], ((0, LANES - GLA_GATE_RANK), (0, 0))).astype(BF16)
        y_gla = _gla(u3, gate3, w_alpha_pad, b_alpha[l].reshape(1, dk_total),
                     gla_norm[l].reshape(1, dv), d_pool=d_pool, dk=dk, dv=dv)

        h = _mix_out(h, y_pool.reshape(b * s, d_pool), y_gla.reshape(b * s, d_gla),
                     w_out_mix[l].astype(BF16))
        h = _ffn(h, ffn2_norm[l].reshape(1, d), ffn2_w_in[l].astype(BF16),
                 ffn2_w_out[l].astype(BF16), final_gain, final_norm=last)
    return h.reshape(b, s, d)
```

```python
import functools

import jax
import jax.numpy as jnp
from jax import lax
from jax.experimental import pallas as pl
from jax.experimental.pallas import tpu as pltpu

F32 = jnp.float32
BF16 = jnp.bfloat16

EPS = 1e-6
POOL_WINDOWS = (2, 4, 8, 16)
N_POOL_GROUPS = len(POOL_WINDOWS)
GLA_HEADS = 4
GLA_GATE_RANK = 16
GATE_LOGIT_NORMALIZER = 16.0
CHUNK = 64

LANES = 128
VMEM_LIMIT_BYTES = 56 * 1024 * 1024


def _rmsnorm(x, gain):
    ms = jnp.mean(x * x, axis=-1, keepdims=True)
    return x * lax.rsqrt(ms + EPS) * gain


def _silu(x):
    return x * (1.0 / (1.0 + jnp.exp(-x)))


def _params(semantics):
    return pltpu.CompilerParams(dimension_semantics=semantics, vmem_limit_bytes=VMEM_LIMIT_BYTES)


def _ffn_body(x_ref, gain_ref, wg_ref, wu_ref, wo_ref, fgain_ref, o_ref, n_ref, *, final_norm):
    j = pl.program_id(1)

    @pl.when(j == 0)
    def _init():
        x = x_ref[...]
        n_ref[...] = _rmsnorm(x, gain_ref[...]).astype(BF16)
        o_ref[...] = x

    n = n_ref[...]
    gate = jnp.dot(n, wg_ref[...].astype(BF16), preferred_element_type=F32)
    up = jnp.dot(n, wu_ref[...].astype(BF16), preferred_element_type=F32)
    act = (0.5 * _silu(gate) * up).astype(BF16)
    o_ref[...] += jnp.dot(act, wo_ref[...].astype(BF16), preferred_element_type=F32)

    if final_norm:

        @pl.when(j == pl.num_programs(1) - 1)
        def _finish():
            o_ref[...] = _rmsnorm(o_ref[...], fgain_ref[...])


def _ffn(x, gain, w_in, w_out, final_gain, *, final_norm, tm=1024, tf=256):
    t, d = x.shape
    d_ff = w_out.shape[0]
    nj = d_ff // tf
    assert t % tm == 0 and d_ff % tf == 0 and w_in.shape == (d, 2 * d_ff)
    return pl.pallas_call(
        functools.partial(_ffn_body, final_norm=final_norm),
        grid=(t // tm, nj),
        in_specs=[
            pl.BlockSpec((tm, d), lambda i, j: (i, 0)),
            pl.BlockSpec((1, d), lambda i, j: (0, 0)),
            pl.BlockSpec((d, tf), lambda i, j: (0, j)),
            pl.BlockSpec((d, tf), lambda i, j: (0, j + nj)),
            pl.BlockSpec((tf, d), lambda i, j: (j, 0)),
            pl.BlockSpec((1, d), lambda i, j: (0, 0)),
        ],
        out_specs=pl.BlockSpec((tm, d), lambda i, j: (i, 0)),
        out_shape=jax.ShapeDtypeStruct((t, d), F32),
        scratch_shapes=[pltpu.VMEM((tm, d), BF16)],
        compiler_params=_params(("parallel", "arbitrary")),
        name="ffn_final" if final_norm else "ffn",
    )(x, gain, w_in, w_in, w_out, final_gain)


def _mix_in_body(h_ref, gain_ref, w_ref, wgate_ref, u_ref, gate_ref, n_ref):
    j = pl.program_id(1)

    @pl.when(j == 0)
    def _init():
        n = _rmsnorm(h_ref[...], gain_ref[...]).astype(BF16)
        n_ref[...] = n
        gate_ref[...] = jnp.dot(n, wgate_ref[...], preferred_element_type=F32)

    u_ref[...] = jnp.dot(n_ref[...], w_ref[...], preferred_element_type=F32)


def _mix_in(h, gain, w_main, w_gate, *, tm=1024, tn=1024):
    t, d = h.shape
    n_main = w_main.shape[1]
    assert t % tm == 0 and n_main % tn == 0 and w_gate.shape == (d, LANES)
    return pl.pallas_call(
        _mix_in_body,
        grid=(t // tm, n_main // tn),
        in_specs=[
            pl.BlockSpec((tm, d), lambda i, j: (i, 0)),
            pl.BlockSpec((1, d), lambda i, j: (0, 0)),
            pl.BlockSpec((d, tn), lambda i, j: (0, j)),
            pl.BlockSpec((d, LANES), lambda i, j: (0, 0)),
        ],
        out_specs=[
            pl.BlockSpec((tm, tn), lambda i, j: (i, j)),
            pl.BlockSpec((tm, LANES), lambda i, j: (i, 0)),
        ],
        out_shape=[
            jax.ShapeDtypeStruct((t, n_main), F32),
            jax.ShapeDtypeStruct((t, LANES), F32),
        ],
        scratch_shapes=[pltpu.VMEM((tm, d), BF16)],
        compiler_params=_params(("parallel", "arbitrary")),
        name="mix_in",
    )(h, gain, w_main, w_gate)


def _shift_rows(x, k, row):
    return jnp.where(row >= k, pltpu.roll(x, k, axis=0), 0.0)


def _pool_body(u_ref, w_ref, scale_ref, y_ref, *, group_dim):
    s = u_ref.shape[1]
    row = lax.broadcasted_iota(jnp.int32, (s, group_dim), 0)
    rowf = (row + 1).astype(F32)
    for gi, window in enumerate(POOL_WINDOWS):
        cols = slice(gi * group_dim, (gi + 1) * group_dim)
        x = u_ref[0, :, cols]
        acc = x
        span = 1
        while span < window:
            acc = acc + _shift_rows(acc, span, row)
            span *= 2
        cnt = jnp.minimum(rowf, float(window))
        pooled = (acc / cnt - x).astype(BF16)
        y = jnp.dot(pooled, w_ref[gi], preferred_element_type=F32)
        y_ref[0, :, cols] = (y * scale_ref[:, cols]).astype(y_ref.dtype)


def _pool(u3, w_pool, pool_scale, *, d_pool):
    b, s, _ = u3.shape
    group_dim = d_pool // N_POOL_GROUPS
    return pl.pallas_call(
        functools.partial(_pool_body, group_dim=group_dim),
        grid=(b,),
        in_specs=[
            pl.BlockSpec((1, s, d_pool), lambda i: (i, 0, 0)),
            pl.BlockSpec((N_POOL_GROUPS, group_dim, group_dim), lambda i: (0, 0, 0)),
            pl.BlockSpec((1, d_pool), lambda i: (0, 0)),
        ],
        out_specs=pl.BlockSpec((1, s, d_pool), lambda i: (i, 0, 0)),
        out_shape=jax.ShapeDtypeStruct((b, s, d_pool), BF16),
        compiler_params=_params(("parallel",)),
        name="pool",
    )(u3, w_pool, pool_scale)


def _gla_body(q_ref, k_ref, v_ref, g_ref, gate_ref, wa_ref, ba_ref, gn_ref, y_ref,
              qd_ref, ki_ref, kt_ref, bc_ref, st_ref, *, dk):
    s = q_ref.shape[1]
    n_chunks = s // CHUNK

    logits = jnp.dot(gate_ref[0].astype(BF16), wa_ref[...], preferred_element_type=F32) + ba_ref[...]
    log_alpha = (jnp.minimum(logits, 0.0) - jnp.log1p(jnp.exp(-jnp.abs(logits)))) * (
        1.0 / GATE_LOGIT_NORMALIZER)

    pos = lax.broadcasted_iota(jnp.int32, (s, dk), 0) & (CHUNK - 1)
    bcum = log_alpha
    span = 1
    while span < CHUNK:
        bcum = bcum + jnp.where(pos >= span, pltpu.roll(bcum, span, axis=0), 0.0)
        span *= 2
    bc_ref[...] = bcum
    b3 = bcum.reshape(n_chunks, CHUNK, dk)
    b_last = jnp.broadcast_to(b3[:, CHUNK - 1:CHUNK, :], b3.shape).reshape(s, dk)

    q = q_ref[0]
    k = k_ref[0]
    qd_ref[...] = (q * (dk ** -0.5) * jnp.exp(bcum)).astype(BF16)
    ki_ref[...] = (k * jnp.exp(-bcum)).astype(BF16)
    kt_ref[...] = (k * jnp.exp(b_last - bcum)).astype(BF16)

    st_ref[...] = jnp.zeros_like(st_ref)
    tril = (lax.broadcasted_iota(jnp.int32, (CHUNK, CHUNK), 0)
            >= lax.broadcasted_iota(jnp.int32, (CHUNK, CHUNK), 1))
    nt_dims = (((1,), (1,)), ((), ()))
    tn_dims = (((0,), (0,)), ((), ()))

    def chunk_step(n, carry):
        rows = pl.ds(pl.multiple_of(n * CHUNK, CHUNK), CHUNK)
        qd = qd_ref[rows, :]
        v = v_ref[0, rows, :].astype(BF16)
        scores = lax.dot_general(qd, ki_ref[rows, :], nt_dims, preferred_element_type=F32)
        scores = jnp.where(tril, scores, 0.0).astype(BF16)
        state_t = st_ref[...]
        o = jnp.dot(scores, v, preferred_element_type=F32)
        o = o + lax.dot_general(qd, state_t.astype(BF16), nt_dims, preferred_element_type=F32)
        kv_t = lax.dot_general(v, kt_ref[rows, :], tn_dims, preferred_element_type=F32)
        decay = jnp.exp(bc_ref[pl.ds(n * CHUNK + (CHUNK - 1), 1), :])
        st_ref[...] = state_t * decay + kv_t
        o = _rmsnorm(o, gn_ref[...])
        y_ref[0, rows, :] = (o * _silu(g_ref[0, rows, :])).astype(y_ref.dtype)
        return carry

    lax.fori_loop(0, n_chunks, chunk_step, 0, unroll=2)


def _gla(u3, gate3, w_alpha, b_alpha, gla_norm, *, d_pool, dk, dv):
    b, s, _ = u3.shape
    h = GLA_HEADS
    q_blk = d_pool // dk
    k_blk = q_blk + h
    v_blk = (d_pool + 2 * h * dk) // dv
    g_blk = v_blk + h
    return pl.pallas_call(
        functools.partial(_gla_body, dk=dk),
        grid=(b, h),
        in_specs=[
            pl.BlockSpec((1, s, dk), lambda i, j: (i, 0, q_blk + j)),
            pl.BlockSpec((1, s, dk), lambda i, j: (i, 0, k_blk + j)),
            pl.BlockSpec((1, s, dv), lambda i, j: (i, 0, v_blk + j)),
            pl.BlockSpec((1, s, dv), lambda i, j: (i, 0, g_blk + j)),
            pl.BlockSpec((1, s, LANES), lambda i, j: (i, 0, 0)),
            pl.BlockSpec((LANES, dk), lambda i, j: (0, j)),
            pl.BlockSpec((1, dk), lambda i, j: (0, j)),
            pl.BlockSpec((1, dv), lambda i, j: (0, 0)),
        ],
        out_specs=pl.BlockSpec((1, s, dv), lambda i, j: (i, 0, j)),
        out_shape=jax.ShapeDtypeStruct((b, s, h * dv), BF16),
        scratch_shapes=[
            pltpu.VMEM((s, dk), BF16),
            pltpu.VMEM((s, dk), BF16),
            pltpu.VMEM((s, dk), BF16),
            pltpu.VMEM((s, dk), F32),
            pltpu.VMEM((dv, dk), F32),
        ],
        compiler_params=_params(("parallel", "parallel")),
        name="gla",
    )(u3, u3, u3, u3, gate3, w_alpha, b_alpha, gla_norm)


def _mix_out_body(h_ref, yp_ref, yg_ref, wp_ref, wg_ref, o_ref):
    acc = jnp.dot(yp_ref[...], wp_ref[...], preferred_element_type=F32)
    acc = acc + jnp.dot(yg_ref[...], wg_ref[...], preferred_element_type=F32)
    o_ref[...] = h_ref[...] + acc


def _mix_out(h, y_pool, y_gla, w_out, *, tm=1024, tn=1024):
    t, d = h.shape
    d_pool = y_pool.shape[1]
    d_gla = y_gla.shape[1]
    assert t % tm == 0 and d % tn == 0 and d_pool % tn == 0
    gla_row_blk = d_pool // d_gla if d_pool % d_gla == 0 else None
    assert gla_row_blk is not None
    return pl.pallas_call(
        _mix_out_body,
        grid=(t // tm, d // tn),
        in_specs=[
            pl.BlockSpec((tm, tn), lambda i, j: (i, j)),
            pl.BlockSpec((tm, d_pool), lambda i, j: (i, 0)),
            pl.BlockSpec((tm, d_gla), lambda i, j: (i, 0)),
            pl.BlockSpec((d_pool, tn), lambda i, j: (0, j)),
            pl.BlockSpec((d_gla, tn), lambda i, j: (gla_row_blk, j)),
        ],
        out_specs=pl.BlockSpec((tm, tn), lambda i, j: (i, j)),
        out_shape=jax.ShapeDtypeStruct((t, d), F32),
        compiler_params=_params(("parallel", "arbitrary")),
        name="mix_out",
    )(h, y_pool, y_gla, w_out, w_out)


def kernel(x, ffn1_norm, ffn1_w_in, ffn1_w_out, mix_norm, w_in_mix, w_pool, pool_scale,
           w_alpha, b_alpha, gla_norm, w_out_mix, ffn2_norm, ffn2_w_in, ffn2_w_out, final_norm):
    b, s, d = x.shape
    depth = ffn1_norm.shape[0]
    d_pool = pool_scale.shape[1]
    dk_total = w_alpha.shape[2]
    dk = dk_total // GLA_HEADS
    dv = gla_norm.shape[1]
    d_gla = GLA_HEADS * dv
    n_main = d_pool + 2 * dk_total + 2 * d_gla
    assert w_in_mix.shape[2] == n_main + GLA_GATE_RANK and s % CHUNK == 0 and depth >= 1

    h = x.reshape(b * s, d)
    final_gain = final_norm.reshape(1, d)
    for l in range(depth):
        last = l == depth - 1
        h = _ffn(h, ffn1_norm[l].reshape(1, d), ffn1_w_in[l],
                 ffn1_w_out[l], final_gain, final_norm=False)

        w_mix = w_in_mix[l]
        w_gate = jnp.pad(w_mix[:, n_main:], ((0, 0), (0, LANES - GLA_GATE_RANK))).astype(BF16)
        u, gate = _mix_in(h, mix_norm[l].reshape(1, d), w_mix[:, :n_main].astype(BF16), w_gate)
        u3 = u.reshape(b, s, n_main)
        gate3 = gate.reshape(b, s, LANES)

        y_pool = _pool(u3, w_pool[l].astype(BF16), pool_scale[l].reshape(1, d_pool), d_pool=d_pool)
        w_alpha_pad = jnp.pad(w_alpha[l], ((0, LANES - GLA_GATE_RANK), (0, 0))).astype(BF16)
        y_gla = _gla(u3, gate3, w_alpha_pad, b_alpha[l].reshape(1, dk_total),
                     gla_norm[l].reshape(1, dv), d_pool=d_pool, dk=dk, dv=dv)

        h = _mix_out(h, y_pool.reshape(b * s, d_pool), y_gla.reshape(b * s, d_gla),
                     w_out_mix[l].astype(BF16))
        h = _ffn(h, ffn2_norm[l].reshape(1, d), ffn2_w_in[l],
                 ffn2_w_out[l], final_gain, final_norm=last)
    return h.reshape(b, s, d)
```

```python
import functools

import jax
import jax.numpy as jnp
from jax import lax
from jax.experimental import pallas as pl
from jax.experimental.pallas import tpu as pltpu

F32 = jnp.float32
BF16 = jnp.bfloat16

EPS = 1e-6
POOL_WINDOWS = (2, 4, 8, 16)
N_POOL_GROUPS = len(POOL_WINDOWS)
GLA_HEADS = 4
GLA_GATE_RANK = 16
GATE_LOGIT_NORMALIZER = 16.0
CHUNK = 64

LANES = 128
VMEM_LIMIT_BYTES = 56 * 1024 * 1024


def _rmsnorm(x, gain):
    ms = jnp.mean(x * x, axis=-1, keepdims=True)
    return x * lax.rsqrt(ms + EPS) * gain


def _silu(x):
    return x * (1.0 / (1.0 + jnp.exp(-x)))


def _params(semantics):
    return pltpu.CompilerParams(dimension_semantics=semantics, vmem_limit_bytes=VMEM_LIMIT_BYTES)


def _ffn_body(x_ref, gain_ref, wg_ref, wu_ref, wo_ref, fgain_ref, o_ref, n_ref, *, final_norm):
    j = pl.program_id(1)

    @pl.when(j == 0)
    def _init():
        x = x_ref[...]
        n_ref[...] = _rmsnorm(x, gain_ref[...]).astype(BF16)
        o_ref[...] = x

    n = n_ref[...]
    gate = jnp.dot(n, wg_ref[...].astype(BF16), preferred_element_type=F32)
    up = jnp.dot(n, wu_ref[...].astype(BF16), preferred_element_type=F32)
    act = (0.5 * _silu(gate) * up).astype(BF16)
    o_ref[...] += jnp.dot(act, wo_ref[...].astype(BF16), preferred_element_type=F32)

    if final_norm:

        @pl.when(j == pl.num_programs(1) - 1)
        def _finish():
            o_ref[...] = _rmsnorm(o_ref[...], fgain_ref[...])


def _ffn(x, gain, w_in, w_out, final_gain, *, final_norm, tm=1024, tf=256):
    t, d = x.shape
    d_ff = w_out.shape[0]
    nj = d_ff // tf
    assert t % tm == 0 and d_ff % tf == 0 and w_in.shape == (d, 2 * d_ff)
    return pl.pallas_call(
        functools.partial(_ffn_body, final_norm=final_norm),
        grid=(t // tm, nj),
        in_specs=[
            pl.BlockSpec((tm, d), lambda i, j: (i, 0)),
            pl.BlockSpec((1, d), lambda i, j: (0, 0)),
            pl.BlockSpec((d, tf), lambda i, j: (0, j)),
            pl.BlockSpec((d, tf), lambda i, j: (0, j + nj)),
            pl.BlockSpec((tf, d), lambda i, j: (j, 0)),
            pl.BlockSpec((1, d), lambda i, j: (0, 0)),
        ],
        out_specs=pl.BlockSpec((tm, d), lambda i, j: (i, 0)),
        out_shape=jax.ShapeDtypeStruct((t, d), F32),
        scratch_shapes=[pltpu.VMEM((tm, d), BF16)],
        compiler_params=_params(("parallel", "arbitrary")),
        name="ffn_final" if final_norm else "ffn",
    )(x, gain, w_in, w_in, w_out, final_gain)


def _mix_in_body(h_ref, gain_ref, w_ref, wgate_ref, u_ref, gate_ref, n_ref):
    j = pl.program_id(1)

    @pl.when(j == 0)
    def _init():
        n = _rmsnorm(h_ref[...], gain_ref[...]).astype(BF16)
        n_ref[...] = n
        gate_ref[...] = jnp.dot(n, wgate_ref[...], preferred_element_type=F32)

    u_ref[...] = jnp.dot(n_ref[...], w_ref[...], preferred_element_type=F32)


def _mix_in(h, gain, w_main, w_gate, *, tm=1024, tn=1024):
    t, d = h.shape
    n_main = w_main.shape[1]
    assert t % tm == 0 and n_main % tn == 0 and w_gate.shape == (d, LANES)
    return pl.pallas_call(
        _mix_in_body,
        grid=(t // tm, n_main // tn),
        in_specs=[
            pl.BlockSpec((tm, d), lambda i, j: (i, 0)),
            pl.BlockSpec((1, d), lambda i, j: (0, 0)),
            pl.BlockSpec((d, tn), lambda i, j: (0, j)),
            pl.BlockSpec((d, LANES), lambda i, j: (0, 0)),
        ],
        out_specs=[
            pl.BlockSpec((tm, tn), lambda i, j: (i, j)),
            pl.BlockSpec((tm, LANES), lambda i, j: (i, 0)),
        ],
        out_shape=[
            jax.ShapeDtypeStruct((t, n_main), F32),
            jax.ShapeDtypeStruct((t, LANES), F32),
        ],
        scratch_shapes=[pltpu.VMEM((tm, d), BF16)],
        compiler_params=_params(("parallel", "arbitrary")),
        name="mix_in",
    )(h, gain, w_main, w_gate)


def _shift_rows(x, k, row):
    return jnp.where(row >= k, pltpu.roll(x, k, axis=0), 0.0)


def _pool_body(u_ref, w_ref, scale_ref, y_ref, *, group_dim):
    s = u_ref.shape[1]
    row = lax.broadcasted_iota(jnp.int32, (s, group_dim), 0)
    rowf = (row + 1).astype(F32)
    for gi, window in enumerate(POOL_WINDOWS):
        cols = slice(gi * group_dim, (gi + 1) * group_dim)
        x = u_ref[0, :, cols]
        acc = x
        span = 1
        while span < window:
            acc = acc + _shift_rows(acc, span, row)
            span *= 2
        cnt = jnp.minimum(rowf, float(window))
        pooled = (acc / cnt - x).astype(BF16)
        y = jnp.dot(pooled, w_ref[gi], preferred_element_type=F32)
        y_ref[0, :, cols] = (y * scale_ref[:, cols]).astype(y_ref.dtype)


def _pool(u3, w_pool, pool_scale, *, d_pool):
    b, s, _ = u3.shape
    group_dim = d_pool // N_POOL_GROUPS
    return pl.pallas_call(
        functools.partial(_pool_body, group_dim=group_dim),
        grid=(b,),
        in_specs=[
            pl.BlockSpec((1, s, d_pool), lambda i: (i, 0, 0)),
            pl.BlockSpec((N_POOL_GROUPS, group_dim, group_dim), lambda i: (0, 0, 0)),
            pl.BlockSpec((1, d_pool), lambda i: (0, 0)),
        ],
        out_specs=pl.BlockSpec((1, s, d_pool), lambda i: (i, 0, 0)),
        out_shape=jax.ShapeDtypeStruct((b, s, d_pool), BF16),
        compiler_params=_params(("parallel",)),
        name="pool",
    )(u3, w_pool, pool_scale)


def _gla_body(q_ref, k_ref, v_ref, g_ref, gate_ref, wa_ref, ba_ref, gn_ref, y_ref,
              qd_ref, ki_ref, kt_ref, bc_ref, st_ref, *, dk):
    s = q_ref.shape[1]
    n_chunks = s // CHUNK

    logits = jnp.dot(gate_ref[0].astype(BF16), wa_ref[...], preferred_element_type=F32) + ba_ref[...]
    log_alpha = (jnp.minimum(logits, 0.0) - jnp.log1p(jnp.exp(-jnp.abs(logits)))) * (
        1.0 / GATE_LOGIT_NORMALIZER)

    pos = lax.broadcasted_iota(jnp.int32, (s, dk), 0) & (CHUNK - 1)
    bcum = log_alpha
    span = 1
    while span < CHUNK:
        bcum = bcum + jnp.where(pos >= span, pltpu.roll(bcum, span, axis=0), 0.0)
        span *= 2
    bc_ref[...] = bcum
    b3 = bcum.reshape(n_chunks, CHUNK, dk)
    b_last = jnp.broadcast_to(b3[:, CHUNK - 1:CHUNK, :], b3.shape).reshape(s, dk)

    q = q_ref[0]
    k = k_ref[0]
    qd_ref[...] = (q * (dk ** -0.5) * jnp.exp(bcum)).astype(BF16)
    ki_ref[...] = (k * jnp.exp(-bcum)).astype(BF16)
    kt_ref[...] = (k * jnp.exp(b_last - bcum)).astype(BF16)

    st_ref[...] = jnp.zeros_like(st_ref)
    tril = (lax.broadcasted_iota(jnp.int32, (CHUNK, CHUNK), 0)
            >= lax.broadcasted_iota(jnp.int32, (CHUNK, CHUNK), 1))
    nt_dims = (((1,), (1,)), ((), ()))
    tn_dims = (((0,), (0,)), ((), ()))

    def chunk_step(n, carry):
        rows = pl.ds(pl.multiple_of(n * CHUNK, CHUNK), CHUNK)
        qd = qd_ref[rows, :]
        v = v_ref[0, rows, :].astype(BF16)
        scores = lax.dot_general(qd, ki_ref[rows, :], nt_dims, preferred_element_type=F32)
        scores = jnp.where(tril, scores, 0.0).astype(BF16)
        state_t = st_ref[...]
        o = jnp.dot(scores, v, preferred_element_type=F32)
        o = o + lax.dot_general(qd, state_t.astype(BF16), nt_dims, preferred_element_type=F32)
        kv_t = lax.dot_general(v, kt_ref[rows, :], tn_dims, preferred_element_type=F32)
        decay = jnp.exp(bc_ref[pl.ds(n * CHUNK + (CHUNK - 1), 1), :])
        st_ref[...] = state_t * decay + kv_t
        o = _rmsnorm(o, gn_ref[...])
        y_ref[0, rows, :] = (o * _silu(g_ref[0, rows, :])).astype(y_ref.dtype)
        return carry

    lax.fori_loop(0, n_chunks, chunk_step, 0, unroll=8)


def _gla(u3, gate3, w_alpha, b_alpha, gla_norm, *, d_pool, dk, dv):
    b, s, _ = u3.shape
    h = GLA_HEADS
    q_blk = d_pool // dk
    k_blk = q_blk + h
    v_blk = (d_pool + 2 * h * dk) // dv
    g_blk = v_blk + h
    return pl.pallas_call(
        functools.partial(_gla_body, dk=dk),
        grid=(b, h),
        in_specs=[
            pl.BlockSpec((1, s, dk), lambda i, j: (i, 0, q_blk + j)),
            pl.BlockSpec((1, s, dk), lambda i, j: (i, 0, k_blk + j)),
            pl.BlockSpec((1, s, dv), lambda i, j: (i, 0, v_blk + j)),
            pl.BlockSpec((1, s, dv), lambda i, j: (i, 0, g_blk + j)),
            pl.BlockSpec((1, s, LANES), lambda i, j: (i, 0, 0)),
            pl.BlockSpec((LANES, dk), lambda i, j: (0, j)),
            pl.BlockSpec((1, dk), lambda i, j: (0, j)),
            pl.BlockSpec((1, dv), lambda i, j: (0, 0)),
        ],
        out_specs=pl.BlockSpec((1, s, dv), lambda i, j: (i, 0, j)),
        out_shape=jax.ShapeDtypeStruct((b, s, h * dv), BF16),
        scratch_shapes=[
            pltpu.VMEM((s, dk), BF16),
            pltpu.VMEM((s, dk), BF16),
            pltpu.VMEM((s, dk), BF16),
            pltpu.VMEM((s, dk), F32),
            pltpu.VMEM((dv, dk), F32),
        ],
        compiler_params=_params(("parallel", "parallel")),
        name="gla",
    )(u3, u3, u3, u3, gate3, w_alpha, b_alpha, gla_norm)


def _mix_out_body(h_ref, yp_ref, yg_ref, wp_ref, wg_ref, o_ref):
    acc = jnp.dot(yp_ref[...], wp_ref[...], preferred_element_type=F32)
    acc = acc + jnp.dot(yg_ref[...], wg_ref[...], preferred_element_type=F32)
    o_ref[...] = h_ref[...] + acc


def _mix_out(h, y_pool, y_gla, w_out, *, tm=1024, tn=1024):
    t, d = h.shape
    d_pool = y_pool.shape[1]
    d_gla = y_gla.shape[1]
    assert t % tm == 0 and d % tn == 0 and d_pool % tn == 0
    gla_row_blk = d_pool // d_gla if d_pool % d_gla == 0 else None
    assert gla_row_blk is not None
    return pl.pallas_call(
        _mix_out_body,
        grid=(t // tm, d // tn),
        in_specs=[
            pl.BlockSpec((tm, tn), lambda i, j: (i, j)),
            pl.BlockSpec((tm, d_pool), lambda i, j: (i, 0)),
            pl.BlockSpec((tm, d_gla), lambda i, j: (i, 0)),
            pl.BlockSpec((d_pool, tn), lambda i, j: (0, j)),
            pl.BlockSpec((d_gla, tn), lambda i, j: (gla_row_blk, j)),
        ],
        out_specs=pl.BlockSpec((tm, tn), lambda i, j: (i, j)),
        out_shape=jax.ShapeDtypeStruct((t, d), F32),
        compiler_params=_params(("parallel", "arbitrary")),
        name="mix_out",
    )(h, y_pool, y_gla, w_out, w_out)


def kernel(x, ffn1_norm, ffn1_w_in, ffn1_w_out, mix_norm, w_in_mix, w_pool, pool_scale,
           w_alpha, b_alpha, gla_norm, w_out_mix, ffn2_norm, ffn2_w_in, ffn2_w_out, final_norm):
    b, s, d = x.shape
    depth = ffn1_norm.shape[0]
    d_pool = pool_scale.shape[1]
    dk_total = w_alpha.shape[2]
    dk = dk_total // GLA_HEADS
    dv = gla_norm.shape[1]
    d_gla = GLA_HEADS * dv
    n_main = d_pool + 2 * dk_total + 2 * d_gla
    assert w_in_mix.shape[2] == n_main + GLA_GATE_RANK and s % CHUNK == 0 and depth >= 1

    h = x.reshape(b * s, d)
    final_gain = final_norm.reshape(1, d)
    for l in range(depth):
        last = l == depth - 1
        h = _ffn(h, ffn1_norm[l].reshape(1, d), ffn1_w_in[l],
                 ffn1_w_out[l], final_gain, final_norm=False)

        w_mix = w_in_mix[l]
        w_gate = jnp.pad(w_mix[:, n_main:], ((0, 0), (0, LANES - GLA_GATE_RANK))).astype(BF16)
        u, gate = _mix_in(h, mix_norm[l].reshape(1, d), w_mix[:, :n_main].astype(BF16), w_gate)
        u3 = u.reshape(b, s, n_main)
        gate3 = gate.reshape(b, s, LANES)

        y_pool = _pool(u3, w_pool[l].astype(BF16), pool_scale[l].reshape(1, d_pool), d_pool=d_pool)
        w_alpha_pad = jnp.pad(w_alpha[l], ((0, LANES - GLA_GATE_RANK), (0, 0))).astype(BF16)
        y_gla = _gla(u3, gate3, w_alpha_pad, b_alpha[l].reshape(1, dk_total),
                     gla_norm[l].reshape(1, dv), d_pool=d_pool, dk=dk, dv=dv)

        h = _mix_out(h, y_pool.reshape(b * s, d_pool), y_gla.reshape(b * s, d_gla),
                     w_out_mix[l].astype(BF16))
        h = _ffn(h, ffn2_norm[l].reshape(1, d), ffn2_w_in[l],
                 ffn2_w_out[l], final_gain, final_norm=last)
    return h.reshape(b, s, d)
```

```python
import functools

import jax
import jax.numpy as jnp
from jax import lax
from jax.experimental import pallas as pl
from jax.experimental.pallas import tpu as pltpu

F32 = jnp.float32
BF16 = jnp.bfloat16

EPS = 1e-6
POOL_WINDOWS = (2, 4, 8, 16)
N_POOL_GROUPS = len(POOL_WINDOWS)
POOL_HALO = 16
GLA_HEADS = 4
GLA_GATE_RANK = 16
GATE_LOGIT_NORMALIZER = 16.0
CHUNK = 64

LANES = 128
VMEM_LIMIT_BYTES = 56 * 1024 * 1024


def _rmsnorm(x, gain):
    ms = jnp.mean(x * x, axis=-1, keepdims=True)
    return x * lax.rsqrt(ms + EPS) * gain


def _silu(x):
    return x * (1.0 / (1.0 + jnp.exp(-x)))


def _log_sigmoid(z):
    return jnp.minimum(z, 0.0) - jnp.log1p(jnp.exp(-jnp.abs(z)))


def _dot(a, b):
    return jnp.dot(a, b, preferred_element_type=F32)


def _params(semantics):
    return pltpu.CompilerParams(dimension_semantics=semantics, vmem_limit_bytes=VMEM_LIMIT_BYTES)


def _ffn_body(x_ref, gain_ref, wg_ref, wu_ref, wo_ref, fgain_ref, o_ref, n_ref, *, final_norm):
    j = pl.program_id(1)

    @pl.when(j == 0)
    def _init():
        x = x_ref[...]
        n_ref[...] = _rmsnorm(x, gain_ref[...]).astype(BF16)
        o_ref[...] = x

    n = n_ref[...]
    gate = _dot(n, wg_ref[...].astype(BF16))
    up = _dot(n, wu_ref[...].astype(BF16))
    act = (0.5 * _silu(gate) * up).astype(BF16)
    o_ref[...] += _dot(act, wo_ref[...].astype(BF16))

    if final_norm:

        @pl.when(j == pl.num_programs(1) - 1)
        def _finish():
            o_ref[...] = _rmsnorm(o_ref[...], fgain_ref[...])


def _ffn(x, gain, w_in, w_out, final_gain, *, final_norm, tm=1024, tf=256):
    t, d = x.shape
    d_ff = w_out.shape[0]
    nj = d_ff // tf
    assert t % tm == 0 and d_ff % tf == 0 and w_in.shape == (d, 2 * d_ff)
    return pl.pallas_call(
        functools.partial(_ffn_body, final_norm=final_norm),
        grid=(t // tm, nj),
        in_specs=[
            pl.BlockSpec((tm, d), lambda i, j: (i, 0)),
            pl.BlockSpec((1, d), lambda i, j: (0, 0)),
            pl.BlockSpec((d, tf), lambda i, j: (0, j)),
            pl.BlockSpec((d, tf), lambda i, j: (0, j + nj)),
            pl.BlockSpec((tf, d), lambda i, j: (j, 0)),
            pl.BlockSpec((1, d), lambda i, j: (0, 0)),
        ],
        out_specs=pl.BlockSpec((tm, d), lambda i, j: (i, 0)),
        out_shape=jax.ShapeDtypeStruct((t, d), F32),
        scratch_shapes=[pltpu.VMEM((tm, d), BF16)],
        compiler_params=_params(("parallel", "arbitrary")),
        name="ffn_final" if final_norm else "ffn",
    )(x, gain, w_in, w_in, w_out, final_gain)


def _pool_group(x, halo, window, w, scale, t_first):
    tt = x.shape[0]
    ext = jnp.concatenate([halo, x], axis=0)
    row = lax.broadcasted_iota(jnp.int32, ext.shape, 0)
    acc = ext
    span = 1
    while span < window:
        acc = acc + jnp.where(row >= span, pltpu.roll(acc, span, axis=0), 0.0)
        span *= 2
    cnt = jnp.clip(row + (t_first + 1 - POOL_HALO), 1, window).astype(F32)
    pooled = (acc / cnt - ext)[POOL_HALO:, :].astype(BF16)
    return _dot(pooled, w) * scale


def _mixer_body(h_ref, gain_ref, w_ref, wgate_ref, wpool_ref, pscale_ref, wa_ref, ba_ref, gn_ref,
                y_ref,
                win_s, qd_s, ki_s, kt_s, bc_s, v_s, sg_s, st_s, halo_s,
                *, n_cast, nt, d_pool, dk, dv):
    s = pl.program_id(0)
    tt = h_ref.shape[0]
    wc = w_ref.shape[1]
    heads = GLA_HEADS
    group_dim = d_pool // N_POOL_GROUPS
    pool_chunks = d_pool // wc
    q_chunk = pool_chunks
    k_chunk = q_chunk + 1
    v_chunk0 = k_chunk + 1
    v_chunks = heads * dv // wc
    g_chunk0 = v_chunk0 + v_chunks

    @pl.when(s < n_cast)
    def _cast_weights():
        win_s[s] = w_ref[...].astype(BF16)

    @pl.when(s >= n_cast)
    def _compute():
        t = lax.rem(s - n_cast, nt)

        @pl.when(t == 0)
        def _reset():
            st_s[...] = jnp.zeros_like(st_s)
            halo_s[...] = jnp.zeros_like(halo_s)

        n = _rmsnorm(h_ref[...], gain_ref[...]).astype(BF16)

        per_chunk = wc // group_dim
        for c in range(pool_chunks):
            u_c = _dot(n, win_s[c])
            for part in range(per_chunk):
                gi = c * per_chunk + part
                cols = slice(gi * group_dim, (gi + 1) * group_dim)
                x = u_c[:, part * group_dim:(part + 1) * group_dim]
                y = _pool_group(x, halo_s[:, cols], POOL_WINDOWS[gi], wpool_ref[gi].astype(BF16),
                                pscale_ref[:, cols], t * tt)
                halo_s[:, cols] = x[tt - POOL_HALO:, :]
                y_ref[:, cols] = y.astype(y_ref.dtype)

        gate = _dot(n, wgate_ref[...])
        logits = _dot(gate.astype(BF16), wa_ref[...]) + ba_ref[...]
        log_alpha = _log_sigmoid(logits) * (1.0 / GATE_LOGIT_NORMALIZER)
        pos = lax.broadcasted_iota(jnp.int32, log_alpha.shape, 0) & (CHUNK - 1)
        bcum = log_alpha
        span = 1
        while span < CHUNK:
            bcum = bcum + jnp.where(pos >= span, pltpu.roll(bcum, span, axis=0), 0.0)
            span *= 2
        bc_s[...] = bcum
        n_chunks = tt // CHUNK
        b3 = bcum.reshape(n_chunks, CHUNK, heads * dk)
        b_last = jnp.broadcast_to(b3[:, CHUNK - 1:CHUNK, :], b3.shape).reshape(tt, heads * dk)
        q = _dot(n, win_s[q_chunk])
        qd_s[...] = (q * (dk ** -0.5) * jnp.exp(bcum)).astype(BF16)
        k = _dot(n, win_s[k_chunk])
        ki_s[...] = (k * jnp.exp(-bcum)).astype(BF16)
        kt_s[...] = (k * jnp.exp(b_last - bcum)).astype(BF16)
        for c in range(v_chunks):
            cols = slice(c * wc, (c + 1) * wc)
            v_s[:, cols] = _dot(n, win_s[v_chunk0 + c]).astype(BF16)
            sg_s[:, cols] = _silu(_dot(n, win_s[g_chunk0 + c]))

        tril = (lax.broadcasted_iota(jnp.int32, (CHUNK, CHUNK), 0)
                >= lax.broadcasted_iota(jnp.int32, (CHUNK, CHUNK), 1))
        nt_dims = (((1,), (1,)), ((), ()))
        tn_dims = (((0,), (0,)), ((), ()))

        def chunk_step(ci, carry):
            rows = pl.ds(pl.multiple_of(ci * CHUNK, CHUNK), CHUNK)
            decay = jnp.exp(bc_s[pl.ds(ci * CHUNK + (CHUNK - 1), 1), :])
            for hd in range(heads):
                kc = slice(hd * dk, (hd + 1) * dk)
                vc = slice(hd * dv, (hd + 1) * dv)
                qd = qd_s[rows, kc]
                v = v_s[rows, vc]
                scores = lax.dot_general(qd, ki_s[rows, kc], nt_dims, preferred_element_type=F32)
                scores = jnp.where(tril, scores, 0.0).astype(BF16)
                state_t = st_s[hd]
                o = _dot(scores, v)
                o = o + lax.dot_general(qd, state_t.astype(BF16), nt_dims, preferred_element_type=F32)
                kv_t = lax.dot_general(v, kt_s[rows, kc], tn_dims, preferred_element_type=F32)
                st_s[hd] = state_t * decay[:, kc] + kv_t
                o = _rmsnorm(o, gn_ref[...])
                y_ref[rows, d_pool + hd * dv:d_pool + (hd + 1) * dv] = (o * sg_s[rows, vc]).astype(y_ref.dtype)
            return carry

        lax.fori_loop(0, n_chunks, chunk_step, 0, unroll=2)


def _mixer(h, gain, w_in_mix, w_gate, w_pool, pool_scale, w_alpha, b_alpha, gla_norm, *,
           seq, d_pool, dk, dv, tt=512, wc=512):
    t, d = h.shape
    heads = GLA_HEADS
    d_gla = heads * dv
    n_main = d_pool + 2 * heads * dk + 2 * d_gla
    n_cast = n_main // wc
    nt = seq // tt
    n_tiles = t // tt
    group_dim = d_pool // N_POOL_GROUPS
    assert seq % tt == 0 and tt % CHUNK == 0 and n_main % wc == 0 and heads * dk == wc
    assert d_pool % wc == 0 and d_gla % wc == 0 and wc % group_dim == 0
    tile = lambda s: jnp.maximum(s - n_cast, 0)
    const = lambda s: (0, 0)
    return pl.pallas_call(
        functools.partial(_mixer_body, n_cast=n_cast, nt=nt, d_pool=d_pool, dk=dk, dv=dv),
        grid=(n_cast + n_tiles,),
        in_specs=[
            pl.BlockSpec((tt, d), lambda s: (tile(s), 0)),
            pl.BlockSpec((1, d), const),
            pl.BlockSpec((d, wc), lambda s: (0, jnp.minimum(s, n_cast - 1))),
            pl.BlockSpec((d, LANES), const),
            pl.BlockSpec((N_POOL_GROUPS, group_dim, group_dim), lambda s: (0, 0, 0)),
            pl.BlockSpec((1, d_pool), const),
            pl.BlockSpec((LANES, heads * dk), const),
            pl.BlockSpec((1, heads * dk), const),
            pl.BlockSpec((1, dv), const),
        ],
        out_specs=pl.BlockSpec((tt, d_pool + d_gla), lambda s: (tile(s), 0)),
        out_shape=jax.ShapeDtypeStruct((t, d_pool + d_gla), BF16),
        scratch_shapes=[
            pltpu.VMEM((n_cast, d, wc), BF16),
            pltpu.VMEM((tt, heads * dk), BF16),
            pltpu.VMEM((tt, heads * dk), BF16),
            pltpu.VMEM((tt, heads * dk), BF16),
            pltpu.VMEM((tt, heads * dk), F32),
            pltpu.VMEM((tt, d_gla), BF16),
            pltpu.VMEM((tt, d_gla), F32),
            pltpu.VMEM((heads, dv, dk), F32),
            pltpu.VMEM((POOL_HALO, d_pool), F32),
        ],
        compiler_params=_params(("arbitrary",)),
        name="mixer",
    )(h, gain, w_in_mix, w_gate, w_pool, pool_scale, w_alpha, b_alpha, gla_norm)


def _mix_out_body(h_ref, y_ref, w_ref, o_ref):
    o_ref[...] = h_ref[...] + _dot(y_ref[...], w_ref[...])


def _mix_out(h, y, w_out, *, tm=1024, tn=1024):
    t, d = h.shape
    d_mix = y.shape[1]
    assert t % tm == 0 and d % tn == 0 and w_out.shape == (d_mix, d)
    return pl.pallas_call(
        _mix_out_body,
        grid=(t // tm, d // tn),
        in_specs=[
            pl.BlockSpec((tm, tn), lambda i, j: (i, j)),
            pl.BlockSpec((tm, d_mix), lambda i, j: (i, 0)),
            pl.BlockSpec((d_mix, tn), lambda i, j: (0, j)),
        ],
        out_specs=pl.BlockSpec((tm, tn), lambda i, j: (i, j)),
        out_shape=jax.ShapeDtypeStruct((t, d), F32),
        compiler_params=_params(("parallel", "arbitrary")),
        name="mix_out",
    )(h, y, w_out)


def kernel(x, ffn1_norm, ffn1_w_in, ffn1_w_out, mix_norm, w_in_mix, w_pool, pool_scale,
           w_alpha, b_alpha, gla_norm, w_out_mix, ffn2_norm, ffn2_w_in, ffn2_w_out, final_norm):
    b, s, d = x.shape
    depth = ffn1_norm.shape[0]
    d_pool = pool_scale.shape[1]
    dk_total = w_alpha.shape[2]
    dk = dk_total // GLA_HEADS
    dv = gla_norm.shape[1]
    d_gla = GLA_HEADS * dv
    n_main = d_pool + 2 * dk_total + 2 * d_gla
    assert w_in_mix.shape[2] == n_main + GLA_GATE_RANK and s % CHUNK == 0 and depth >= 1

    h = x.reshape(b * s, d)
    final_gain = final_norm.reshape(1, d)
    for l in range(depth):
        last = l == depth - 1
        h = _ffn(h, ffn1_norm[l].reshape(1, d), ffn1_w_in[l], ffn1_w_out[l], final_gain,
                 final_norm=False)

        w_mix = w_in_mix[l]
        w_gate = jnp.pad(w_mix[:, n_main:], ((0, 0), (0, LANES - GLA_GATE_RANK))).astype(BF16)
        w_alpha_pad = jnp.pad(w_alpha[l], ((0, LANES - GLA_GATE_RANK), (0, 0))).astype(BF16)
        y = _mixer(h, mix_norm[l].reshape(1, d), w_mix, w_gate, w_pool[l],
                   pool_scale[l].reshape(1, d_pool), w_alpha_pad, b_alpha[l].reshape(1, dk_total),
                   gla_norm[l].reshape(1, dv), seq=s, d_pool=d_pool, dk=dk, dv=dv)
        h = _mix_out(h, y, w_out_mix[l].astype(BF16))

        h = _ffn(h, ffn2_norm[l].reshape(1, d), ffn2_w_in[l], ffn2_w_out[l], final_gain,
                 final_norm=last)
    return h.reshape(b, s, d)
```

```python
import functools

import jax
import jax.numpy as jnp
from jax import lax
from jax.experimental import pallas as pl
from jax.experimental.pallas import tpu as pltpu

F32 = jnp.float32
BF16 = jnp.bfloat16

EPS = 1e-6
POOL_WINDOWS = (2, 4, 8, 16)
N_POOL_GROUPS = len(POOL_WINDOWS)
POOL_HALO = 16
GLA_HEADS = 4
GLA_GATE_RANK = 16
GATE_LOGIT_NORMALIZER = 16.0
CHUNK = 64

LANES = 128
VMEM_LIMIT_BYTES = 56 * 1024 * 1024


def _rmsnorm(x, gain):
    ms = jnp.mean(x * x, axis=-1, keepdims=True)
    return x * lax.rsqrt(ms + EPS) * gain


def _silu(x):
    return x * (1.0 / (1.0 + jnp.exp(-x)))


def _log_sigmoid(z):
    return jnp.minimum(z, 0.0) - jnp.log1p(jnp.exp(-jnp.abs(z)))


def _dot(a, b):
    return jnp.dot(a, b, preferred_element_type=F32)


def _dot_nt(a, b):
    return lax.dot_general(a, b, (((1,), (1,)), ((), ())), preferred_element_type=F32)


def _dot_tn(a, b):
    return lax.dot_general(a, b, (((0,), (0,)), ((), ())), preferred_element_type=F32)


def _params(semantics):
    return pltpu.CompilerParams(dimension_semantics=semantics, vmem_limit_bytes=VMEM_LIMIT_BYTES)


def _ffn_body(x_ref, gain_ref, wg_ref, wu_ref, wo_ref, fgain_ref, o_ref, n_ref, *, final_norm):
    j = pl.program_id(1)

    @pl.when(j == 0)
    def _init():
        x = x_ref[...]
        n_ref[...] = _rmsnorm(x, gain_ref[...]).astype(BF16)
        o_ref[...] = x

    n = n_ref[...]
    gate = _dot(n, wg_ref[...].astype(BF16))
    up = _dot(n, wu_ref[...].astype(BF16))
    act = (0.5 * _silu(gate) * up).astype(BF16)
    o_ref[...] += _dot(act, wo_ref[...].astype(BF16))

    if final_norm:

        @pl.when(j == pl.num_programs(1) - 1)
        def _finish():
            o_ref[...] = _rmsnorm(o_ref[...], fgain_ref[...])


def _ffn(x, gain, w_in, w_out, final_gain, *, final_norm, tm=1024, tf=256):
    t, d = x.shape
    d_ff = w_out.shape[0]
    nj = d_ff // tf
    assert t % tm == 0 and d_ff % tf == 0 and w_in.shape == (d, 2 * d_ff)
    return pl.pallas_call(
        functools.partial(_ffn_body, final_norm=final_norm),
        grid=(t // tm, nj),
        in_specs=[
            pl.BlockSpec((tm, d), lambda i, j: (i, 0)),
            pl.BlockSpec((1, d), lambda i, j: (0, 0)),
            pl.BlockSpec((d, tf), lambda i, j: (0, j)),
            pl.BlockSpec((d, tf), lambda i, j: (0, j + nj)),
            pl.BlockSpec((tf, d), lambda i, j: (j, 0)),
            pl.BlockSpec((1, d), lambda i, j: (0, 0)),
        ],
        out_specs=pl.BlockSpec((tm, d), lambda i, j: (i, 0)),
        out_shape=jax.ShapeDtypeStruct((t, d), F32),
        scratch_shapes=[pltpu.VMEM((tm, d), BF16)],
        compiler_params=_params(("parallel", "arbitrary")),
        name="ffn_final" if final_norm else "ffn",
    )(x, gain, w_in, w_in, w_out, final_gain)


def _pool_group(x, halo, window, w, scale, t_first):
    tt = x.shape[0]
    ext = jnp.concatenate([halo, x], axis=0)
    row = lax.broadcasted_iota(jnp.int32, ext.shape, 0)
    acc = ext
    span = 1
    while span < window:
        acc = acc + jnp.where(row >= span, pltpu.roll(acc, span, axis=0), 0.0)
        span *= 2
    cnt = jnp.clip(row + (t_first + 1 - POOL_HALO), 1, window).astype(F32)
    pooled = (acc / cnt - ext)[POOL_HALO:, :].astype(BF16)
    return _dot(pooled, w) * scale


def _mixer_body(h_ref, gain_ref, w_ref, wgate_ref, wpool_ref, pscale_ref, wa_ref, ba_ref, gn_ref,
                y_ref,
                win_s, qd_s, ki_s, kt_s, bc_s, v_s, sg_s, st_s, halo_s,
                *, n_cast, nt, d_pool, dk, dv):
    s = pl.program_id(0)
    tt = h_ref.shape[0]
    wc = w_ref.shape[0]
    heads = GLA_HEADS
    group_dim = d_pool // N_POOL_GROUPS
    pool_chunks = d_pool // wc
    q_chunk = pool_chunks
    k_chunk = q_chunk + 1
    v_chunk0 = k_chunk + 1
    v_chunks = heads * dv // wc
    g_chunk0 = v_chunk0 + v_chunks

    @pl.when(s < n_cast)
    def _cast_weights():
        win_s[s] = w_ref[...].astype(BF16)

    @pl.when(s >= n_cast)
    def _compute():
        t = lax.rem(s - n_cast, nt)

        @pl.when(t == 0)
        def _reset():
            st_s[...] = jnp.zeros_like(st_s)
            halo_s[...] = jnp.zeros_like(halo_s)

        n = _rmsnorm(h_ref[...], gain_ref[...]).astype(BF16)

        per_chunk = wc // group_dim
        for c in range(pool_chunks):
            u_c = _dot_nt(n, win_s[c])
            for part in range(per_chunk):
                gi = c * per_chunk + part
                cols = slice(gi * group_dim, (gi + 1) * group_dim)
                x = u_c[:, part * group_dim:(part + 1) * group_dim]
                y = _pool_group(x, halo_s[:, cols], POOL_WINDOWS[gi], wpool_ref[gi].astype(BF16),
                                pscale_ref[:, cols], t * tt)
                halo_s[:, cols] = x[tt - POOL_HALO:, :]
                y_ref[:, cols] = y.astype(y_ref.dtype)

        gate = _dot_nt(n, wgate_ref[...])
        logits = _dot(gate.astype(BF16), wa_ref[...]) + ba_ref[...]
        log_alpha = _log_sigmoid(logits) * (1.0 / GATE_LOGIT_NORMALIZER)
        pos = lax.broadcasted_iota(jnp.int32, log_alpha.shape, 0) & (CHUNK - 1)
        bcum = log_alpha
        span = 1
        while span < CHUNK:
            bcum = bcum + jnp.where(pos >= span, pltpu.roll(bcum, span, axis=0), 0.0)
            span *= 2
        bc_s[...] = bcum
        n_chunks = tt // CHUNK
        b3 = bcum.reshape(n_chunks, CHUNK, heads * dk)
        b_last = jnp.broadcast_to(b3[:, CHUNK - 1:CHUNK, :], b3.shape).reshape(tt, heads * dk)
        q = _dot_nt(n, win_s[q_chunk])
        qd_s[...] = (q * (dk ** -0.5) * jnp.exp(bcum)).astype(BF16)
        k = _dot_nt(n, win_s[k_chunk])
        ki_s[...] = (k * jnp.exp(-bcum)).astype(BF16)
        kt_s[...] = (k * jnp.exp(b_last - bcum)).astype(BF16)
        for c in range(v_chunks):
            cols = slice(c * wc, (c + 1) * wc)
            v_s[:, cols] = _dot_nt(n, win_s[v_chunk0 + c]).astype(BF16)
            sg_s[:, cols] = _silu(_dot_nt(n, win_s[g_chunk0 + c]))

        tril = (lax.broadcasted_iota(jnp.int32, (CHUNK, CHUNK), 0)
                >= lax.broadcasted_iota(jnp.int32, (CHUNK, CHUNK), 1))

        def chunk_step(ci, carry):
            rows = pl.ds(pl.multiple_of(ci * CHUNK, CHUNK), CHUNK)
            decay = jnp.exp(bc_s[pl.ds(ci * CHUNK + (CHUNK - 1), 1), :])
            for hd in range(heads):
                kc = slice(hd * dk, (hd + 1) * dk)
                vc = slice(hd * dv, (hd + 1) * dv)
                qd = qd_s[rows, kc]
                v = v_s[rows, vc]
                scores = _dot_nt(qd, ki_s[rows, kc])
                scores = jnp.where(tril, scores, 0.0).astype(BF16)
                state_t = st_s[hd]
                o = _dot(scores, v)
                o = o + _dot_nt(qd, state_t.astype(BF16))
                kv_t = _dot_tn(v, kt_s[rows, kc])
                st_s[hd] = state_t * decay[:, kc] + kv_t
                o = _rmsnorm(o, gn_ref[...])
                y_ref[rows, d_pool + hd * dv:d_pool + (hd + 1) * dv] = (o * sg_s[rows, vc]).astype(y_ref.dtype)
            return carry

        lax.fori_loop(0, n_chunks, chunk_step, 0, unroll=2)


def _mixer(h, gain, w_in_t, w_gate_t, w_pool, pool_scale, w_alpha, b_alpha, gla_norm, *,
           seq, d_pool, dk, dv, tt=512, wc=512):
    t, d = h.shape
    heads = GLA_HEADS
    d_gla = heads * dv
    n_main = d_pool + 2 * heads * dk + 2 * d_gla
    n_cast = n_main // wc
    nt = seq // tt
    n_tiles = t // tt
    group_dim = d_pool // N_POOL_GROUPS
    assert seq % tt == 0 and tt % CHUNK == 0 and n_main % wc == 0 and heads * dk == wc
    assert d_pool % wc == 0 and d_gla % wc == 0 and wc % group_dim == 0
    tile = lambda s: jnp.maximum(s - n_cast, 0)
    const = lambda s: (0, 0)
    return pl.pallas_call(
        functools.partial(_mixer_body, n_cast=n_cast, nt=nt, d_pool=d_pool, dk=dk, dv=dv),
        grid=(n_cast + n_tiles,),
        in_specs=[
            pl.BlockSpec((tt, d), lambda s: (tile(s), 0)),
            pl.BlockSpec((1, d), const),
            pl.BlockSpec((wc, d), lambda s: (jnp.minimum(s, n_cast - 1), 0)),
            pl.BlockSpec((LANES, d), const),
            pl.BlockSpec((N_POOL_GROUPS, group_dim, group_dim), lambda s: (0, 0, 0)),
            pl.BlockSpec((1, d_pool), const),
            pl.BlockSpec((LANES, heads * dk), const),
            pl.BlockSpec((1, heads * dk), const),
            pl.BlockSpec((1, dv), const),
        ],
        out_specs=pl.BlockSpec((tt, d_pool + d_gla), lambda s: (tile(s), 0)),
        out_shape=jax.ShapeDtypeStruct((t, d_pool + d_gla), BF16),
        scratch_shapes=[
            pltpu.VMEM((n_cast, wc, d), BF16),
            pltpu.VMEM((tt, heads * dk), BF16),
            pltpu.VMEM((tt, heads * dk), BF16),
            pltpu.VMEM((tt, heads * dk), BF16),
            pltpu.VMEM((tt, heads * dk), F32),
            pltpu.VMEM((tt, d_gla), BF16),
            pltpu.VMEM((tt, d_gla), F32),
            pltpu.VMEM((heads, dv, dk), F32),
            pltpu.VMEM((POOL_HALO, d_pool), F32),
        ],
        compiler_params=_params(("arbitrary",)),
        name="mixer",
    )(h, gain, w_in_t, w_gate_t, w_pool, pool_scale, w_alpha, b_alpha, gla_norm)


def _mix_out_body(h_ref, y_ref, w_ref, o_ref):
    o_ref[...] = h_ref[...] + _dot(y_ref[...], w_ref[...])


def _mix_out(h, y, w_out, *, tm=1024, tn=1024):
    t, d = h.shape
    d_mix = y.shape[1]
    assert t % tm == 0 and d % tn == 0 and w_out.shape == (d_mix, d)
    return pl.pallas_call(
        _mix_out_body,
        grid=(t // tm, d // tn),
        in_specs=[
            pl.BlockSpec((tm, tn), lambda i, j: (i, j)),
            pl.BlockSpec((tm, d_mix), lambda i, j: (i, 0)),
            pl.BlockSpec((d_mix, tn), lambda i, j: (0, j)),
        ],
        out_specs=pl.BlockSpec((tm, tn), lambda i, j: (i, j)),
        out_shape=jax.ShapeDtypeStruct((t, d), F32),
        compiler_params=_params(("parallel", "arbitrary")),
        name="mix_out",
    )(h, y, w_out)


def kernel(x, ffn1_norm, ffn1_w_in, ffn1_w_out, mix_norm, w_in_mix, w_pool, pool_scale,
           w_alpha, b_alpha, gla_norm, w_out_mix, ffn2_norm, ffn2_w_in, ffn2_w_out, final_norm):
    b, s, d = x.shape
    depth = ffn1_norm.shape[0]
    d_pool = pool_scale.shape[1]
    dk_total = w_alpha.shape[2]
    dk = dk_total // GLA_HEADS
    dv = gla_norm.shape[1]
    d_gla = GLA_HEADS * dv
    n_main = d_pool + 2 * dk_total + 2 * d_gla
    assert w_in_mix.shape[2] == n_main + GLA_GATE_RANK and s % CHUNK == 0 and depth >= 1

    h = x.reshape(b * s, d)
    final_gain = final_norm.reshape(1, d)
    for l in range(depth):
        last = l == depth - 1
        h = _ffn(h, ffn1_norm[l].reshape(1, d), ffn1_w_in[l], ffn1_w_out[l], final_gain,
                 final_norm=False)

        w_mix_t = jnp.swapaxes(w_in_mix[l], 0, 1)
        w_gate_t = jnp.pad(w_mix_t[n_main:, :], ((0, LANES - GLA_GATE_RANK), (0, 0))).astype(BF16)
        w_alpha_pad = jnp.pad(w_alpha[l], ((0, LANES - GLA_GATE_RANK), (0, 0))).astype(BF16)
        y = _mixer(h, mix_norm[l].reshape(1, d), w_mix_t, w_gate_t, w_pool[l],
                   pool_scale[l].reshape(1, d_pool), w_alpha_pad, b_alpha[l].reshape(1, dk_total),
                   gla_norm[l].reshape(1, dv), seq=s, d_pool=d_pool, dk=dk, dv=dv)
        h = _mix_out(h, y, w_out_mix[l].astype(BF16))

        h = _ffn(h, ffn2_norm[l].reshape(1, d), ffn2_w_in[l], ffn2_w_out[l], final_gain,
                 final_norm=last)
    return h.reshape(b, s, d)
```

```python
import functools

import jax
import jax.numpy as jnp
from jax import lax
from jax.experimental import pallas as pl
from jax.experimental.pallas import tpu as pltpu

F32 = jnp.float32
BF16 = jnp.bfloat16

EPS = 1e-6
POOL_WINDOWS = (2, 4, 8, 16)
N_POOL_GROUPS = len(POOL_WINDOWS)
POOL_HALO = 16
GLA_HEADS = 4
GLA_GATE_RANK = 16
GATE_LOGIT_NORMALIZER = 16.0
CHUNK = 64

LANES = 128
VMEM_LIMIT_BYTES = 58 * 1024 * 1024


def _rmsnorm(x, gain):
    ms = jnp.mean(x * x, axis=-1, keepdims=True)
    return x * lax.rsqrt(ms + EPS) * gain


def _silu(x):
    return x * (1.0 / (1.0 + jnp.exp(-x)))


def _log_sigmoid(z):
    return jnp.minimum(z, 0.0) - jnp.log1p(jnp.exp(-jnp.abs(z)))


def _dot(a, b):
    return jnp.dot(a, b, preferred_element_type=F32)


def _dot_nt(a, b):
    return lax.dot_general(a, b, (((1,), (1,)), ((), ())), preferred_element_type=F32)


def _dot_tn(a, b):
    return lax.dot_general(a, b, (((0,), (0,)), ((), ())), preferred_element_type=F32)


def _params(semantics):
    return pltpu.CompilerParams(dimension_semantics=semantics, vmem_limit_bytes=VMEM_LIMIT_BYTES)


def _ffn_step(j, nj, x_ref, gain_ref, weights, fgain_ref, o_ref, n_ref, final_norm):
    @pl.when(j == 0)
    def _init():
        x = x_ref[...]
        n_ref[...] = _rmsnorm(x, gain_ref[...]).astype(BF16)
        o_ref[...] = x

    wg, wu, wo = weights()
    n = n_ref[...]
    gate = _dot(n, wg)
    up = _dot(n, wu)
    act = (0.5 * _silu(gate) * up).astype(BF16)
    o_ref[...] += _dot(act, wo)

    if final_norm:

        @pl.when(j == nj - 1)
        def _finish():
            o_ref[...] = _rmsnorm(o_ref[...], fgain_ref[...])


def _ffn_head_body(x_ref, gain_ref, wg_ref, wu_ref, wo_ref, fgain_ref,
                   o_ref, wg_bf_ref, wu_bf_ref, wo_bf_ref, n_ref, *, final_norm):
    def weights():
        wg = wg_ref[...].astype(BF16)
        wu = wu_ref[...].astype(BF16)
        wo = wo_ref[...].astype(BF16)
        wg_bf_ref[...] = wg
        wu_bf_ref[...] = wu
        wo_bf_ref[...] = wo
        return wg, wu, wo

    _ffn_step(pl.program_id(0), pl.num_programs(0), x_ref, gain_ref, weights, fgain_ref,
              o_ref, n_ref, final_norm)


def _ffn_tail_body(x_ref, gain_ref, wg_ref, wu_ref, wo_ref, fgain_ref, head_ref,
                   o_ref, n_ref, *, final_norm, n_copy):
    i = pl.program_id(0)
    j = pl.program_id(1)
    rc = head_ref.shape[0]

    @pl.when((i == 0) & (j < n_copy))
    def _copy_head_rows():
        o_ref[pl.ds(pl.multiple_of(j * rc, rc), rc), :] = head_ref[...]

    @pl.when(i > 0)
    def _compute():
        _ffn_step(j, pl.num_programs(1), x_ref, gain_ref,
                  lambda: (wg_ref[...], wu_ref[...], wo_ref[...]), fgain_ref, o_ref, n_ref,
                  final_norm)


def _ffn(x, gain, w_in, w_out, final_gain, *, final_norm, tm=1024, tf_head=256, tf_tail=512,
         copy_rows=128):
    t, d = x.shape
    d_ff = w_out.shape[0]
    assert t % tm == 0 and t // tm >= 2 and w_in.shape == (d, 2 * d_ff)
    assert d_ff % tf_head == 0 and d_ff % tf_tail == 0
    nj = d_ff // tf_head
    vec = pl.BlockSpec((1, d), lambda j: (0, 0))
    h_head, wg_bf, wu_bf, wo_bf = pl.pallas_call(
        functools.partial(_ffn_head_body, final_norm=final_norm),
        grid=(nj,),
        in_specs=[
            pl.BlockSpec((tm, d), lambda j: (0, 0), pipeline_mode=pl.Buffered(1)),
            vec,
            pl.BlockSpec((d, tf_head), lambda j: (0, j)),
            pl.BlockSpec((d, tf_head), lambda j: (0, j + nj)),
            pl.BlockSpec((tf_head, d), lambda j: (j, 0)),
            vec,
        ],
        out_specs=[
            pl.BlockSpec((tm, d), lambda j: (0, 0)),
            pl.BlockSpec((d, tf_head), lambda j: (0, j)),
            pl.BlockSpec((d, tf_head), lambda j: (0, j)),
            pl.BlockSpec((tf_head, d), lambda j: (j, 0)),
        ],
        out_shape=[
            jax.ShapeDtypeStruct((tm, d), F32),
            jax.ShapeDtypeStruct((d, d_ff), BF16),
            jax.ShapeDtypeStruct((d, d_ff), BF16),
            jax.ShapeDtypeStruct((d_ff, d), BF16),
        ],
        scratch_shapes=[pltpu.VMEM((tm, d), BF16)],
        compiler_params=_params(("arbitrary",)),
        name="ffn_head_final" if final_norm else "ffn_head",
    )(x, gain, w_in, w_in, w_out, final_gain)

    nj_tail = d_ff // tf_tail
    n_copy = tm // copy_rows
    assert tm % copy_rows == 0 and n_copy <= nj_tail
    vec2 = pl.BlockSpec((1, d), lambda i, j: (0, 0))
    wcol = lambda i, j: (0, j * jnp.minimum(i, 1))
    wrow = lambda i, j: (j * jnp.minimum(i, 1), 0)
    head_rows = lambda i, j: (jnp.where(i == 0, jnp.minimum(j, n_copy - 1), n_copy - 1), 0)
    return pl.pallas_call(
        functools.partial(_ffn_tail_body, final_norm=final_norm, n_copy=n_copy),
        grid=(t // tm, nj_tail),
        in_specs=[
            pl.BlockSpec((tm, d), lambda i, j: (jnp.maximum(i, 1), 0)),
            vec2,
            pl.BlockSpec((d, tf_tail), wcol),
            pl.BlockSpec((d, tf_tail), wcol),
            pl.BlockSpec((tf_tail, d), wrow),
            vec2,
            pl.BlockSpec((copy_rows, d), head_rows, pipeline_mode=pl.Buffered(1)),
        ],
        out_specs=pl.BlockSpec((tm, d), lambda i, j: (i, 0)),
        out_shape=jax.ShapeDtypeStruct((t, d), F32),
        scratch_shapes=[pltpu.VMEM((tm, d), BF16)],
        compiler_params=_params(("parallel", "arbitrary")),
        name="ffn_tail_final" if final_norm else "ffn_tail",
    )(x, gain, wg_bf, wu_bf, wo_bf, final_gain, h_head)


def _pool_group(x, halo, window, w, scale, t_first):
    tt = x.shape[0]
    ext = jnp.concatenate([halo, x], axis=0)
    row = lax.broadcasted_iota(jnp.int32, ext.shape, 0)
    acc = ext
    span = 1
    while span < window:
        acc = acc + jnp.where(row >= span, pltpu.roll(acc, span, axis=0), 0.0)
        span *= 2
    cnt = jnp.clip(row + (t_first + 1 - POOL_HALO), 1, window).astype(F32)
    pooled = (acc / cnt - ext)[POOL_HALO:, :].astype(BF16)
    return _dot(pooled, w) * scale


def _mixer_body(h_ref, gain_ref, w_ref, wgate_ref, wpool_ref, pscale_ref, wa_ref, ba_ref, gn_ref,
                y_ref,
                win_s, qd_s, ki_s, kt_s, bc_s, v_s, sg_s, st_s, halo_s,
                *, n_cast, nt, d_pool, dk, dv):
    s = pl.program_id(0)
    tt = h_ref.shape[0]
    wc = w_ref.shape[0]
    heads = GLA_HEADS
    group_dim = d_pool // N_POOL_GROUPS
    pool_chunks = d_pool // wc
    q_chunk = pool_chunks
    k_chunk = q_chunk + 1
    v_chunk0 = k_chunk + 1
    v_chunks = heads * dv // wc
    g_chunk0 = v_chunk0 + v_chunks

    @pl.when(s < n_cast)
    def _cast_weights():
        win_s[s] = w_ref[...].astype(BF16)

    @pl.when(s >= n_cast)
    def _compute():
        t = lax.rem(s - n_cast, nt)

        @pl.when(t == 0)
        def _reset():
            st_s[...] = jnp.zeros_like(st_s)
            halo_s[...] = jnp.zeros_like(halo_s)

        n = _rmsnorm(h_ref[...], gain_ref[...]).astype(BF16)

        per_chunk = wc // group_dim
        for c in range(pool_chunks):
            u_c = _dot_nt(n, win_s[c])
            for part in range(per_chunk):
                gi = c * per_chunk + part
                cols = slice(gi * group_dim, (gi + 1) * group_dim)
                x = u_c[:, part * group_dim:(part + 1) * group_dim]
                y = _pool_group(x, halo_s[:, cols], POOL_WINDOWS[gi], wpool_ref[gi].astype(BF16),
                                pscale_ref[:, cols], t * tt)
                halo_s[:, cols] = x[tt - POOL_HALO:, :]
                y_ref[:, cols] = y.astype(y_ref.dtype)

        gate = _dot_nt(n, wgate_ref[...])
        logits = _dot(gate.astype(BF16), wa_ref[...]) + ba_ref[...]
        log_alpha = _log_sigmoid(logits) * (1.0 / GATE_LOGIT_NORMALIZER)
        pos = lax.broadcasted_iota(jnp.int32, log_alpha.shape, 0) & (CHUNK - 1)
        bcum = log_alpha
        span = 1
        while span < CHUNK:
            bcum = bcum + jnp.where(pos >= span, pltpu.roll(bcum, span, axis=0), 0.0)
            span *= 2
        bc_s[...] = bcum
        n_chunks = tt // CHUNK
        b3 = bcum.reshape(n_chunks, CHUNK, heads * dk)
        b_last = jnp.broadcast_to(b3[:, CHUNK - 1:CHUNK, :], b3.shape).reshape(tt, heads * dk)
        q = _dot_nt(n, win_s[q_chunk])
        qd_s[...] = (q * (dk ** -0.5) * jnp.exp(bcum)).astype(BF16)
        k = _dot_nt(n, win_s[k_chunk])
        ki_s[...] = (k * jnp.exp(-bcum)).astype(BF16)
        kt_s[...] = (k * jnp.exp(b_last - bcum)).astype(BF16)
        for c in range(v_chunks):
            cols = slice(c * wc, (c + 1) * wc)
            v_s[:, cols] = _dot_nt(n, win_s[v_chunk0 + c]).astype(BF16)
            sg_s[:, cols] = _silu(_dot_nt(n, win_s[g_chunk0 + c]))

        tril = (lax.broadcasted_iota(jnp.int32, (CHUNK, CHUNK), 0)
                >= lax.broadcasted_iota(jnp.int32, (CHUNK, CHUNK), 1))

        def chunk_step(ci, carry):
            rows = pl.ds(pl.multiple_of(ci * CHUNK, CHUNK), CHUNK)
            decay = jnp.exp(bc_s[pl.ds(ci * CHUNK + (CHUNK - 1), 1), :])
            for hd in range(heads):
                kc = slice(hd * dk, (hd + 1) * dk)
                vc = slice(hd * dv, (hd + 1) * dv)
                qd = qd_s[rows, kc]
                v = v_s[rows, vc]
                scores = _dot_nt(qd, ki_s[rows, kc])
                scores = jnp.where(tril, scores, 0.0).astype(BF16)
                state_t = st_s[hd]
                o = _dot(scores, v)
                o = o + _dot_nt(qd, state_t.astype(BF16))
                kv_t = _dot_tn(v, kt_s[rows, kc])
                st_s[hd] = state_t * decay[:, kc] + kv_t
                o = _rmsnorm(o, gn_ref[...])
                y_ref[rows, d_pool + hd * dv:d_pool + (hd + 1) * dv] = (o * sg_s[rows, vc]).astype(y_ref.dtype)
            return carry

        lax.fori_loop(0, n_chunks, chunk_step, 0, unroll=2)


def _mixer(h, gain, w_in_t, w_gate_t, w_pool, pool_scale, w_alpha, b_alpha, gla_norm, *,
           seq, d_pool, dk, dv, tt=512, wc=512):
    t, d = h.shape
    heads = GLA_HEADS
    d_gla = heads * dv
    n_main = d_pool + 2 * heads * dk + 2 * d_gla
    n_cast = n_main // wc
    nt = seq // tt
    n_tiles = t // tt
    group_dim = d_pool // N_POOL_GROUPS
    assert seq % tt == 0 and tt % CHUNK == 0 and n_main % wc == 0 and heads * dk == wc
    assert d_pool % wc == 0 and d_gla % wc == 0 and wc % group_dim == 0
    tile = lambda s: jnp.maximum(s - n_cast, 0)
    const = lambda s: (0, 0)
    return pl.pallas_call(
        functools.partial(_mixer_body, n_cast=n_cast, nt=nt, d_pool=d_pool, dk=dk, dv=dv),
        grid=(n_cast + n_tiles,),
        in_specs=[
            pl.BlockSpec((tt, d), lambda s: (tile(s), 0)),
            pl.BlockSpec((1, d), const),
            pl.BlockSpec((wc, d), lambda s: (jnp.minimum(s, n_cast - 1), 0)),
            pl.BlockSpec((LANES, d), const),
            pl.BlockSpec((N_POOL_GROUPS, group_dim, group_dim), lambda s: (0, 0, 0)),
            pl.BlockSpec((1, d_pool), const),
            pl.BlockSpec((LANES, heads * dk), const),
            pl.BlockSpec((1, heads * dk), const),
            pl.BlockSpec((1, dv), const),
        ],
        out_specs=pl.BlockSpec((tt, d_pool + d_gla), lambda s: (tile(s), 0)),
        out_shape=jax.ShapeDtypeStruct((t, d_pool + d_gla), BF16),
        scratch_shapes=[
            pltpu.VMEM((n_cast, wc, d), BF16),
            pltpu.VMEM((tt, heads * dk), BF16),
            pltpu.VMEM((tt, heads * dk), BF16),
            pltpu.VMEM((tt, heads * dk), BF16),
            pltpu.VMEM((tt, heads * dk), F32),
            pltpu.VMEM((tt, d_gla), BF16),
            pltpu.VMEM((tt, d_gla), F32),
            pltpu.VMEM((heads, dv, dk), F32),
            pltpu.VMEM((POOL_HALO, d_pool), F32),
        ],
        compiler_params=_params(("arbitrary",)),
        name="mixer",
    )(h, gain, w_in_t, w_gate_t, w_pool, pool_scale, w_alpha, b_alpha, gla_norm)


def _mix_out_body(h_ref, y_ref, w_ref, o_ref):
    o_ref[...] = h_ref[...] + _dot(y_ref[...], w_ref[...])


def _mix_out(h, y, w_out, *, tm=1024, tn=1024):
    t, d = h.shape
    d_mix = y.shape[1]
    assert t % tm == 0 and d % tn == 0 and w_out.shape == (d_mix, d)
    return pl.pallas_call(
        _mix_out_body,
        grid=(t // tm, d // tn),
        in_specs=[
            pl.BlockSpec((tm, tn), lambda i, j: (i, j)),
            pl.BlockSpec((tm, d_mix), lambda i, j: (i, 0)),
            pl.BlockSpec((d_mix, tn), lambda i, j: (0, j)),
        ],
        out_specs=pl.BlockSpec((tm, tn), lambda i, j: (i, j)),
        out_shape=jax.ShapeDtypeStruct((t, d), F32),
        compiler_params=_params(("parallel", "arbitrary")),
        name="mix_out",
    )(h, y, w_out)


def kernel(x, ffn1_norm, ffn1_w_in, ffn1_w_out, mix_norm, w_in_mix, w_pool, pool_scale,
           w_alpha, b_alpha, gla_norm, w_out_mix, ffn2_norm, ffn2_w_in, ffn2_w_out, final_norm):
    b, s, d = x.shape
    depth = ffn1_norm.shape[0]
    d_pool = pool_scale.shape[1]
    dk_total = w_alpha.shape[2]
    dk = dk_total // GLA_HEADS
    dv = gla_norm.shape[1]
    d_gla = GLA_HEADS * dv
    n_main = d_pool + 2 * dk_total + 2 * d_gla
    assert w_in_mix.shape[2] == n_main + GLA_GATE_RANK and s % CHUNK == 0 and depth >= 1

    h = x.reshape(b * s, d)
    final_gain = final_norm.reshape(1, d)
    for l in range(depth):
        last = l == depth - 1
        h = _ffn(h, ffn1_norm[l].reshape(1, d), ffn1_w_in[l], ffn1_w_out[l], final_gain,
                 final_norm=False)

        w_mix_t = jnp.swapaxes(w_in_mix[l], 0, 1)
        w_gate_t = jnp.pad(w_mix_t[n_main:, :], ((0, LANES - GLA_GATE_RANK), (0, 0))).astype(BF16)
        w_alpha_pad = jnp.pad(w_alpha[l], ((0, LANES - GLA_GATE_RANK), (0, 0))).astype(BF16)
        y = _mixer(h, mix_norm[l].reshape(1, d), w_mix_t, w_gate_t, w_pool[l],
                   pool_scale[l].reshape(1, d_pool), w_alpha_pad, b_alpha[l].reshape(1, dk_total),
                   gla_norm[l].reshape(1, dv), seq=s, d_pool=d_pool, dk=dk, dv=dv)
        h = _mix_out(h, y, w_out_mix[l].astype(BF16))

        h = _ffn(h, ffn2_norm[l].reshape(1, d), ffn2_w_in[l], ffn2_w_out[l], final_gain,
                 final_norm=last)
    return h.reshape(b, s, d)
```

```python
import functools

import jax
import jax.numpy as jnp
from jax import lax
from jax.experimental import pallas as pl
from jax.experimental.pallas import tpu as pltpu

F32 = jnp.float32
BF16 = jnp.bfloat16

EPS = 1e-6
POOL_WINDOWS = (2, 4, 8, 16)
N_POOL_GROUPS = len(POOL_WINDOWS)
POOL_HALO = 16
GLA_HEADS = 4
GLA_GATE_RANK = 16
GATE_LOGIT_NORMALIZER = 16.0
CHUNK = 64

LANES = 128
VMEM_LIMIT_BYTES = 58 * 1024 * 1024


def _rmsnorm(x, gain):
    ms = jnp.mean(x * x, axis=-1, keepdims=True)
    return x * lax.rsqrt(ms + EPS) * gain


def _silu(x):
    return x * (1.0 / (1.0 + jnp.exp(-x)))


def _log_sigmoid(z):
    return jnp.minimum(z, 0.0) - jnp.log1p(jnp.exp(-jnp.abs(z)))


def _dot(a, b):
    return jnp.dot(a, b, preferred_element_type=F32)


def _dot_nt(a, b):
    return lax.dot_general(a, b, (((1,), (1,)), ((), ())), preferred_element_type=F32)


def _dot_tn(a, b):
    return lax.dot_general(a, b, (((0,), (0,)), ((), ())), preferred_element_type=F32)


def _params(semantics):
    return pltpu.CompilerParams(dimension_semantics=semantics, vmem_limit_bytes=VMEM_LIMIT_BYTES)


def _ffn_step(j, nj, x_ref, gain_ref, weights, fgain_ref, o_ref, n_ref, final_norm):
    @pl.when(j == 0)
    def _init():
        x = x_ref[...]
        n_ref[...] = _rmsnorm(x, gain_ref[...]).astype(BF16)
        o_ref[...] = x

    wg, wu, wo = weights()
    n = n_ref[...]
    gate = _dot(n, wg)
    up = _dot(n, wu)
    act = (0.5 * _silu(gate) * up).astype(BF16)
    o_ref[...] += _dot(act, wo)

    if final_norm:

        @pl.when(j == nj - 1)
        def _finish():
            o_ref[...] = _rmsnorm(o_ref[...], fgain_ref[...])


def _ffn_head_body(x_ref, gain_ref, wg_ref, wu_ref, wo_ref, fgain_ref,
                   o_ref, wg_bf_ref, wu_bf_ref, wo_bf_ref, n_ref, *, final_norm):
    def weights():
        wg = wg_ref[...].astype(BF16)
        wu = wu_ref[...].astype(BF16)
        wo = wo_ref[...].astype(BF16)
        wg_bf_ref[...] = wg
        wu_bf_ref[...] = wu
        wo_bf_ref[...] = wo
        return wg, wu, wo

    _ffn_step(pl.program_id(0), pl.num_programs(0), x_ref, gain_ref, weights, fgain_ref,
              o_ref, n_ref, final_norm)


def _ffn_tail_body(x_ref, gain_ref, wg_ref, wu_ref, wo_ref, fgain_ref, head_ref,
                   o_ref, n_ref, *, final_norm, n_copy):
    i = pl.program_id(0)
    j = pl.program_id(1)
    rc = head_ref.shape[0]

    @pl.when((i == 0) & (j < n_copy))
    def _copy_head_rows():
        o_ref[pl.ds(pl.multiple_of(j * rc, rc), rc), :] = head_ref[...]

    @pl.when(i > 0)
    def _compute():
        _ffn_step(j, pl.num_programs(1), x_ref, gain_ref,
                  lambda: (wg_ref[...], wu_ref[...], wo_ref[...]), fgain_ref, o_ref, n_ref,
                  final_norm)


def _ffn(x, gain, w_in, w_out, final_gain, *, final_norm, tm=1024, tf_head=256, tf_tail=512,
         copy_rows=128):
    t, d = x.shape
    d_ff = w_out.shape[0]
    assert t % tm == 0 and t // tm >= 2 and w_in.shape == (d, 2 * d_ff)
    assert d_ff % tf_head == 0 and d_ff % tf_tail == 0
    nj = d_ff // tf_head
    vec = pl.BlockSpec((1, d), lambda j: (0, 0))
    h_head, wg_bf, wu_bf, wo_bf = pl.pallas_call(
        functools.partial(_ffn_head_body, final_norm=final_norm),
        grid=(nj,),
        in_specs=[
            pl.BlockSpec((tm, d), lambda j: (0, 0), pipeline_mode=pl.Buffered(1)),
            vec,
            pl.BlockSpec((d, tf_head), lambda j: (0, j)),
            pl.BlockSpec((d, tf_head), lambda j: (0, j + nj)),
            pl.BlockSpec((tf_head, d), lambda j: (j, 0)),
            vec,
        ],
        out_specs=[
            pl.BlockSpec((tm, d), lambda j: (0, 0)),
            pl.BlockSpec((d, tf_head), lambda j: (0, j)),
            pl.BlockSpec((d, tf_head), lambda j: (0, j)),
            pl.BlockSpec((tf_head, d), lambda j: (j, 0)),
        ],
        out_shape=[
            jax.ShapeDtypeStruct((tm, d), F32),
            jax.ShapeDtypeStruct((d, d_ff), BF16),
            jax.ShapeDtypeStruct((d, d_ff), BF16),
            jax.ShapeDtypeStruct((d_ff, d), BF16),
        ],
        scratch_shapes=[pltpu.VMEM((tm, d), BF16)],
        compiler_params=_params(("arbitrary",)),
        name="ffn_head_final" if final_norm else "ffn_head",
    )(x, gain, w_in, w_in, w_out, final_gain)

    nj_tail = d_ff // tf_tail
    n_copy = tm // copy_rows
    assert tm % copy_rows == 0 and n_copy <= nj_tail
    vec2 = pl.BlockSpec((1, d), lambda i, j: (0, 0))
    wcol = lambda i, j: (0, j * jnp.minimum(i, 1))
    wrow = lambda i, j: (j * jnp.minimum(i, 1), 0)
    head_rows = lambda i, j: (jnp.where(i == 0, jnp.minimum(j, n_copy - 1), n_copy - 1), 0)
    return pl.pallas_call(
        functools.partial(_ffn_tail_body, final_norm=final_norm, n_copy=n_copy),
        grid=(t // tm, nj_tail),
        in_specs=[
            pl.BlockSpec((tm, d), lambda i, j: (jnp.maximum(i, 1), 0)),
            vec2,
            pl.BlockSpec((d, tf_tail), wcol),
            pl.BlockSpec((d, tf_tail), wcol),
            pl.BlockSpec((tf_tail, d), wrow),
            vec2,
            pl.BlockSpec((copy_rows, d), head_rows, pipeline_mode=pl.Buffered(1)),
        ],
        out_specs=pl.BlockSpec((tm, d), lambda i, j: (i, 0)),
        out_shape=jax.ShapeDtypeStruct((t, d), F32),
        scratch_shapes=[pltpu.VMEM((tm, d), BF16)],
        compiler_params=_params(("parallel", "arbitrary")),
        name="ffn_tail_final" if final_norm else "ffn_tail",
    )(x, gain, wg_bf, wu_bf, wo_bf, final_gain, h_head)


def _pool_group(x, halo, window, w, scale, t_first):
    tt = x.shape[0]
    ext = jnp.concatenate([halo, x], axis=0)
    row = lax.broadcasted_iota(jnp.int32, ext.shape, 0)
    acc = ext
    span = 1
    while span < window:
        acc = acc + jnp.where(row >= span, pltpu.roll(acc, span, axis=0), 0.0)
        span *= 2
    cnt = jnp.clip(row + (t_first + 1 - POOL_HALO), 1, window).astype(F32)
    pooled = (acc / cnt - ext)[POOL_HALO:, :].astype(BF16)
    return _dot(pooled, w) * scale


def _mixer_body(h_ref, gain_ref, w_ref, wgate_ref, wpool_ref, pscale_ref, wa_ref, ba_ref, gn_ref,
                y_ref,
                win_s, qd_s, ki_s, kt_s, bc_s, v_s, sg_s, st_s, halo_s,
                *, n_cast, nt, d_pool, dk, dv):
    s = pl.program_id(0)
    tt = h_ref.shape[0]
    wc = w_ref.shape[0]
    heads = GLA_HEADS
    group_dim = d_pool // N_POOL_GROUPS
    pool_chunks = d_pool // wc
    q_chunk = pool_chunks
    k_chunk = q_chunk + 1
    v_chunk0 = k_chunk + 1
    v_chunks = heads * dv // wc
    g_chunk0 = v_chunk0 + v_chunks

    @pl.when(s < n_cast)
    def _cast_weights():
        win_s[s] = w_ref[...].astype(BF16)

    @pl.when(s >= n_cast)
    def _compute():
        t = lax.rem(s - n_cast, nt)

        @pl.when(t == 0)
        def _reset():
            st_s[...] = jnp.zeros_like(st_s)
            halo_s[...] = jnp.zeros_like(halo_s)

        n = _rmsnorm(h_ref[...], gain_ref[...]).astype(BF16)

        per_chunk = wc // group_dim
        for c in range(pool_chunks):
            u_c = _dot_nt(n, win_s[c])
            for part in range(per_chunk):
                gi = c * per_chunk + part
                cols = slice(gi * group_dim, (gi + 1) * group_dim)
                x = u_c[:, part * group_dim:(part + 1) * group_dim]
                y = _pool_group(x, halo_s[:, cols], POOL_WINDOWS[gi], wpool_ref[gi].astype(BF16),
                                pscale_ref[:, cols], t * tt)
                halo_s[:, cols] = x[tt - POOL_HALO:, :]
                y_ref[:, cols] = y.astype(y_ref.dtype)

        gate = _dot_nt(n, wgate_ref[...])
        logits = _dot(gate.astype(BF16), wa_ref[...]) + ba_ref[...]
        log_alpha = _log_sigmoid(logits) * (1.0 / GATE_LOGIT_NORMALIZER)
        pos = lax.broadcasted_iota(jnp.int32, log_alpha.shape, 0) & (CHUNK - 1)
        bcum = log_alpha
        span = 1
        while span < CHUNK:
            bcum = bcum + jnp.where(pos >= span, pltpu.roll(bcum, span, axis=0), 0.0)
            span *= 2
        bc_s[...] = bcum
        n_chunks = tt // CHUNK
        b3 = bcum.reshape(n_chunks, CHUNK, heads * dk)
        b_last = jnp.broadcast_to(b3[:, CHUNK - 1:CHUNK, :], b3.shape).reshape(tt, heads * dk)
        q = _dot_nt(n, win_s[q_chunk])
        qd_s[...] = (q * (dk ** -0.5) * jnp.exp(bcum)).astype(BF16)
        k = _dot_nt(n, win_s[k_chunk])
        ki_s[...] = (k * jnp.exp(-bcum)).astype(BF16)
        kt_s[...] = (k * jnp.exp(b_last - bcum)).astype(BF16)
        for c in range(v_chunks):
            cols = slice(c * wc, (c + 1) * wc)
            v_s[:, cols] = _dot_nt(n, win_s[v_chunk0 + c]).astype(BF16)
            sg_s[:, cols] = _silu(_dot_nt(n, win_s[g_chunk0 + c]))

        tril = (lax.broadcasted_iota(jnp.int32, (CHUNK, CHUNK), 0)
                >= lax.broadcasted_iota(jnp.int32, (CHUNK, CHUNK), 1))

        def chunk_step(ci, carry):
            rows = pl.ds(pl.multiple_of(ci * CHUNK, CHUNK), CHUNK)
            decay = jnp.exp(bc_s[pl.ds(ci * CHUNK + (CHUNK - 1), 1), :])
            for hd in range(heads):
                kc = slice(hd * dk, (hd + 1) * dk)
                vc = slice(hd * dv, (hd + 1) * dv)
                qd = qd_s[rows, kc]
                v = v_s[rows, vc]
                scores = _dot_nt(qd, ki_s[rows, kc])
                scores = jnp.where(tril, scores, 0.0).astype(BF16)
                state_t = st_s[hd]
                o = _dot(scores, v)
                o = o + _dot_nt(qd, state_t.astype(BF16))
                kv_t = _dot_tn(v, kt_s[rows, kc])
                st_s[hd] = state_t * decay[:, kc] + kv_t
                o = _rmsnorm(o, gn_ref[...])
                y_ref[rows, d_pool + hd * dv:d_pool + (hd + 1) * dv] = (o * sg_s[rows, vc]).astype(y_ref.dtype)
            return carry

        lax.fori_loop(0, n_chunks, chunk_step, 0, unroll=4)


def _mixer(h, gain, w_in_t, w_gate_t, w_pool, pool_scale, w_alpha, b_alpha, gla_norm, *,
           seq, d_pool, dk, dv, tt=512, wc=512):
    t, d = h.shape
    heads = GLA_HEADS
    d_gla = heads * dv
    n_main = d_pool + 2 * heads * dk + 2 * d_gla
    n_cast = n_main // wc
    nt = seq // tt
    n_tiles = t // tt
    group_dim = d_pool // N_POOL_GROUPS
    assert seq % tt == 0 and tt % CHUNK == 0 and n_main % wc == 0 and heads * dk == wc
    assert d_pool % wc == 0 and d_gla % wc == 0 and wc % group_dim == 0
    tile = lambda s: jnp.maximum(s - n_cast, 0)
    const = lambda s: (0, 0)
    return pl.pallas_call(
        functools.partial(_mixer_body, n_cast=n_cast, nt=nt, d_pool=d_pool, dk=dk, dv=dv),
        grid=(n_cast + n_tiles,),
        in_specs=[
            pl.BlockSpec((tt, d), lambda s: (tile(s), 0)),
            pl.BlockSpec((1, d), const),
            pl.BlockSpec((wc, d), lambda s: (jnp.minimum(s, n_cast - 1), 0)),
            pl.BlockSpec((LANES, d), const),
            pl.BlockSpec((N_POOL_GROUPS, group_dim, group_dim), lambda s: (0, 0, 0)),
            pl.BlockSpec((1, d_pool), const),
            pl.BlockSpec((LANES, heads * dk), const),
            pl.BlockSpec((1, heads * dk), const),
            pl.BlockSpec((1, dv), const),
        ],
        out_specs=pl.BlockSpec((tt, d_pool + d_gla), lambda s: (tile(s), 0)),
        out_shape=jax.ShapeDtypeStruct((t, d_pool + d_gla), BF16),
        scratch_shapes=[
            pltpu.VMEM((n_cast, wc, d), BF16),
            pltpu.VMEM((tt, heads * dk), BF16),
            pltpu.VMEM((tt, heads * dk), BF16),
            pltpu.VMEM((tt, heads * dk), BF16),
            pltpu.VMEM((tt, heads * dk), F32),
            pltpu.VMEM((tt, d_gla), BF16),
            pltpu.VMEM((tt, d_gla), F32),
            pltpu.VMEM((heads, dv, dk), F32),
            pltpu.VMEM((POOL_HALO, d_pool), F32),
        ],
        compiler_params=_params(("arbitrary",)),
        name="mixer",
    )(h, gain, w_in_t, w_gate_t, w_pool, pool_scale, w_alpha, b_alpha, gla_norm)


def _mix_out_body(h_ref, y_ref, w_ref, o_ref, w_bf_s):
    @pl.when(pl.program_id(0) == 0)
    def _cast_weights():
        w_bf_s[...] = w_ref[...].astype(BF16)

    o_ref[...] = h_ref[...] + _dot(y_ref[...], w_bf_s[...])


def _mix_out(h, y, w_out, *, tm=512):
    t, d = h.shape
    d_mix = y.shape[1]
    assert t % tm == 0 and w_out.shape == (d_mix, d)
    return pl.pallas_call(
        _mix_out_body,
        grid=(t // tm,),
        in_specs=[
            pl.BlockSpec((tm, d), lambda i: (i, 0)),
            pl.BlockSpec((tm, d_mix), lambda i: (i, 0)),
            pl.BlockSpec((d_mix, d), lambda i: (0, 0), pipeline_mode=pl.Buffered(1)),
        ],
        out_specs=pl.BlockSpec((tm, d), lambda i: (i, 0)),
        out_shape=jax.ShapeDtypeStruct((t, d), F32),
        scratch_shapes=[pltpu.VMEM((d_mix, d), BF16)],
        compiler_params=_params(("arbitrary",)),
        name="mix_out",
    )(h, y, w_out)


def kernel(x, ffn1_norm, ffn1_w_in, ffn1_w_out, mix_norm, w_in_mix, w_pool, pool_scale,
           w_alpha, b_alpha, gla_norm, w_out_mix, ffn2_norm, ffn2_w_in, ffn2_w_out, final_norm):
    b, s, d = x.shape
    depth = ffn1_norm.shape[0]
    d_pool = pool_scale.shape[1]
    dk_total = w_alpha.shape[2]
    dk = dk_total // GLA_HEADS
    dv = gla_norm.shape[1]
    d_gla = GLA_HEADS * dv
    n_main = d_pool + 2 * dk_total + 2 * d_gla
    assert w_in_mix.shape[2] == n_main + GLA_GATE_RANK and s % CHUNK == 0 and depth >= 1

    h = x.reshape(b * s, d)
    final_gain = final_norm.reshape(1, d)
    for l in range(depth):
        last = l == depth - 1
        h = _ffn(h, ffn1_norm[l].reshape(1, d), ffn1_w_in[l], ffn1_w_out[l], final_gain,
                 final_norm=False)

        w_mix_t = jnp.swapaxes(w_in_mix[l], 0, 1)
        w_gate_t = jnp.pad(w_mix_t[n_main:, :], ((0, LANES - GLA_GATE_RANK), (0, 0))).astype(BF16)
        w_alpha_pad = jnp.pad(w_alpha[l], ((0, LANES - GLA_GATE_RANK), (0, 0))).astype(BF16)
        y = _mixer(h, mix_norm[l].reshape(1, d), w_mix_t, w_gate_t, w_pool[l],
                   pool_scale[l].reshape(1, d_pool), w_alpha_pad, b_alpha[l].reshape(1, dk_total),
                   gla_norm[l].reshape(1, dv), seq=s, d_pool=d_pool, dk=dk, dv=dv)
        h = _mix_out(h, y, w_out_mix[l])

        h = _ffn(h, ffn2_norm[l].reshape(1, d), ffn2_w_in[l], ffn2_w_out[l], final_gain,
                 final_norm=last)
    return h.reshape(b, s, d)
```

```python
import functools

import jax
import jax.numpy as jnp
from jax import lax
from jax.experimental import pallas as pl
from jax.experimental.pallas import tpu as pltpu

F32 = jnp.float32
BF16 = jnp.bfloat16

EPS = 1e-6
POOL_WINDOWS = (2, 4, 8, 16)
N_POOL_GROUPS = len(POOL_WINDOWS)
POOL_HALO = 16
GLA_HEADS = 4
GLA_GATE_RANK = 16
GATE_LOGIT_NORMALIZER = 16.0
CHUNK = 64
SB_CHUNKS = 4

LANES = 128
VMEM_LIMIT_BYTES = 58 * 1024 * 1024


def _rmsnorm(x, gain):
    ms = jnp.mean(x * x, axis=-1, keepdims=True)
    return x * lax.rsqrt(ms + EPS) * gain


def _silu(x):
    return x * (1.0 / (1.0 + jnp.exp(-x)))


def _log_sigmoid(z):
    return jnp.minimum(z, 0.0) - jnp.log1p(jnp.exp(-jnp.abs(z)))


def _dot(a, b):
    return jnp.dot(a, b, preferred_element_type=F32)


def _dot_nt(a, b):
    return lax.dot_general(a, b, (((1,), (1,)), ((), ())), preferred_element_type=F32)


def _dot_tn(a, b):
    return lax.dot_general(a, b, (((0,), (0,)), ((), ())), preferred_element_type=F32)


def _params(semantics):
    return pltpu.CompilerParams(dimension_semantics=semantics, vmem_limit_bytes=VMEM_LIMIT_BYTES)


def _ffn_step(j, nj, x_ref, gain_ref, weights, fgain_ref, o_ref, n_ref, final_norm):
    @pl.when(j == 0)
    def _init():
        x = x_ref[...]
        n_ref[...] = _rmsnorm(x, gain_ref[...]).astype(BF16)
        o_ref[...] = x

    wg, wu, wo = weights()
    n = n_ref[...]
    gate = _dot(n, wg)
    up = _dot(n, wu)
    act = (0.5 * _silu(gate) * up).astype(BF16)
    o_ref[...] += _dot(act, wo)

    if final_norm:

        @pl.when(j == nj - 1)
        def _finish():
            o_ref[...] = _rmsnorm(o_ref[...], fgain_ref[...])


def _ffn_head_body(x_ref, gain_ref, wg_ref, wu_ref, wo_ref, fgain_ref,
                   o_ref, wg_bf_ref, wu_bf_ref, wo_bf_ref, n_ref, *, final_norm):
    def weights():
        wg = wg_ref[...].astype(BF16)
        wu = wu_ref[...].astype(BF16)
        wo = wo_ref[...].astype(BF16)
        wg_bf_ref[...] = wg
        wu_bf_ref[...] = wu
        wo_bf_ref[...] = wo
        return wg, wu, wo

    _ffn_step(pl.program_id(0), pl.num_programs(0), x_ref, gain_ref, weights, fgain_ref,
              o_ref, n_ref, final_norm)


def _ffn_tail_body(x_ref, gain_ref, wg_ref, wu_ref, wo_ref, fgain_ref, head_ref,
                   o_ref, n_ref, *, final_norm, n_copy):
    i = pl.program_id(0)
    j = pl.program_id(1)
    rc = head_ref.shape[0]

    @pl.when((i == 0) & (j < n_copy))
    def _copy_head_rows():
        o_ref[pl.ds(pl.multiple_of(j * rc, rc), rc), :] = head_ref[...]

    @pl.when(i > 0)
    def _compute():
        _ffn_step(j, pl.num_programs(1), x_ref, gain_ref,
                  lambda: (wg_ref[...], wu_ref[...], wo_ref[...]), fgain_ref, o_ref, n_ref,
                  final_norm)


def _ffn(x, gain, w_in, w_out, final_gain, *, final_norm, tm=1024, tf_head=256, tf_tail=512,
         copy_rows=128):
    t, d = x.shape
    d_ff = w_out.shape[0]
    assert t % tm == 0 and t // tm >= 2 and w_in.shape == (d, 2 * d_ff)
    assert d_ff % tf_head == 0 and d_ff % tf_tail == 0
    nj = d_ff // tf_head
    vec = pl.BlockSpec((1, d), lambda j: (0, 0))
    h_head, wg_bf, wu_bf, wo_bf = pl.pallas_call(
        functools.partial(_ffn_head_body, final_norm=final_norm),
        grid=(nj,),
        in_specs=[
            pl.BlockSpec((tm, d), lambda j: (0, 0), pipeline_mode=pl.Buffered(1)),
            vec,
            pl.BlockSpec((d, tf_head), lambda j: (0, j)),
            pl.BlockSpec((d, tf_head), lambda j: (0, j + nj)),
            pl.BlockSpec((tf_head, d), lambda j: (j, 0)),
            vec,
        ],
        out_specs=[
            pl.BlockSpec((tm, d), lambda j: (0, 0)),
            pl.BlockSpec((d, tf_head), lambda j: (0, j)),
            pl.BlockSpec((d, tf_head), lambda j: (0, j)),
            pl.BlockSpec((tf_head, d), lambda j: (j, 0)),
        ],
        out_shape=[
            jax.ShapeDtypeStruct((tm, d), F32),
            jax.ShapeDtypeStruct((d, d_ff), BF16),
            jax.ShapeDtypeStruct((d, d_ff), BF16),
            jax.ShapeDtypeStruct((d_ff, d), BF16),
        ],
        scratch_shapes=[pltpu.VMEM((tm, d), BF16)],
        compiler_params=_params(("arbitrary",)),
        name="ffn_head_final" if final_norm else "ffn_head",
    )(x, gain, w_in, w_in, w_out, final_gain)

    nj_tail = d_ff // tf_tail
    n_copy = tm // copy_rows
    assert tm % copy_rows == 0 and n_copy <= nj_tail
    vec2 = pl.BlockSpec((1, d), lambda i, j: (0, 0))
    wcol = lambda i, j: (0, j * jnp.minimum(i, 1))
    wrow = lambda i, j: (j * jnp.minimum(i, 1), 0)
    head_rows = lambda i, j: (jnp.where(i == 0, jnp.minimum(j, n_copy - 1), n_copy - 1), 0)
    return pl.pallas_call(
        functools.partial(_ffn_tail_body, final_norm=final_norm, n_copy=n_copy),
        grid=(t // tm, nj_tail),
        in_specs=[
            pl.BlockSpec((tm, d), lambda i, j: (jnp.maximum(i, 1), 0)),
            vec2,
            pl.BlockSpec((d, tf_tail), wcol),
            pl.BlockSpec((d, tf_tail), wcol),
            pl.BlockSpec((tf_tail, d), wrow),
            vec2,
            pl.BlockSpec((copy_rows, d), head_rows, pipeline_mode=pl.Buffered(1)),
        ],
        out_specs=pl.BlockSpec((tm, d), lambda i, j: (i, 0)),
        out_shape=jax.ShapeDtypeStruct((t, d), F32),
        scratch_shapes=[pltpu.VMEM((tm, d), BF16)],
        compiler_params=_params(("parallel", "arbitrary")),
        name="ffn_tail_final" if final_norm else "ffn_tail",
    )(x, gain, wg_bf, wu_bf, wo_bf, final_gain, h_head)


def _pool_group(x, halo, window, w, scale, t_first):
    tt = x.shape[0]
    ext = jnp.concatenate([halo, x], axis=0)
    row = lax.broadcasted_iota(jnp.int32, ext.shape, 0)
    acc = ext
    span = 1
    while span < window:
        acc = acc + jnp.where(row >= span, pltpu.roll(acc, span, axis=0), 0.0)
        span *= 2
    cnt = jnp.clip(row + (t_first + 1 - POOL_HALO), 1, window).astype(F32)
    pooled = (acc / cnt - ext)[POOL_HALO:, :].astype(BF16)
    return _dot(pooled, w) * scale


def _mixer_body(h_ref, gain_ref, w_ref, wgate_ref, wpool_ref, pscale_ref, wa_ref, ba_ref, gn_ref,
                y_ref,
                win_s, qd_s, ki_s, kt_s, bc_s, v_s, sg_s, st_s, halo_s,
                *, n_cast, nt, d_pool, dk, dv):
    s = pl.program_id(0)
    tt = h_ref.shape[0]
    wc = w_ref.shape[0]
    heads = GLA_HEADS
    group_dim = d_pool // N_POOL_GROUPS
    pool_chunks = d_pool // wc
    q_chunk = pool_chunks
    k_chunk = q_chunk + 1
    v_chunk0 = k_chunk + 1
    v_chunks = heads * dv // wc
    g_chunk0 = v_chunk0 + v_chunks

    @pl.when(s < n_cast)
    def _cast_weights():
        win_s[s] = w_ref[...].astype(BF16)

    @pl.when(s >= n_cast)
    def _compute():
        t = lax.rem(s - n_cast, nt)

        @pl.when(t == 0)
        def _reset():
            st_s[...] = jnp.zeros_like(st_s)
            halo_s[...] = jnp.zeros_like(halo_s)

        n = _rmsnorm(h_ref[...], gain_ref[...]).astype(BF16)

        per_chunk = wc // group_dim
        for c in range(pool_chunks):
            u_c = _dot_nt(n, win_s[c])
            for part in range(per_chunk):
                gi = c * per_chunk + part
                cols = slice(gi * group_dim, (gi + 1) * group_dim)
                x = u_c[:, part * group_dim:(part + 1) * group_dim]
                y = _pool_group(x, halo_s[:, cols], POOL_WINDOWS[gi], wpool_ref[gi].astype(BF16),
                                pscale_ref[:, cols], t * tt)
                halo_s[:, cols] = x[tt - POOL_HALO:, :]
                y_ref[:, cols] = y.astype(y_ref.dtype)

        gate = _dot_nt(n, wgate_ref[...])
        logits = _dot(gate.astype(BF16), wa_ref[...]) + ba_ref[...]
        log_alpha = _log_sigmoid(logits) * (1.0 / GATE_LOGIT_NORMALIZER)
        pos = lax.broadcasted_iota(jnp.int32, log_alpha.shape, 0) & (CHUNK - 1)
        bcum = log_alpha
        span = 1
        while span < CHUNK:
            bcum = bcum + jnp.where(pos >= span, pltpu.roll(bcum, span, axis=0), 0.0)
            span *= 2
        bc_s[...] = bcum
        n_chunks = tt // CHUNK
        b3 = bcum.reshape(n_chunks, CHUNK, heads * dk)
        b_last = jnp.broadcast_to(b3[:, CHUNK - 1:CHUNK, :], b3.shape).reshape(tt, heads * dk)
        q = _dot_nt(n, win_s[q_chunk])
        qd_s[...] = (q * (dk ** -0.5) * jnp.exp(bcum)).astype(BF16)
        k = _dot_nt(n, win_s[k_chunk])
        ki_s[...] = (k * jnp.exp(-bcum)).astype(BF16)
        kt_s[...] = (k * jnp.exp(b_last - bcum)).astype(BF16)
        for c in range(v_chunks):
            cols = slice(c * wc, (c + 1) * wc)
            v_s[:, cols] = _dot_nt(n, win_s[v_chunk0 + c]).astype(BF16)
            sg_s[:, cols] = _silu(_dot_nt(n, win_s[g_chunk0 + c]))

        sb_rows = SB_CHUNKS * CHUNK
        r_idx = lax.broadcasted_iota(jnp.int32, (sb_rows, sb_rows), 0)
        c_idx = lax.broadcasted_iota(jnp.int32, (sb_rows, sb_rows), 1)
        same_chunk_causal = ((r_idx // CHUNK) == (c_idx // CHUNK)) & (r_idx >= c_idx)

        def superblock(sb, carry):
            row0 = pl.multiple_of(sb * sb_rows, sb_rows)
            rows = pl.ds(row0, sb_rows)
            decays = [jnp.exp(bc_s[pl.ds(row0 + (c * CHUNK + CHUNK - 1), 1), :]) for c in range(SB_CHUNKS)]
            for hd in range(heads):
                kc = slice(hd * dk, (hd + 1) * dk)
                vc = slice(hd * dv, (hd + 1) * dv)
                qd = qd_s[rows, kc]
                kt = kt_s[rows, kc]
                v = v_s[rows, vc]
                scores = jnp.where(same_chunk_causal, _dot_nt(qd, ki_s[rows, kc]), 0.0).astype(BF16)
                o = _dot(scores, v)
                state_t = st_s[hd]
                inter = []
                for c in range(SB_CHUNKS):
                    cr = slice(c * CHUNK, (c + 1) * CHUNK)
                    inter.append(_dot_nt(qd[cr], state_t.astype(BF16)))
                    state_t = state_t * decays[c][:, kc] + _dot_tn(v[cr], kt[cr])
                st_s[hd] = state_t
                o = _rmsnorm(o + jnp.concatenate(inter, axis=0), gn_ref[...])
                y_ref[rows, d_pool + hd * dv:d_pool + (hd + 1) * dv] = (o * sg_s[rows, vc]).astype(y_ref.dtype)
            return carry

        lax.fori_loop(0, tt // sb_rows, superblock, 0)


def _mixer(h, gain, w_in_t, w_gate_t, w_pool, pool_scale, w_alpha, b_alpha, gla_norm, *,
           seq, d_pool, dk, dv, tt=512, wc=512):
    t, d = h.shape
    heads = GLA_HEADS
    d_gla = heads * dv
    n_main = d_pool + 2 * heads * dk + 2 * d_gla
    n_cast = n_main // wc
    nt = seq // tt
    n_tiles = t // tt
    group_dim = d_pool // N_POOL_GROUPS
    assert seq % tt == 0 and tt % (SB_CHUNKS * CHUNK) == 0 and n_main % wc == 0 and heads * dk == wc
    assert d_pool % wc == 0 and d_gla % wc == 0 and wc % group_dim == 0
    tile = lambda s: jnp.maximum(s - n_cast, 0)
    const = lambda s: (0, 0)
    return pl.pallas_call(
        functools.partial(_mixer_body, n_cast=n_cast, nt=nt, d_pool=d_pool, dk=dk, dv=dv),
        grid=(n_cast + n_tiles,),
        in_specs=[
            pl.BlockSpec((tt, d), lambda s: (tile(s), 0)),
            pl.BlockSpec((1, d), const),
            pl.BlockSpec((wc, d), lambda s: (jnp.minimum(s, n_cast - 1), 0)),
            pl.BlockSpec((LANES, d), const),
            pl.BlockSpec((N_POOL_GROUPS, group_dim, group_dim), lambda s: (0, 0, 0)),
            pl.BlockSpec((1, d_pool), const),
            pl.BlockSpec((LANES, heads * dk), const),
            pl.BlockSpec((1, heads * dk), const),
            pl.BlockSpec((1, dv), const),
        ],
        out_specs=pl.BlockSpec((tt, d_pool + d_gla), lambda s: (tile(s), 0)),
        out_shape=jax.ShapeDtypeStruct((t, d_pool + d_gla), BF16),
        scratch_shapes=[
            pltpu.VMEM((n_cast, wc, d), BF16),
            pltpu.VMEM((tt, heads * dk), BF16),
            pltpu.VMEM((tt, heads * dk), BF16),
            pltpu.VMEM((tt, heads * dk), BF16),
            pltpu.VMEM((tt, heads * dk), F32),
            pltpu.VMEM((tt, d_gla), BF16),
            pltpu.VMEM((tt, d_gla), F32),
            pltpu.VMEM((heads, dv, dk), F32),
            pltpu.VMEM((POOL_HALO, d_pool), F32),
        ],
        compiler_params=_params(("arbitrary",)),
        name="mixer",
    )(h, gain, w_in_t, w_gate_t, w_pool, pool_scale, w_alpha, b_alpha, gla_norm)


def _mix_out_body(h_ref, y_ref, w_ref, o_ref, w_bf_s):
    @pl.when(pl.program_id(0) == 0)
    def _cast_weights():
        w_bf_s[...] = w_ref[...].astype(BF16)

    o_ref[...] = h_ref[...] + _dot(y_ref[...], w_bf_s[...])


def _mix_out(h, y, w_out, *, tm=512):
    t, d = h.shape
    d_mix = y.shape[1]
    assert t % tm == 0 and w_out.shape == (d_mix, d)
    return pl.pallas_call(
        _mix_out_body,
        grid=(t // tm,),
        in_specs=[
            pl.BlockSpec((tm, d), lambda i: (i, 0)),
            pl.BlockSpec((tm, d_mix), lambda i: (i, 0)),
            pl.BlockSpec((d_mix, d), lambda i: (0, 0), pipeline_mode=pl.Buffered(1)),
        ],
        out_specs=pl.BlockSpec((tm, d), lambda i: (i, 0)),
        out_shape=jax.ShapeDtypeStruct((t, d), F32),
        scratch_shapes=[pltpu.VMEM((d_mix, d), BF16)],
        compiler_params=_params(("arbitrary",)),
        name="mix_out",
    )(h, y, w_out)


def kernel(x, ffn1_norm, ffn1_w_in, ffn1_w_out, mix_norm, w_in_mix, w_pool, pool_scale,
           w_alpha, b_alpha, gla_norm, w_out_mix, ffn2_norm, ffn2_w_in, ffn2_w_out, final_norm):
    b, s, d = x.shape
    depth = ffn1_norm.shape[0]
    d_pool = pool_scale.shape[1]
    dk_total = w_alpha.shape[2]
    dk = dk_total // GLA_HEADS
    dv = gla_norm.shape[1]
    d_gla = GLA_HEADS * dv
    n_main = d_pool + 2 * dk_total + 2 * d_gla
    assert w_in_mix.shape[2] == n_main + GLA_GATE_RANK and s % CHUNK == 0 and depth >= 1

    h = x.reshape(b * s, d)
    final_gain = final_norm.reshape(1, d)
    for l in range(depth):
        last = l == depth - 1
        h = _ffn(h, ffn1_norm[l].reshape(1, d), ffn1_w_in[l], ffn1_w_out[l], final_gain,
                 final_norm=False)

        w_mix_t = jnp.swapaxes(w_in_mix[l], 0, 1)
        w_gate_t = jnp.pad(w_mix_t[n_main:, :], ((0, LANES - GLA_GATE_RANK), (0, 0))).astype(BF16)
        w_alpha_pad = jnp.pad(w_alpha[l], ((0, LANES - GLA_GATE_RANK), (0, 0))).astype(BF16)
        y = _mixer(h, mix_norm[l].reshape(1, d), w_mix_t, w_gate_t, w_pool[l],
                   pool_scale[l].reshape(1, d_pool), w_alpha_pad, b_alpha[l].reshape(1, dk_total),
                   gla_norm[l].reshape(1, dv), seq=s, d_pool=d_pool, dk=dk, dv=dv)
        h = _mix_out(h, y, w_out_mix[l])

        h = _ffn(h, ffn2_norm[l].reshape(1, d), ffn2_w_in[l], ffn2_w_out[l], final_gain,
                 final_norm=last)
    return h.reshape(b, s, d)
```

```python
import functools

import jax
import jax.numpy as jnp
from jax import lax
from jax.experimental import pallas as pl
from jax.experimental.pallas import tpu as pltpu

F32 = jnp.float32
BF16 = jnp.bfloat16

EPS = 1e-6
POOL_WINDOWS = (2, 4, 8, 16)
N_POOL_GROUPS = len(POOL_WINDOWS)
POOL_HALO = 16
GLA_HEADS = 4
GLA_GATE_RANK = 16
GATE_LOGIT_NORMALIZER = 16.0
CHUNK = 64
SB_CHUNKS = 4

LANES = 128
VMEM_LIMIT_BYTES = 58 * 1024 * 1024


def _rmsnorm(x, gain):
    ms = jnp.mean(x * x, axis=-1, keepdims=True)
    return x * lax.rsqrt(ms + EPS) * gain


def _silu(x):
    return x * (1.0 / (1.0 + jnp.exp(-x)))


def _log_sigmoid(z):
    return jnp.minimum(z, 0.0) - jnp.log1p(jnp.exp(-jnp.abs(z)))


def _dot(a, b):
    return jnp.dot(a, b, preferred_element_type=F32)


def _dot_nt(a, b):
    return lax.dot_general(a, b, (((1,), (1,)), ((), ())), preferred_element_type=F32)


def _dot_tn(a, b):
    return lax.dot_general(a, b, (((0,), (0,)), ((), ())), preferred_element_type=F32)


def _params(semantics):
    return pltpu.CompilerParams(dimension_semantics=semantics, vmem_limit_bytes=VMEM_LIMIT_BYTES)


def _ffn_step(j, nj, x_ref, gain_ref, weights, fgain_ref, o_ref, n_ref, final_norm):
    def slice_update(n):
        wg, wu, wo = weights()
        gate = _dot(n, wg)
        up = _dot(n, wu)
        act = (0.5 * _silu(gate) * up).astype(BF16)
        return _dot(act, wo)

    @pl.when(j == 0)
    def _first():
        x = x_ref[...]
        n = _rmsnorm(x, gain_ref[...]).astype(BF16)
        n_ref[...] = n
        o_ref[...] = x + slice_update(n)

    @pl.when(j > 0)
    def _rest():
        o_ref[...] += slice_update(n_ref[...])

    if final_norm:

        @pl.when(j == nj - 1)
        def _finish():
            o_ref[...] = _rmsnorm(o_ref[...], fgain_ref[...])


def _ffn_head_body(x_ref, gain_ref, wg_ref, wu_ref, wo_ref, fgain_ref,
                   o_ref, wg_bf_ref, wu_bf_ref, wo_bf_ref, n_ref, *, final_norm):
    def weights():
        wg = wg_ref[...].astype(BF16)
        wu = wu_ref[...].astype(BF16)
        wo = wo_ref[...].astype(BF16)
        wg_bf_ref[...] = wg
        wu_bf_ref[...] = wu
        wo_bf_ref[...] = wo
        return wg, wu, wo

    _ffn_step(pl.program_id(0), pl.num_programs(0), x_ref, gain_ref, weights, fgain_ref,
              o_ref, n_ref, final_norm)


def _ffn_tail_body(x_ref, gain_ref, wg_ref, wu_ref, wo_ref, fgain_ref, head_ref,
                   o_ref, n_ref, *, final_norm, n_copy):
    i = pl.program_id(0)
    j = pl.program_id(1)
    rc = head_ref.shape[0]

    @pl.when((i == 0) & (j < n_copy))
    def _copy_head_rows():
        o_ref[pl.ds(pl.multiple_of(j * rc, rc), rc), :] = head_ref[...]

    @pl.when(i > 0)
    def _compute():
        _ffn_step(j, pl.num_programs(1), x_ref, gain_ref,
                  lambda: (wg_ref[...], wu_ref[...], wo_ref[...]), fgain_ref, o_ref, n_ref,
                  final_norm)


def _ffn(x, gain, w_in, w_out, final_gain, *, final_norm, tm=1024, tf_head=256, tf_tail=512,
         copy_rows=128):
    t, d = x.shape
    d_ff = w_out.shape[0]
    assert t % tm == 0 and t // tm >= 2 and w_in.shape == (d, 2 * d_ff)
    assert d_ff % tf_head == 0 and d_ff % tf_tail == 0
    nj = d_ff // tf_head
    vec = pl.BlockSpec((1, d), lambda j: (0, 0))
    h_head, wg_bf, wu_bf, wo_bf = pl.pallas_call(
        functools.partial(_ffn_head_body, final_norm=final_norm),
        grid=(nj,),
        in_specs=[
            pl.BlockSpec((tm, d), lambda j: (0, 0), pipeline_mode=pl.Buffered(1)),
            vec,
            pl.BlockSpec((d, tf_head), lambda j: (0, j)),
            pl.BlockSpec((d, tf_head), lambda j: (0, j + nj)),
            pl.BlockSpec((tf_head, d), lambda j: (j, 0)),
            vec,
        ],
        out_specs=[
            pl.BlockSpec((tm, d), lambda j: (0, 0)),
            pl.BlockSpec((d, tf_head), lambda j: (0, j)),
            pl.BlockSpec((d, tf_head), lambda j: (0, j)),
            pl.BlockSpec((tf_head, d), lambda j: (j, 0)),
        ],
        out_shape=[
            jax.ShapeDtypeStruct((tm, d), F32),
            jax.ShapeDtypeStruct((d, d_ff), BF16),
            jax.ShapeDtypeStruct((d, d_ff), BF16),
            jax.ShapeDtypeStruct((d_ff, d), BF16),
        ],
        scratch_shapes=[pltpu.VMEM((tm, d), BF16)],
        compiler_params=_params(("arbitrary",)),
        name="ffn_head_final" if final_norm else "ffn_head",
    )(x, gain, w_in, w_in, w_out, final_gain)

    nj_tail = d_ff // tf_tail
    n_copy = tm // copy_rows
    assert tm % copy_rows == 0 and n_copy <= nj_tail
    vec2 = pl.BlockSpec((1, d), lambda i, j: (0, 0))
    wcol = lambda i, j: (0, j * jnp.minimum(i, 1))
    wrow = lambda i, j: (j * jnp.minimum(i, 1), 0)
    head_rows = lambda i, j: (jnp.where(i == 0, jnp.minimum(j, n_copy - 1), n_copy - 1), 0)
    return pl.pallas_call(
        functools.partial(_ffn_tail_body, final_norm=final_norm, n_copy=n_copy),
        grid=(t // tm, nj_tail),
        in_specs=[
            pl.BlockSpec((tm, d), lambda i, j: (jnp.maximum(i, 1), 0)),
            vec2,
            pl.BlockSpec((d, tf_tail), wcol),
            pl.BlockSpec((d, tf_tail), wcol),
            pl.BlockSpec((tf_tail, d), wrow),
            vec2,
            pl.BlockSpec((copy_rows, d), head_rows, pipeline_mode=pl.Buffered(1)),
        ],
        out_specs=pl.BlockSpec((tm, d), lambda i, j: (i, 0)),
        out_shape=jax.ShapeDtypeStruct((t, d), F32),
        scratch_shapes=[pltpu.VMEM((tm, d), BF16)],
        compiler_params=_params(("parallel", "arbitrary")),
        name="ffn_tail_final" if final_norm else "ffn_tail",
    )(x, gain, wg_bf, wu_bf, wo_bf, final_gain, h_head)


def _pool_group(x, halo, window, w, scale, t_first):
    tt = x.shape[0]
    ext = jnp.concatenate([halo, x], axis=0)
    row = lax.broadcasted_iota(jnp.int32, ext.shape, 0)
    acc = ext
    span = 1
    while span < window:
        acc = acc + jnp.where(row >= span, pltpu.roll(acc, span, axis=0), 0.0)
        span *= 2
    cnt = jnp.clip(row + (t_first + 1 - POOL_HALO), 1, window).astype(F32)
    pooled = (acc / cnt - ext)[POOL_HALO:, :].astype(BF16)
    return _dot(pooled, w) * scale


def _mixer_body(h_ref, gain_ref, w_ref, wgate_ref, wpool_ref, pscale_ref, wa_ref, ba_ref, gn_ref,
                y_ref,
                win_s, qd_s, ki_s, kt_s, bc_s, v_s, sg_s, st_s, halo_s,
                *, n_cast, nt, d_pool, dk, dv):
    s = pl.program_id(0)
    tt = h_ref.shape[0]
    wc = w_ref.shape[0]
    heads = GLA_HEADS
    group_dim = d_pool // N_POOL_GROUPS
    pool_chunks = d_pool // wc
    q_chunk = pool_chunks
    k_chunk = q_chunk + 1
    v_chunk0 = k_chunk + 1
    v_chunks = heads * dv // wc
    g_chunk0 = v_chunk0 + v_chunks

    @pl.when(s < n_cast)
    def _cast_weights():
        win_s[s] = w_ref[...].astype(BF16)

    @pl.when(s >= n_cast)
    def _compute():
        t = lax.rem(s - n_cast, nt)

        @pl.when(t == 0)
        def _reset():
            st_s[...] = jnp.zeros_like(st_s)
            halo_s[...] = jnp.zeros_like(halo_s)

        n = _rmsnorm(h_ref[...], gain_ref[...]).astype(BF16)

        per_chunk = wc // group_dim
        for c in range(pool_chunks):
            u_c = _dot_nt(n, win_s[c])
            for part in range(per_chunk):
                gi = c * per_chunk + part
                cols = slice(gi * group_dim, (gi + 1) * group_dim)
                x = u_c[:, part * group_dim:(part + 1) * group_dim]
                y = _pool_group(x, halo_s[:, cols], POOL_WINDOWS[gi], wpool_ref[gi].astype(BF16),
                                pscale_ref[:, cols], t * tt)
                halo_s[:, cols] = x[tt - POOL_HALO:, :]
                y_ref[:, cols] = y.astype(y_ref.dtype)

        gate = _dot_nt(n, wgate_ref[...])
        logits = _dot(gate.astype(BF16), wa_ref[...]) + ba_ref[...]
        log_alpha = _log_sigmoid(logits) * (1.0 / GATE_LOGIT_NORMALIZER)
        pos = lax.broadcasted_iota(jnp.int32, log_alpha.shape, 0) & (CHUNK - 1)
        bcum = log_alpha
        span = 1
        while span < CHUNK:
            bcum = bcum + jnp.where(pos >= span, pltpu.roll(bcum, span, axis=0), 0.0)
            span *= 2
        bc_s[...] = bcum
        n_chunks = tt // CHUNK
        b3 = bcum.reshape(n_chunks, CHUNK, heads * dk)
        b_last = jnp.broadcast_to(b3[:, CHUNK - 1:CHUNK, :], b3.shape).reshape(tt, heads * dk)
        q = _dot_nt(n, win_s[q_chunk])
        qd_s[...] = (q * (dk ** -0.5) * jnp.exp(bcum)).astype(BF16)
        k = _dot_nt(n, win_s[k_chunk])
        ki_s[...] = (k * jnp.exp(-bcum)).astype(BF16)
        kt_s[...] = (k * jnp.exp(b_last - bcum)).astype(BF16)
        for c in range(v_chunks):
            cols = slice(c * wc, (c + 1) * wc)
            v_s[:, cols] = _dot_nt(n, win_s[v_chunk0 + c]).astype(BF16)
            sg_s[:, cols] = _silu(_dot_nt(n, win_s[g_chunk0 + c]))

        sb_rows = SB_CHUNKS * CHUNK
        r_idx = lax.broadcasted_iota(jnp.int32, (sb_rows, sb_rows), 0)
        c_idx = lax.broadcasted_iota(jnp.int32, (sb_rows, sb_rows), 1)
        same_chunk_causal = ((r_idx // CHUNK) == (c_idx // CHUNK)) & (r_idx >= c_idx)

        def superblock(sb, carry):
            row0 = pl.multiple_of(sb * sb_rows, sb_rows)
            rows = pl.ds(row0, sb_rows)
            decays = [jnp.exp(bc_s[pl.ds(row0 + (c * CHUNK + CHUNK - 1), 1), :]) for c in range(SB_CHUNKS)]
            for hd in range(heads):
                kc = slice(hd * dk, (hd + 1) * dk)
                vc = slice(hd * dv, (hd + 1) * dv)
                qd = qd_s[rows, kc]
                kt = kt_s[rows, kc]
                v = v_s[rows, vc]
                scores = jnp.where(same_chunk_causal, _dot_nt(qd, ki_s[rows, kc]), 0.0).astype(BF16)
                o = _dot(scores, v)
                state_t = st_s[hd]
                inter = []
                for c in range(SB_CHUNKS):
                    cr = slice(c * CHUNK, (c + 1) * CHUNK)
                    inter.append(_dot_nt(qd[cr], state_t.astype(BF16)))
                    state_t = state_t * decays[c][:, kc] + _dot_tn(v[cr], kt[cr])
                st_s[hd] = state_t
                o = _rmsnorm(o + jnp.concatenate(inter, axis=0), gn_ref[...])
                y_ref[rows, d_pool + hd * dv:d_pool + (hd + 1) * dv] = (o * sg_s[rows, vc]).astype(y_ref.dtype)
            return carry

        lax.fori_loop(0, tt // sb_rows, superblock, 0)


def _mixer(h, gain, w_in_t, w_gate_t, w_pool, pool_scale, w_alpha, b_alpha, gla_norm, *,
           seq, d_pool, dk, dv, tt=512, wc=512):
    t, d = h.shape
    heads = GLA_HEADS
    d_gla = heads * dv
    n_main = d_pool + 2 * heads * dk + 2 * d_gla
    n_cast = n_main // wc
    nt = seq // tt
    n_tiles = t // tt
    group_dim = d_pool // N_POOL_GROUPS
    assert seq % tt == 0 and tt % (SB_CHUNKS * CHUNK) == 0 and n_main % wc == 0 and heads * dk == wc
    assert d_pool % wc == 0 and d_gla % wc == 0 and wc % group_dim == 0
    tile = lambda s: jnp.maximum(s - n_cast, 0)
    const = lambda s: (0, 0)
    return pl.pallas_call(
        functools.partial(_mixer_body, n_cast=n_cast, nt=nt, d_pool=d_pool, dk=dk, dv=dv),
        grid=(n_cast + n_tiles,),
        in_specs=[
            pl.BlockSpec((tt, d), lambda s: (tile(s), 0)),
            pl.BlockSpec((1, d), const),
            pl.BlockSpec((wc, d), lambda s: (jnp.minimum(s, n_cast - 1), 0)),
            pl.BlockSpec((LANES, d), const),
            pl.BlockSpec((N_POOL_GROUPS, group_dim, group_dim), lambda s: (0, 0, 0)),
            pl.BlockSpec((1, d_pool), const),
            pl.BlockSpec((LANES, heads * dk), const),
            pl.BlockSpec((1, heads * dk), const),
            pl.BlockSpec((1, dv), const),
        ],
        out_specs=pl.BlockSpec((tt, d_pool + d_gla), lambda s: (tile(s), 0)),
        out_shape=jax.ShapeDtypeStruct((t, d_pool + d_gla), BF16),
        scratch_shapes=[
            pltpu.VMEM((n_cast, wc, d), BF16),
            pltpu.VMEM((tt, heads * dk), BF16),
            pltpu.VMEM((tt, heads * dk), BF16),
            pltpu.VMEM((tt, heads * dk), BF16),
            pltpu.VMEM((tt, heads * dk), F32),
            pltpu.VMEM((tt, d_gla), BF16),
            pltpu.VMEM((tt, d_gla), F32),
            pltpu.VMEM((heads, dv, dk), F32),
            pltpu.VMEM((POOL_HALO, d_pool), F32),
        ],
        compiler_params=_params(("arbitrary",)),
        name="mixer",
    )(h, gain, w_in_t, w_gate_t, w_pool, pool_scale, w_alpha, b_alpha, gla_norm)


def _mix_out_body(h_ref, y_ref, w_ref, o_ref, w_bf_s):
    @pl.when(pl.program_id(0) == 0)
    def _cast_weights():
        w_bf_s[...] = w_ref[...].astype(BF16)

    o_ref[...] = h_ref[...] + _dot(y_ref[...], w_bf_s[...])


def _mix_out(h, y, w_out, *, tm=512):
    t, d = h.shape
    d_mix = y.shape[1]
    assert t % tm == 0 and w_out.shape == (d_mix, d)
    return pl.pallas_call(
        _mix_out_body,
        grid=(t // tm,),
        in_specs=[
            pl.BlockSpec((tm, d), lambda i: (i, 0)),
            pl.BlockSpec((tm, d_mix), lambda i: (i, 0)),
            pl.BlockSpec((d_mix, d), lambda i: (0, 0), pipeline_mode=pl.Buffered(1)),
        ],
        out_specs=pl.BlockSpec((tm, d), lambda i: (i, 0)),
        out_shape=jax.ShapeDtypeStruct((t, d), F32),
        scratch_shapes=[pltpu.VMEM((d_mix, d), BF16)],
        compiler_params=_params(("arbitrary",)),
        name="mix_out",
    )(h, y, w_out)


def kernel(x, ffn1_norm, ffn1_w_in, ffn1_w_out, mix_norm, w_in_mix, w_pool, pool_scale,
           w_alpha, b_alpha, gla_norm, w_out_mix, ffn2_norm, ffn2_w_in, ffn2_w_out, final_norm):
    b, s, d = x.shape
    depth = ffn1_norm.shape[0]
    d_pool = pool_scale.shape[1]
    dk_total = w_alpha.shape[2]
    dk = dk_total // GLA_HEADS
    dv = gla_norm.shape[1]
    d_gla = GLA_HEADS * dv
    n_main = d_pool + 2 * dk_total + 2 * d_gla
    assert w_in_mix.shape[2] == n_main + GLA_GATE_RANK and s % CHUNK == 0 and depth >= 1

    h = x.reshape(b * s, d)
    final_gain = final_norm.reshape(1, d)
    for l in range(depth):
        last = l == depth - 1
        h = _ffn(h, ffn1_norm[l].reshape(1, d), ffn1_w_in[l], ffn1_w_out[l], final_gain,
                 final_norm=False)

        w_mix_t = jnp.swapaxes(w_in_mix[l], 0, 1)
        w_gate_t = jnp.pad(w_mix_t[n_main:, :], ((0, LANES - GLA_GATE_RANK), (0, 0))).astype(BF16)
        w_alpha_pad = jnp.pad(w_alpha[l], ((0, LANES - GLA_GATE_RANK), (0, 0))).astype(BF16)
        y = _mixer(h, mix_norm[l].reshape(1, d), w_mix_t, w_gate_t, w_pool[l],
                   pool_scale[l].reshape(1, d_pool), w_alpha_pad, b_alpha[l].reshape(1, dk_total),
                   gla_norm[l].reshape(1, dv), seq=s, d_pool=d_pool, dk=dk, dv=dv)
        h = _mix_out(h, y, w_out_mix[l])

        h = _ffn(h, ffn2_norm[l].reshape(1, d), ffn2_w_in[l], ffn2_w_out[l], final_gain,
                 final_norm=last)
    return h.reshape(b, s, d)
```

```python
import functools

import jax
import jax.numpy as jnp
from jax import lax
from jax.experimental import pallas as pl
from jax.experimental.pallas import tpu as pltpu

F32 = jnp.float32
BF16 = jnp.bfloat16

EPS = 1e-6
POOL_WINDOWS = (2, 4, 8, 16)
N_POOL_GROUPS = len(POOL_WINDOWS)
POOL_HALO = 16
GLA_HEADS = 4
GLA_GATE_RANK = 16
GATE_LOGIT_NORMALIZER = 16.0
CHUNK = 64
SB_CHUNKS = 4
HEAD_WEIGHT_SLOTS = 3

LANES = 128
VMEM_LIMIT_BYTES = 58 * 1024 * 1024


def _rmsnorm(x, gain):
    ms = jnp.mean(x * x, axis=-1, keepdims=True)
    return x * lax.rsqrt(ms + EPS) * gain


def _silu(x):
    return x * (1.0 / (1.0 + jnp.exp(-x)))


def _log_sigmoid(z):
    return jnp.minimum(z, 0.0) - jnp.log1p(jnp.exp(-jnp.abs(z)))


def _dot(a, b):
    return jnp.dot(a, b, preferred_element_type=F32)


def _dot_nt(a, b):
    return lax.dot_general(a, b, (((1,), (1,)), ((), ())), preferred_element_type=F32)


def _dot_tn(a, b):
    return lax.dot_general(a, b, (((0,), (0,)), ((), ())), preferred_element_type=F32)


def _params(semantics):
    return pltpu.CompilerParams(dimension_semantics=semantics, vmem_limit_bytes=VMEM_LIMIT_BYTES)


def _ffn_step(j, nj, x_ref, gain_ref, weights, fgain_ref, o_ref, n_ref, final_norm):
    def slice_update(n):
        wg, wu, wo = weights()
        gate = _dot(n, wg)
        up = _dot(n, wu)
        act = (0.5 * _silu(gate) * up).astype(BF16)
        return _dot(act, wo)

    @pl.when(j == 0)
    def _first():
        x = x_ref[...]
        n = _rmsnorm(x, gain_ref[...]).astype(BF16)
        n_ref[...] = n
        o_ref[...] = x + slice_update(n)

    @pl.when(j > 0)
    def _rest():
        o_ref[...] += slice_update(n_ref[...])

    if final_norm:

        @pl.when(j == nj - 1)
        def _finish():
            o_ref[...] = _rmsnorm(o_ref[...], fgain_ref[...])


def _ffn_head_body(x_ref, gain_ref, w_in_hbm, w_out_hbm, fgain_ref,
                   o_ref, wg_bf_ref, wu_bf_ref, wo_bf_ref,
                   n_ref, wg_buf, wu_buf, wo_buf, sems, *, final_norm):
    j = pl.program_id(0)
    nj = pl.num_programs(0)
    n_slots, _, tf = wg_buf.shape
    d_ff = w_out_hbm.shape[0]
    lookahead = n_slots - 1

    def slice_copies(step, slot):
        lo = pl.multiple_of(step * tf, tf)
        return (
            pltpu.make_async_copy(w_in_hbm.at[:, pl.ds(lo, tf)], wg_buf.at[slot], sems.at[0, slot]),
            pltpu.make_async_copy(w_in_hbm.at[:, pl.ds(d_ff + lo, tf)], wu_buf.at[slot], sems.at[1, slot]),
            pltpu.make_async_copy(w_out_hbm.at[pl.ds(lo, tf), :], wo_buf.at[slot], sems.at[2, slot]),
        )

    @pl.when(j == 0)
    def _prime():
        for step in range(lookahead):
            for c in slice_copies(step, step):
                c.start()

    @pl.when(j + lookahead < nj)
    def _prefetch():
        for c in slice_copies(j + lookahead, lax.rem(j + lookahead, n_slots)):
            c.start()

    slot = lax.rem(j, n_slots)
    for c in slice_copies(j, slot):
        c.wait()

    def weights():
        wg = wg_buf[slot].astype(BF16)
        wu = wu_buf[slot].astype(BF16)
        wo = wo_buf[slot].astype(BF16)
        wg_bf_ref[...] = wg
        wu_bf_ref[...] = wu
        wo_bf_ref[...] = wo
        return wg, wu, wo

    _ffn_step(j, nj, x_ref, gain_ref, weights, fgain_ref, o_ref, n_ref, final_norm)


def _ffn_tail_body(x_ref, gain_ref, wg_ref, wu_ref, wo_ref, fgain_ref, head_ref,
                   o_ref, n_ref, *, final_norm, n_copy):
    i = pl.program_id(0)
    j = pl.program_id(1)
    rc = head_ref.shape[0]

    @pl.when((i == 0) & (j < n_copy))
    def _copy_head_rows():
        o_ref[pl.ds(pl.multiple_of(j * rc, rc), rc), :] = head_ref[...]

    @pl.when(i > 0)
    def _compute():
        _ffn_step(j, pl.num_programs(1), x_ref, gain_ref,
                  lambda: (wg_ref[...], wu_ref[...], wo_ref[...]), fgain_ref, o_ref, n_ref,
                  final_norm)


def _ffn(x, gain, w_in, w_out, final_gain, *, final_norm, tm=1024, tf_head=256, tf_tail=512,
         copy_rows=128):
    t, d = x.shape
    d_ff = w_out.shape[0]
    assert t % tm == 0 and t // tm >= 2 and w_in.shape == (d, 2 * d_ff)
    assert d_ff % tf_head == 0 and d_ff % tf_tail == 0
    nj = d_ff // tf_head
    vec = pl.BlockSpec((1, d), lambda j: (0, 0))
    h_head, wg_bf, wu_bf, wo_bf = pl.pallas_call(
        functools.partial(_ffn_head_body, final_norm=final_norm),
        grid=(nj,),
        in_specs=[
            pl.BlockSpec((tm, d), lambda j: (0, 0), pipeline_mode=pl.Buffered(1)),
            vec,
            pl.BlockSpec(memory_space=pl.ANY),
            pl.BlockSpec(memory_space=pl.ANY),
            vec,
        ],
        out_specs=[
            pl.BlockSpec((tm, d), lambda j: (0, 0)),
            pl.BlockSpec((d, tf_head), lambda j: (0, j)),
            pl.BlockSpec((d, tf_head), lambda j: (0, j)),
            pl.BlockSpec((tf_head, d), lambda j: (j, 0)),
        ],
        out_shape=[
            jax.ShapeDtypeStruct((tm, d), F32),
            jax.ShapeDtypeStruct((d, d_ff), BF16),
            jax.ShapeDtypeStruct((d, d_ff), BF16),
            jax.ShapeDtypeStruct((d_ff, d), BF16),
        ],
        scratch_shapes=[
            pltpu.VMEM((tm, d), BF16),
            pltpu.VMEM((HEAD_WEIGHT_SLOTS, d, tf_head), F32),
            pltpu.VMEM((HEAD_WEIGHT_SLOTS, d, tf_head), F32),
            pltpu.VMEM((HEAD_WEIGHT_SLOTS, tf_head, d), F32),
            pltpu.SemaphoreType.DMA((3, HEAD_WEIGHT_SLOTS)),
        ],
        compiler_params=_params(("arbitrary",)),
        name="ffn_head_final" if final_norm else "ffn_head",
    )(x, gain, w_in, w_out, final_gain)

    nj_tail = d_ff // tf_tail
    n_copy = tm // copy_rows
    assert tm % copy_rows == 0 and n_copy <= nj_tail
    vec2 = pl.BlockSpec((1, d), lambda i, j: (0, 0))
    wcol = lambda i, j: (0, j * jnp.minimum(i, 1))
    wrow = lambda i, j: (j * jnp.minimum(i, 1), 0)
    head_rows = lambda i, j: (jnp.where(i == 0, jnp.minimum(j, n_copy - 1), n_copy - 1), 0)
    return pl.pallas_call(
        functools.partial(_ffn_tail_body, final_norm=final_norm, n_copy=n_copy),
        grid=(t // tm, nj_tail),
        in_specs=[
            pl.BlockSpec((tm, d), lambda i, j: (jnp.maximum(i, 1), 0)),
            vec2,
            pl.BlockSpec((d, tf_tail), wcol),
            pl.BlockSpec((d, tf_tail), wcol),
            pl.BlockSpec((tf_tail, d), wrow),
            vec2,
            pl.BlockSpec((copy_rows, d), head_rows, pipeline_mode=pl.Buffered(1)),
        ],
        out_specs=pl.BlockSpec((tm, d), lambda i, j: (i, 0)),
        out_shape=jax.ShapeDtypeStruct((t, d), F32),
        scratch_shapes=[pltpu.VMEM((tm, d), BF16)],
        compiler_params=_params(("parallel", "arbitrary")),
        name="ffn_tail_final" if final_norm else "ffn_tail",
    )(x, gain, wg_bf, wu_bf, wo_bf, final_gain, h_head)


def _pool_group(x, halo, window, w, scale, t_first):
    tt = x.shape[0]
    ext = jnp.concatenate([halo, x], axis=0)
    row = lax.broadcasted_iota(jnp.int32, ext.shape, 0)
    acc = ext
    span = 1
    while span < window:
        acc = acc + jnp.where(row >= span, pltpu.roll(acc, span, axis=0), 0.0)
        span *= 2
    cnt = jnp.clip(row + (t_first + 1 - POOL_HALO), 1, window).astype(F32)
    pooled = (acc / cnt - ext)[POOL_HALO:, :].astype(BF16)
    return _dot(pooled, w) * scale


def _mixer_body(h_ref, gain_ref, w_ref, wgate_ref, wpool_ref, pscale_ref, wa_ref, ba_ref, gn_ref,
                y_ref,
                win_s, qd_s, ki_s, kt_s, bc_s, v_s, sg_s, st_s, halo_s,
                *, n_cast, nt, d_pool, dk, dv):
    s = pl.program_id(0)
    tt = h_ref.shape[0]
    wc = w_ref.shape[0]
    heads = GLA_HEADS
    group_dim = d_pool // N_POOL_GROUPS
    pool_chunks = d_pool // wc
    q_chunk = pool_chunks
    k_chunk = q_chunk + 1
    v_chunk0 = k_chunk + 1
    v_chunks = heads * dv // wc
    g_chunk0 = v_chunk0 + v_chunks

    @pl.when(s < n_cast)
    def _cast_weights():
        win_s[s] = w_ref[...].astype(BF16)

    @pl.when(s >= n_cast)
    def _compute():
        t = lax.rem(s - n_cast, nt)

        @pl.when(t == 0)
        def _reset():
            st_s[...] = jnp.zeros_like(st_s)
            halo_s[...] = jnp.zeros_like(halo_s)

        n = _rmsnorm(h_ref[...], gain_ref[...]).astype(BF16)

        per_chunk = wc // group_dim
        for c in range(pool_chunks):
            u_c = _dot_nt(n, win_s[c])
            for part in range(per_chunk):
                gi = c * per_chunk + part
                cols = slice(gi * group_dim, (gi + 1) * group_dim)
                x = u_c[:, part * group_dim:(part + 1) * group_dim]
                y = _pool_group(x, halo_s[:, cols], POOL_WINDOWS[gi], wpool_ref[gi].astype(BF16),
                                pscale_ref[:, cols], t * tt)
                halo_s[:, cols] = x[tt - POOL_HALO:, :]
                y_ref[:, cols] = y.astype(y_ref.dtype)

        gate = _dot_nt(n, wgate_ref[...])
        logits = _dot(gate.astype(BF16), wa_ref[...]) + ba_ref[...]
        log_alpha = _log_sigmoid(logits) * (1.0 / GATE_LOGIT_NORMALIZER)
        pos = lax.broadcasted_iota(jnp.int32, log_alpha.shape, 0) & (CHUNK - 1)
        bcum = log_alpha
        span = 1
        while span < CHUNK:
            bcum = bcum + jnp.where(pos >= span, pltpu.roll(bcum, span, axis=0), 0.0)
            span *= 2
        bc_s[...] = bcum
        n_chunks = tt // CHUNK
        b3 = bcum.reshape(n_chunks, CHUNK, heads * dk)
        b_last = jnp.broadcast_to(b3[:, CHUNK - 1:CHUNK, :], b3.shape).reshape(tt, heads * dk)
        q = _dot_nt(n, win_s[q_chunk])
        qd_s[...] = (q * (dk ** -0.5) * jnp.exp(bcum)).astype(BF16)
        k = _dot_nt(n, win_s[k_chunk])
        ki_s[...] = (k * jnp.exp(-bcum)).astype(BF16)
        kt_s[...] = (k * jnp.exp(b_last - bcum)).astype(BF16)
        for c in range(v_chunks):
            cols = slice(c * wc, (c + 1) * wc)
            v_s[:, cols] = _dot_nt(n, win_s[v_chunk0 + c]).astype(BF16)
            sg_s[:, cols] = _silu(_dot_nt(n, win_s[g_chunk0 + c]))

        sb_rows = SB_CHUNKS * CHUNK
        r_idx = lax.broadcasted_iota(jnp.int32, (sb_rows, sb_rows), 0)
        c_idx = lax.broadcasted_iota(jnp.int32, (sb_rows, sb_rows), 1)
        same_chunk_causal = ((r_idx // CHUNK) == (c_idx // CHUNK)) & (r_idx >= c_idx)

        def superblock(sb, carry):
            row0 = pl.multiple_of(sb * sb_rows, sb_rows)
            rows = pl.ds(row0, sb_rows)
            decays = [jnp.exp(bc_s[pl.ds(row0 + (c * CHUNK + CHUNK - 1), 1), :]) for c in range(SB_CHUNKS)]
            for hd in range(heads):
                kc = slice(hd * dk, (hd + 1) * dk)
                vc = slice(hd * dv, (hd + 1) * dv)
                qd = qd_s[rows, kc]
                kt = kt_s[rows, kc]
                v = v_s[rows, vc]
                scores = jnp.where(same_chunk_causal, _dot_nt(qd, ki_s[rows, kc]), 0.0).astype(BF16)
                o = _dot(scores, v)
                state_t = st_s[hd]
                inter = []
                for c in range(SB_CHUNKS):
                    cr = slice(c * CHUNK, (c + 1) * CHUNK)
                    inter.append(_dot_nt(qd[cr], state_t.astype(BF16)))
                    state_t = state_t * decays[c][:, kc] + _dot_tn(v[cr], kt[cr])
                st_s[hd] = state_t
                o = _rmsnorm(o + jnp.concatenate(inter, axis=0), gn_ref[...])
                y_ref[rows, d_pool + hd * dv:d_pool + (hd + 1) * dv] = (o * sg_s[rows, vc]).astype(y_ref.dtype)
            return carry

        lax.fori_loop(0, tt // sb_rows, superblock, 0)


def _mixer(h, gain, w_in_t, w_gate_t, w_pool, pool_scale, w_alpha, b_alpha, gla_norm, *,
           seq, d_pool, dk, dv, tt=512, wc=512):
    t, d = h.shape
    heads = GLA_HEADS
    d_gla = heads * dv
    n_main = d_pool + 2 * heads * dk + 2 * d_gla
    n_cast = n_main // wc
    nt = seq // tt
    n_tiles = t // tt
    group_dim = d_pool // N_POOL_GROUPS
    assert seq % tt == 0 and tt % (SB_CHUNKS * CHUNK) == 0 and n_main % wc == 0 and heads * dk == wc
    assert d_pool % wc == 0 and d_gla % wc == 0 and wc % group_dim == 0
    tile = lambda s: jnp.maximum(s - n_cast, 0)
    const = lambda s: (0, 0)
    return pl.pallas_call(
        functools.partial(_mixer_body, n_cast=n_cast, nt=nt, d_pool=d_pool, dk=dk, dv=dv),
        grid=(n_cast + n_tiles,),
        in_specs=[
            pl.BlockSpec((tt, d), lambda s: (tile(s), 0)),
            pl.BlockSpec((1, d), const),
            pl.BlockSpec((wc, d), lambda s: (jnp.minimum(s, n_cast - 1), 0)),
            pl.BlockSpec((LANES, d), const),
            pl.BlockSpec((N_POOL_GROUPS, group_dim, group_dim), lambda s: (0, 0, 0)),
            pl.BlockSpec((1, d_pool), const),
            pl.BlockSpec((LANES, heads * dk), const),
            pl.BlockSpec((1, heads * dk), const),
            pl.BlockSpec((1, dv), const),
        ],
        out_specs=pl.BlockSpec((tt, d_pool + d_gla), lambda s: (tile(s), 0)),
        out_shape=jax.ShapeDtypeStruct((t, d_pool + d_gla), BF16),
        scratch_shapes=[
            pltpu.VMEM((n_cast, wc, d), BF16),
            pltpu.VMEM((tt, heads * dk), BF16),
            pltpu.VMEM((tt, heads * dk), BF16),
            pltpu.VMEM((tt, heads * dk), BF16),
            pltpu.VMEM((tt, heads * dk), F32),
            pltpu.VMEM((tt, d_gla), BF16),
            pltpu.VMEM((tt, d_gla), F32),
            pltpu.VMEM((heads, dv, dk), F32),
            pltpu.VMEM((POOL_HALO, d_pool), F32),
        ],
        compiler_params=_params(("arbitrary",)),
        name="mixer",
    )(h, gain, w_in_t, w_gate_t, w_pool, pool_scale, w_alpha, b_alpha, gla_norm)


def _mix_out_body(h_ref, y_ref, w_ref, o_ref, w_bf_s):
    @pl.when(pl.program_id(0) == 0)
    def _cast_weights():
        w_bf_s[...] = w_ref[...].astype(BF16)

    o_ref[...] = h_ref[...] + _dot(y_ref[...], w_bf_s[...])


def _mix_out(h, y, w_out, *, tm=512):
    t, d = h.shape
    d_mix = y.shape[1]
    assert t % tm == 0 and w_out.shape == (d_mix, d)
    return pl.pallas_call(
        _mix_out_body,
        grid=(t // tm,),
        in_specs=[
            pl.BlockSpec((tm, d), lambda i: (i, 0)),
            pl.BlockSpec((tm, d_mix), lambda i: (i, 0)),
            pl.BlockSpec((d_mix, d), lambda i: (0, 0), pipeline_mode=pl.Buffered(1)),
        ],
        out_specs=pl.BlockSpec((tm, d), lambda i: (i, 0)),
        out_shape=jax.ShapeDtypeStruct((t, d), F32),
        scratch_shapes=[pltpu.VMEM((d_mix, d), BF16)],
        compiler_params=_params(("arbitrary",)),
        name="mix_out",
    )(h, y, w_out)


def kernel(x, ffn1_norm, ffn1_w_in, ffn1_w_out, mix_norm, w_in_mix, w_pool, pool_scale,
           w_alpha, b_alpha, gla_norm, w_out_mix, ffn2_norm, ffn2_w_in, ffn2_w_out, final_norm):
    b, s, d = x.shape
    depth = ffn1_norm.shape[0]
    d_pool = pool_scale.shape[1]
    dk_total = w_alpha.shape[2]
    dk = dk_total // GLA_HEADS
    dv = gla_norm.shape[1]
    d_gla = GLA_HEADS * dv
    n_main = d_pool + 2 * dk_total + 2 * d_gla
    assert w_in_mix.shape[2] == n_main + GLA_GATE_RANK and s % CHUNK == 0 and depth >= 1

    h = x.reshape(b * s, d)
    final_gain = final_norm.reshape(1, d)
    for l in range(depth):
        last = l == depth - 1
        h = _ffn(h, ffn1_norm[l].reshape(1, d), ffn1_w_in[l], ffn1_w_out[l], final_gain,
                 final_norm=False)

        w_mix_t = jnp.swapaxes(w_in_mix[l], 0, 1)
        w_gate_t = jnp.pad(w_mix_t[n_main:, :], ((0, LANES - GLA_GATE_RANK), (0, 0))).astype(BF16)
        w_alpha_pad = jnp.pad(w_alpha[l], ((0, LANES - GLA_GATE_RANK), (0, 0))).astype(BF16)
        y = _mixer(h, mix_norm[l].reshape(1, d), w_mix_t, w_gate_t, w_pool[l],
                   pool_scale[l].reshape(1, d_pool), w_alpha_pad, b_alpha[l].reshape(1, dk_total),
                   gla_norm[l].reshape(1, dv), seq=s, d_pool=d_pool, dk=dk, dv=dv)
        h = _mix_out(h, y, w_out_mix[l])

        h = _ffn(h, ffn2_norm[l].reshape(1, d), ffn2_w_in[l], ffn2_w_out[l], final_gain,
                 final_norm=last)
    return h.reshape(b, s, d)
```

```python
import functools

import jax
import jax.numpy as jnp
from jax import lax
from jax.experimental import pallas as pl
from jax.experimental.pallas import tpu as pltpu

F32 = jnp.float32
BF16 = jnp.bfloat16

EPS = 1e-6
POOL_WINDOWS = (2, 4, 8, 16)
N_POOL_GROUPS = len(POOL_WINDOWS)
POOL_HALO = 16
GLA_HEADS = 4
GLA_GATE_RANK = 16
GATE_LOGIT_NORMALIZER = 16.0
CHUNK = 64
SB_CHUNKS = 4
HEAD_WEIGHT_SLOTS = 4

LANES = 128
VMEM_LIMIT_BYTES = 58 * 1024 * 1024


def _rmsnorm(x, gain):
    ms = jnp.mean(x * x, axis=-1, keepdims=True)
    return x * lax.rsqrt(ms + EPS) * gain


def _silu(x):
    return x * (1.0 / (1.0 + jnp.exp(-x)))


def _log_sigmoid(z):
    return jnp.minimum(z, 0.0) - jnp.log1p(jnp.exp(-jnp.abs(z)))


def _dot(a, b):
    return jnp.dot(a, b, preferred_element_type=F32)


def _dot_nt(a, b):
    return lax.dot_general(a, b, (((1,), (1,)), ((), ())), preferred_element_type=F32)


def _dot_tn(a, b):
    return lax.dot_general(a, b, (((0,), (0,)), ((), ())), preferred_element_type=F32)


def _params(semantics):
    return pltpu.CompilerParams(dimension_semantics=semantics, vmem_limit_bytes=VMEM_LIMIT_BYTES)


def _ffn_step(j, nj, x_ref, gain_ref, weights, fgain_ref, o_ref, n_ref, final_norm):
    def slice_update(n):
        wg, wu, wo = weights()
        gate = _dot(n, wg)
        up = _dot(n, wu)
        act = (0.5 * _silu(gate) * up).astype(BF16)
        return _dot(act, wo)

    @pl.when(j == 0)
    def _first():
        x = x_ref[...]
        n = _rmsnorm(x, gain_ref[...]).astype(BF16)
        n_ref[...] = n
        o_ref[...] = x + slice_update(n)

    @pl.when(j > 0)
    def _rest():
        o_ref[...] += slice_update(n_ref[...])

    if final_norm:

        @pl.when(j == nj - 1)
        def _finish():
            o_ref[...] = _rmsnorm(o_ref[...], fgain_ref[...])


def _ffn_head_body(x_ref, gain_ref, w_in_hbm, w_out_hbm, fgain_ref,
                   o_ref, wg_bf_ref, wu_bf_ref, wo_bf_ref,
                   n_ref, wg_buf, wu_buf, wo_buf, sems, *, final_norm):
    j = pl.program_id(0)
    nj = pl.num_programs(0)
    n_slots, _, tf = wg_buf.shape
    d_ff = w_out_hbm.shape[0]
    lookahead = n_slots - 1

    def slice_copies(step, slot):
        lo = pl.multiple_of(step * tf, tf)
        return (
            pltpu.make_async_copy(w_in_hbm.at[:, pl.ds(lo, tf)], wg_buf.at[slot], sems.at[0, slot]),
            pltpu.make_async_copy(w_in_hbm.at[:, pl.ds(d_ff + lo, tf)], wu_buf.at[slot], sems.at[1, slot]),
            pltpu.make_async_copy(w_out_hbm.at[pl.ds(lo, tf), :], wo_buf.at[slot], sems.at[2, slot]),
        )

    @pl.when(j == 0)
    def _prime():
        for step in range(lookahead):
            for c in slice_copies(step, step):
                c.start()

    @pl.when(j + lookahead < nj)
    def _prefetch():
        for c in slice_copies(j + lookahead, lax.rem(j + lookahead, n_slots)):
            c.start()

    slot = lax.rem(j, n_slots)
    for c in slice_copies(j, slot):
        c.wait()

    def weights():
        wg = wg_buf[slot].astype(BF16)
        wu = wu_buf[slot].astype(BF16)
        wo = wo_buf[slot].astype(BF16)
        wg_bf_ref[...] = wg
        wu_bf_ref[...] = wu
        wo_bf_ref[...] = wo
        return wg, wu, wo

    _ffn_step(j, nj, x_ref, gain_ref, weights, fgain_ref, o_ref, n_ref, final_norm)


def _ffn_tail_body(x_ref, gain_ref, wg_ref, wu_ref, wo_ref, fgain_ref, head_ref,
                   o_ref, n_ref, *, final_norm, n_copy):
    i = pl.program_id(0)
    j = pl.program_id(1)
    rc = head_ref.shape[0]

    @pl.when((i == 0) & (j < n_copy))
    def _copy_head_rows():
        o_ref[pl.ds(pl.multiple_of(j * rc, rc), rc), :] = head_ref[...]

    @pl.when(i > 0)
    def _compute():
        _ffn_step(j, pl.num_programs(1), x_ref, gain_ref,
                  lambda: (wg_ref[...], wu_ref[...], wo_ref[...]), fgain_ref, o_ref, n_ref,
                  final_norm)


def _ffn(x, gain, w_in, w_out, final_gain, *, final_norm, tm=1024, tf_head=256, tf_tail=512,
         copy_rows=128):
    t, d = x.shape
    d_ff = w_out.shape[0]
    assert t % tm == 0 and t // tm >= 2 and w_in.shape == (d, 2 * d_ff)
    assert d_ff % tf_head == 0 and d_ff % tf_tail == 0
    nj = d_ff // tf_head
    vec = pl.BlockSpec((1, d), lambda j: (0, 0))
    h_head, wg_bf, wu_bf, wo_bf = pl.pallas_call(
        functools.partial(_ffn_head_body, final_norm=final_norm),
        grid=(nj,),
        in_specs=[
            pl.BlockSpec((tm, d), lambda j: (0, 0), pipeline_mode=pl.Buffered(1)),
            vec,
            pl.BlockSpec(memory_space=pl.ANY),
            pl.BlockSpec(memory_space=pl.ANY),
            vec,
        ],
        out_specs=[
            pl.BlockSpec((tm, d), lambda j: (0, 0)),
            pl.BlockSpec((d, tf_head), lambda j: (0, j)),
            pl.BlockSpec((d, tf_head), lambda j: (0, j)),
            pl.BlockSpec((tf_head, d), lambda j: (j, 0)),
        ],
        out_shape=[
            jax.ShapeDtypeStruct((tm, d), F32),
            jax.ShapeDtypeStruct((d, d_ff), BF16),
            jax.ShapeDtypeStruct((d, d_ff), BF16),
            jax.ShapeDtypeStruct((d_ff, d), BF16),
        ],
        scratch_shapes=[
            pltpu.VMEM((tm, d), BF16),
            pltpu.VMEM((HEAD_WEIGHT_SLOTS, d, tf_head), F32),
            pltpu.VMEM((HEAD_WEIGHT_SLOTS, d, tf_head), F32),
            pltpu.VMEM((HEAD_WEIGHT_SLOTS, tf_head, d), F32),
            pltpu.SemaphoreType.DMA((3, HEAD_WEIGHT_SLOTS)),
        ],
        compiler_params=_params(("arbitrary",)),
        name="ffn_head_final" if final_norm else "ffn_head",
    )(x, gain, w_in, w_out, final_gain)

    nj_tail = d_ff // tf_tail
    n_copy = tm // copy_rows
    assert tm % copy_rows == 0 and n_copy <= nj_tail
    vec2 = pl.BlockSpec((1, d), lambda i, j: (0, 0))
    wcol = lambda i, j: (0, j * jnp.minimum(i, 1))
    wrow = lambda i, j: (j * jnp.minimum(i, 1), 0)
    head_rows = lambda i, j: (jnp.where(i == 0, jnp.minimum(j, n_copy - 1), n_copy - 1), 0)
    return pl.pallas_call(
        functools.partial(_ffn_tail_body, final_norm=final_norm, n_copy=n_copy),
        grid=(t // tm, nj_tail),
        in_specs=[
            pl.BlockSpec((tm, d), lambda i, j: (jnp.maximum(i, 1), 0)),
            vec2,
            pl.BlockSpec((d, tf_tail), wcol),
            pl.BlockSpec((d, tf_tail), wcol),
            pl.BlockSpec((tf_tail, d), wrow),
            vec2,
            pl.BlockSpec((copy_rows, d), head_rows, pipeline_mode=pl.Buffered(1)),
        ],
        out_specs=pl.BlockSpec((tm, d), lambda i, j: (i, 0)),
        out_shape=jax.ShapeDtypeStruct((t, d), F32),
        scratch_shapes=[pltpu.VMEM((tm, d), BF16)],
        compiler_params=_params(("parallel", "arbitrary")),
        name="ffn_tail_final" if final_norm else "ffn_tail",
    )(x, gain, wg_bf, wu_bf, wo_bf, final_gain, h_head)


def _pool_group(x, halo, window, w, scale, t_first):
    tt = x.shape[0]
    ext = jnp.concatenate([halo, x], axis=0)
    row = lax.broadcasted_iota(jnp.int32, ext.shape, 0)
    acc = ext
    span = 1
    while span < window:
        acc = acc + jnp.where(row >= span, pltpu.roll(acc, span, axis=0), 0.0)
        span *= 2
    cnt = jnp.clip(row + (t_first + 1 - POOL_HALO), 1, window).astype(F32)
    pooled = (acc / cnt - ext)[POOL_HALO:, :].astype(BF16)
    return _dot(pooled, w) * scale


def _mixer_body(h_ref, gain_ref, w_ref, wgate_ref, wpool_ref, pscale_ref, wa_ref, ba_ref, gn_ref,
                y_ref,
                win_s, qd_s, ki_s, kt_s, bc_s, v_s, sg_s, st_s, halo_s,
                *, n_cast, nt, d_pool, dk, dv):
    s = pl.program_id(0)
    tt = h_ref.shape[0]
    wc = w_ref.shape[0]
    heads = GLA_HEADS
    group_dim = d_pool // N_POOL_GROUPS
    pool_chunks = d_pool // wc
    q_chunk = pool_chunks
    k_chunk = q_chunk + 1
    v_chunk0 = k_chunk + 1
    v_chunks = heads * dv // wc
    g_chunk0 = v_chunk0 + v_chunks

    @pl.when(s < n_cast)
    def _cast_weights():
        win_s[s] = w_ref[...].astype(BF16)

    @pl.when(s >= n_cast)
    def _compute():
        t = lax.rem(s - n_cast, nt)

        @pl.when(t == 0)
        def _reset():
            st_s[...] = jnp.zeros_like(st_s)
            halo_s[...] = jnp.zeros_like(halo_s)

        n = _rmsnorm(h_ref[...], gain_ref[...]).astype(BF16)

        per_chunk = wc // group_dim
        for c in range(pool_chunks):
            u_c = _dot_nt(n, win_s[c])
            for part in range(per_chunk):
                gi = c * per_chunk + part
                cols = slice(gi * group_dim, (gi + 1) * group_dim)
                x = u_c[:, part * group_dim:(part + 1) * group_dim]
                y = _pool_group(x, halo_s[:, cols], POOL_WINDOWS[gi], wpool_ref[gi].astype(BF16),
                                pscale_ref[:, cols], t * tt)
                halo_s[:, cols] = x[tt - POOL_HALO:, :]
                y_ref[:, cols] = y.astype(y_ref.dtype)

        gate = _dot_nt(n, wgate_ref[...])
        logits = _dot(gate.astype(BF16), wa_ref[...]) + ba_ref[...]
        log_alpha = _log_sigmoid(logits) * (1.0 / GATE_LOGIT_NORMALIZER)
        pos = lax.broadcasted_iota(jnp.int32, log_alpha.shape, 0) & (CHUNK - 1)
        bcum = log_alpha
        span = 1
        while span < CHUNK:
            bcum = bcum + jnp.where(pos >= span, pltpu.roll(bcum, span, axis=0), 0.0)
            span *= 2
        bc_s[...] = bcum
        n_chunks = tt // CHUNK
        b3 = bcum.reshape(n_chunks, CHUNK, heads * dk)
        b_last = jnp.broadcast_to(b3[:, CHUNK - 1:CHUNK, :], b3.shape).reshape(tt, heads * dk)
        q = _dot_nt(n, win_s[q_chunk])
        qd_s[...] = (q * (dk ** -0.5) * jnp.exp(bcum)).astype(BF16)
        k = _dot_nt(n, win_s[k_chunk])
        ki_s[...] = (k * jnp.exp(-bcum)).astype(BF16)
        kt_s[...] = (k * jnp.exp(b_last - bcum)).astype(BF16)
        for c in range(v_chunks):
            cols = slice(c * wc, (c + 1) * wc)
            v_s[:, cols] = _dot_nt(n, win_s[v_chunk0 + c]).astype(BF16)
            sg_s[:, cols] = _silu(_dot_nt(n, win_s[g_chunk0 + c]))

        sb_rows = SB_CHUNKS * CHUNK
        r_idx = lax.broadcasted_iota(jnp.int32, (sb_rows, sb_rows), 0)
        c_idx = lax.broadcasted_iota(jnp.int32, (sb_rows, sb_rows), 1)
        same_chunk_causal = ((r_idx // CHUNK) == (c_idx // CHUNK)) & (r_idx >= c_idx)

        def superblock(sb, carry):
            row0 = pl.multiple_of(sb * sb_rows, sb_rows)
            rows = pl.ds(row0, sb_rows)
            decays = [jnp.exp(bc_s[pl.ds(row0 + (c * CHUNK + CHUNK - 1), 1), :]) for c in range(SB_CHUNKS)]
            for hd in range(heads):
                kc = slice(hd * dk, (hd + 1) * dk)
                vc = slice(hd * dv, (hd + 1) * dv)
                qd = qd_s[rows, kc]
                kt = kt_s[rows, kc]
                v = v_s[rows, vc]
                scores = jnp.where(same_chunk_causal, _dot_nt(qd, ki_s[rows, kc]), 0.0).astype(BF16)
                o = _dot(scores, v)
                state_t = st_s[hd]
                inter = []
                for c in range(SB_CHUNKS):
                    cr = slice(c * CHUNK, (c + 1) * CHUNK)
                    inter.append(_dot_nt(qd[cr], state_t.astype(BF16)))
                    state_t = state_t * decays[c][:, kc] + _dot_tn(v[cr], kt[cr])
                st_s[hd] = state_t
                o = _rmsnorm(o + jnp.concatenate(inter, axis=0), gn_ref[...])
                y_ref[rows, d_pool + hd * dv:d_pool + (hd + 1) * dv] = (o * sg_s[rows, vc]).astype(y_ref.dtype)
            return carry

        lax.fori_loop(0, tt // sb_rows, superblock, 0)


def _mixer(h, gain, w_in_t, w_gate_t, w_pool, pool_scale, w_alpha, b_alpha, gla_norm, *,
           seq, d_pool, dk, dv, tt=512, wc=512):
    t, d = h.shape
    heads = GLA_HEADS
    d_gla = heads * dv
    n_main = d_pool + 2 * heads * dk + 2 * d_gla
    n_cast = n_main // wc
    nt = seq // tt
    n_tiles = t // tt
    group_dim = d_pool // N_POOL_GROUPS
    assert seq % tt == 0 and tt % (SB_CHUNKS * CHUNK) == 0 and n_main % wc == 0 and heads * dk == wc
    assert d_pool % wc == 0 and d_gla % wc == 0 and wc % group_dim == 0
    tile = lambda s: jnp.maximum(s - n_cast, 0)
    const = lambda s: (0, 0)
    return pl.pallas_call(
        functools.partial(_mixer_body, n_cast=n_cast, nt=nt, d_pool=d_pool, dk=dk, dv=dv),
        grid=(n_cast + n_tiles,),
        in_specs=[
            pl.BlockSpec((tt, d), lambda s: (tile(s), 0)),
            pl.BlockSpec((1, d), const),
            pl.BlockSpec((wc, d), lambda s: (jnp.minimum(s, n_cast - 1), 0)),
            pl.BlockSpec((LANES, d), const),
            pl.BlockSpec((N_POOL_GROUPS, group_dim, group_dim), lambda s: (0, 0, 0)),
            pl.BlockSpec((1, d_pool), const),
            pl.BlockSpec((LANES, heads * dk), const),
            pl.BlockSpec((1, heads * dk), const),
            pl.BlockSpec((1, dv), const),
        ],
        out_specs=pl.BlockSpec((tt, d_pool + d_gla), lambda s: (tile(s), 0)),
        out_shape=jax.ShapeDtypeStruct((t, d_pool + d_gla), BF16),
        scratch_shapes=[
            pltpu.VMEM((n_cast, wc, d), BF16),
            pltpu.VMEM((tt, heads * dk), BF16),
            pltpu.VMEM((tt, heads * dk), BF16),
            pltpu.VMEM((tt, heads * dk), BF16),
            pltpu.VMEM((tt, heads * dk), F32),
            pltpu.VMEM((tt, d_gla), BF16),
            pltpu.VMEM((tt, d_gla), F32),
            pltpu.VMEM((heads, dv, dk), F32),
            pltpu.VMEM((POOL_HALO, d_pool), F32),
        ],
        compiler_params=_params(("arbitrary",)),
        name="mixer",
    )(h, gain, w_in_t, w_gate_t, w_pool, pool_scale, w_alpha, b_alpha, gla_norm)


def _mix_out_body(h_ref, y_ref, w_ref, o_ref, w_bf_s):
    @pl.when(pl.program_id(0) == 0)
    def _cast_weights():
        w_bf_s[...] = w_ref[...].astype(BF16)

    o_ref[...] = h_ref[...] + _dot(y_ref[...], w_bf_s[...])


def _mix_out(h, y, w_out, *, tm=512):
    t, d = h.shape
    d_mix = y.shape[1]
    assert t % tm == 0 and w_out.shape == (d_mix, d)
    return pl.pallas_call(
        _mix_out_body,
        grid=(t // tm,),
        in_specs=[
            pl.BlockSpec((tm, d), lambda i: (i, 0)),
            pl.BlockSpec((tm, d_mix), lambda i: (i, 0)),
            pl.BlockSpec((d_mix, d), lambda i: (0, 0), pipeline_mode=pl.Buffered(1)),
        ],
        out_specs=pl.BlockSpec((tm, d), lambda i: (i, 0)),
        out_shape=jax.ShapeDtypeStruct((t, d), F32),
        scratch_shapes=[pltpu.VMEM((d_mix, d), BF16)],
        compiler_params=_params(("arbitrary",)),
        name="mix_out",
    )(h, y, w_out)


def kernel(x, ffn1_norm, ffn1_w_in, ffn1_w_out, mix_norm, w_in_mix, w_pool, pool_scale,
           w_alpha, b_alpha, gla_norm, w_out_mix, ffn2_norm, ffn2_w_in, ffn2_w_out, final_norm):
    b, s, d = x.shape
    depth = ffn1_norm.shape[0]
    d_pool = pool_scale.shape[1]
    dk_total = w_alpha.shape[2]
    dk = dk_total // GLA_HEADS
    dv = gla_norm.shape[1]
    d_gla = GLA_HEADS * dv
    n_main = d_pool + 2 * dk_total + 2 * d_gla
    assert w_in_mix.shape[2] == n_main + GLA_GATE_RANK and s % CHUNK == 0 and depth >= 1

    h = x.reshape(b * s, d)
    final_gain = final_norm.reshape(1, d)
    for l in range(depth):
        last = l == depth - 1
        h = _ffn(h, ffn1_norm[l].reshape(1, d), ffn1_w_in[l], ffn1_w_out[l], final_gain,
                 final_norm=False)

        w_mix_t = jnp.swapaxes(w_in_mix[l], 0, 1)
        w_gate_t = jnp.pad(w_mix_t[n_main:, :], ((0, LANES - GLA_GATE_RANK), (0, 0))).astype(BF16)
        w_alpha_pad = jnp.pad(w_alpha[l], ((0, LANES - GLA_GATE_RANK), (0, 0))).astype(BF16)
        y = _mixer(h, mix_norm[l].reshape(1, d), w_mix_t, w_gate_t, w_pool[l],
                   pool_scale[l].reshape(1, d_pool), w_alpha_pad, b_alpha[l].reshape(1, dk_total),
                   gla_norm[l].reshape(1, dv), seq=s, d_pool=d_pool, dk=dk, dv=dv)
        h = _mix_out(h, y, w_out_mix[l])

        h = _ffn(h, ffn2_norm[l].reshape(1, d), ffn2_w_in[l], ffn2_w_out[l], final_gain,
                 final_norm=last)
    return h.reshape(b, s, d)
```

```python
import functools

import jax
import jax.numpy as jnp
from jax import lax
from jax.experimental import pallas as pl
from jax.experimental.pallas import tpu as pltpu

F32 = jnp.float32
BF16 = jnp.bfloat16

EPS = 1e-6
POOL_WINDOWS = (2, 4, 8, 16)
N_POOL_GROUPS = len(POOL_WINDOWS)
POOL_HALO = 16
GLA_HEADS = 4
GLA_GATE_RANK = 16
GATE_LOGIT_NORMALIZER = 16.0
CHUNK = 64
SB_CHUNKS = 4
HEAD_WEIGHT_SLOTS = 3

LANES = 128
BF16_SUBLANES = 16
VMEM_LIMIT_BYTES = 58 * 1024 * 1024


def _rmsnorm(x, gain):
    ms = jnp.mean(x * x, axis=-1, keepdims=True)
    return x * lax.rsqrt(ms + EPS) * gain


def _silu(x):
    return x * (1.0 / (1.0 + jnp.exp(-x)))


def _log_sigmoid(z):
    return jnp.minimum(z, 0.0) - jnp.log1p(jnp.exp(-jnp.abs(z)))


def _dot(a, b):
    return jnp.dot(a, b, preferred_element_type=F32)


def _dot_nt(a, b):
    return lax.dot_general(a, b, (((1,), (1,)), ((), ())), preferred_element_type=F32)


def _dot_tn(a, b):
    return lax.dot_general(a, b, (((0,), (0,)), ((), ())), preferred_element_type=F32)


def _params(semantics):
    return pltpu.CompilerParams(dimension_semantics=semantics, vmem_limit_bytes=VMEM_LIMIT_BYTES)


def _ffn_step(j, nj, x_ref, gain_ref, weights, fgain_ref, o_ref, n_ref, final_norm):
    def slice_update(n):
        wg, wu, wo = weights()
        gate = _dot(n, wg)
        up = _dot(n, wu)
        act = (0.5 * _silu(gate) * up).astype(BF16)
        return _dot(act, wo)

    @pl.when(j == 0)
    def _first():
        x = x_ref[...]
        n = _rmsnorm(x, gain_ref[...]).astype(BF16)
        n_ref[...] = n
        o_ref[...] = x + slice_update(n)

    @pl.when(j > 0)
    def _rest():
        o_ref[...] += slice_update(n_ref[...])

    if final_norm:

        @pl.when(j == nj - 1)
        def _finish():
            o_ref[...] = _rmsnorm(o_ref[...], fgain_ref[...])


def _ffn_head_body(x_ref, gain_ref, w_in_hbm, w_out_hbm,
                   o_ref, wg_bf_ref, wu_bf_ref, wo_bf_ref,
                   n_ref, wg_buf, wu_buf, wo_buf, sems):
    j = pl.program_id(0)
    nj = pl.num_programs(0)
    n_slots, _, tf = wg_buf.shape
    d_ff = w_out_hbm.shape[0]
    lookahead = n_slots - 1

    def slice_copies(step, slot):
        lo = pl.multiple_of(step * tf, tf)
        return (
            pltpu.make_async_copy(w_in_hbm.at[:, pl.ds(lo, tf)], wg_buf.at[slot], sems.at[0, slot]),
            pltpu.make_async_copy(w_in_hbm.at[:, pl.ds(d_ff + lo, tf)], wu_buf.at[slot], sems.at[1, slot]),
            pltpu.make_async_copy(w_out_hbm.at[pl.ds(lo, tf), :], wo_buf.at[slot], sems.at[2, slot]),
        )

    @pl.when(j == 0)
    def _prime():
        for step in range(lookahead):
            for c in slice_copies(step, step):
                c.start()

    @pl.when(j + lookahead < nj)
    def _prefetch():
        for c in slice_copies(j + lookahead, lax.rem(j + lookahead, n_slots)):
            c.start()

    slot = lax.rem(j, n_slots)
    for c in slice_copies(j, slot):
        c.wait()

    def weights():
        wg = wg_buf[slot].astype(BF16)
        wu = wu_buf[slot].astype(BF16)
        wo = wo_buf[slot].astype(BF16)
        wg_bf_ref[...] = wg
        wu_bf_ref[...] = wu
        wo_bf_ref[...] = wo
        return wg, wu, wo

    _ffn_step(j, nj, x_ref, gain_ref, weights, None, o_ref, n_ref, False)


def _ffn_tail_body(x_hbm, gain_ref, wg_ref, wu_ref, wo_ref, head_ref, nw_in_ref, nw_out_ref,
                   o_ref, nw_in_bf_ref, nw_out_bf_ref,
                   n_ref, x_buf, x_sem, *, n_copy):
    i = pl.program_id(0)
    j = pl.program_id(1)
    n_tiles = pl.num_programs(0)
    tm = x_buf.shape[0]
    rc = head_ref.shape[0]

    nw_in_bf_ref[...] = nw_in_ref[...].astype(BF16)
    nw_out_bf_ref[...] = nw_out_ref[...].astype(BF16)

    def x_copy(tile):
        rows = pl.ds(pl.multiple_of(tile * tm, tm), tm)
        return pltpu.make_async_copy(x_hbm.at[rows, :], x_buf, x_sem)

    @pl.when((j == 1) & (i + 1 < n_tiles))
    def _prefetch_x():
        x_copy(i + 1).start()

    @pl.when((i == 0) & (j < n_copy))
    def _copy_head_rows():
        o_ref[pl.ds(pl.multiple_of(j * rc, rc), rc), :] = head_ref[...]

    @pl.when((i > 0) & (j == 0))
    def _wait_x():
        x_copy(i).wait()

    @pl.when(i > 0)
    def _compute():
        _ffn_step(j, pl.num_programs(1), x_buf, gain_ref,
                  lambda: (wg_ref[...], wu_ref[...], wo_ref[...]), None, o_ref, n_ref, False)


def _ffn_bf16_body(x_ref, gain_ref, wg_ref, wu_ref, wo_ref, fgain_ref, o_ref, n_ref, *, final_norm):
    _ffn_step(pl.program_id(1), pl.num_programs(1), x_ref, gain_ref,
              lambda: (wg_ref[...], wu_ref[...], wo_ref[...]), fgain_ref, o_ref, n_ref, final_norm)


def _cast_row_block(rows, steps):
    for rb in range(BF16_SUBLANES, rows + 1, BF16_SUBLANES):
        if rows % rb == 0 and rows // rb <= steps:
            return rb
    raise ValueError(f"cannot cast {rows} rows in {steps} steps")


def _ffn_first(x, gain, w_in, w_out, next_w_in, next_w_out, *, tm=1024, tf_head=256, tf_tail=512,
               copy_rows=128):
    t, d = x.shape
    d_ff = w_out.shape[0]
    assert t % tm == 0 and t // tm >= 2 and w_in.shape == (d, 2 * d_ff)
    assert d_ff % tf_head == 0 and d_ff % tf_tail == 0 and d_ff // tf_tail >= 2
    nj = d_ff // tf_head
    vec = pl.BlockSpec((1, d), lambda j: (0, 0))
    h_head, wg_bf, wu_bf, wo_bf = pl.pallas_call(
        _ffn_head_body,
        grid=(nj,),
        in_specs=[
            pl.BlockSpec((tm, d), lambda j: (0, 0), pipeline_mode=pl.Buffered(1)),
            vec,
            pl.BlockSpec(memory_space=pl.ANY),
            pl.BlockSpec(memory_space=pl.ANY),
        ],
        out_specs=[
            pl.BlockSpec((tm, d), lambda j: (0, 0)),
            pl.BlockSpec((d, tf_head), lambda j: (0, j)),
            pl.BlockSpec((d, tf_head), lambda j: (0, j)),
            pl.BlockSpec((tf_head, d), lambda j: (j, 0)),
        ],
        out_shape=[
            jax.ShapeDtypeStruct((tm, d), F32),
            jax.ShapeDtypeStruct((d, d_ff), BF16),
            jax.ShapeDtypeStruct((d, d_ff), BF16),
            jax.ShapeDtypeStruct((d_ff, d), BF16),
        ],
        scratch_shapes=[
            pltpu.VMEM((tm, d), BF16),
            pltpu.VMEM((HEAD_WEIGHT_SLOTS, d, tf_head), F32),
            pltpu.VMEM((HEAD_WEIGHT_SLOTS, d, tf_head), F32),
            pltpu.VMEM((HEAD_WEIGHT_SLOTS, tf_head, d), F32),
            pltpu.SemaphoreType.DMA((3, HEAD_WEIGHT_SLOTS)),
        ],
        compiler_params=_params(("arbitrary",)),
        name="ffn_head",
    )(x, gain, w_in, w_out)

    n_tiles = t // tm
    nj_tail = d_ff // tf_tail
    n_copy = tm // copy_rows
    assert tm % copy_rows == 0 and n_copy <= nj_tail
    steps = n_tiles * nj_tail
    nd, nd_ff2 = next_w_in.shape
    assert next_w_out.shape == (nd_ff2 // 2, nd)
    rb_in = _cast_row_block(nd, steps)
    rb_out = _cast_row_block(nd_ff2 // 2, steps)
    vec2 = pl.BlockSpec((1, d), lambda i, j: (0, 0))
    wcol = lambda i, j: (0, j * jnp.minimum(i, 1))
    wrow = lambda i, j: (j * jnp.minimum(i, 1), 0)
    head_rows = lambda i, j: (jnp.where(i == 0, jnp.minimum(j, n_copy - 1), n_copy - 1), 0)
    cast_in = lambda i, j: (jnp.minimum(i * nj_tail + j, nd // rb_in - 1), 0)
    cast_out = lambda i, j: (jnp.minimum(i * nj_tail + j, nd_ff2 // 2 // rb_out - 1), 0)
    return pl.pallas_call(
        functools.partial(_ffn_tail_body, n_copy=n_copy),
        grid=(n_tiles, nj_tail),
        in_specs=[
            pl.BlockSpec(memory_space=pl.ANY),
            vec2,
            pl.BlockSpec((d, tf_tail), wcol),
            pl.BlockSpec((d, tf_tail), wcol),
            pl.BlockSpec((tf_tail, d), wrow),
            pl.BlockSpec((copy_rows, d), head_rows, pipeline_mode=pl.Buffered(1)),
            pl.BlockSpec((rb_in, nd_ff2), cast_in),
            pl.BlockSpec((rb_out, nd), cast_out),
        ],
        out_specs=[
            pl.BlockSpec((tm, d), lambda i, j: (i, 0)),
            pl.BlockSpec((rb_in, nd_ff2), cast_in),
            pl.BlockSpec((rb_out, nd), cast_out),
        ],
        out_shape=[
            jax.ShapeDtypeStruct((t, d), F32),
            jax.ShapeDtypeStruct((nd, nd_ff2), BF16),
            jax.ShapeDtypeStruct((nd_ff2 // 2, nd), BF16),
        ],
        scratch_shapes=[
            pltpu.VMEM((tm, d), BF16),
            pltpu.VMEM((tm, d), F32),
            pltpu.SemaphoreType.DMA(()),
        ],
        compiler_params=_params(("arbitrary", "arbitrary")),
        name="ffn_tail",
    )(x, gain, wg_bf, wu_bf, wo_bf, h_head, next_w_in, next_w_out)


def _ffn_second(x, gain, w_in_bf, w_out_bf, final_gain, *, final_norm, tm=1024, tf=512):
    t, d = x.shape
    d_ff = w_out_bf.shape[0]
    nj = d_ff // tf
    assert t % tm == 0 and d_ff % tf == 0 and w_in_bf.shape == (d, 2 * d_ff)
    vec = pl.BlockSpec((1, d), lambda i, j: (0, 0))
    return pl.pallas_call(
        functools.partial(_ffn_bf16_body, final_norm=final_norm),
        grid=(t // tm, nj),
        in_specs=[
            pl.BlockSpec((tm, d), lambda i, j: (i, 0)),
            vec,
            pl.BlockSpec((d, tf), lambda i, j: (0, j)),
            pl.BlockSpec((d, tf), lambda i, j: (0, j + nj)),
            pl.BlockSpec((tf, d), lambda i, j: (j, 0)),
            vec,
        ],
        out_specs=pl.BlockSpec((tm, d), lambda i, j: (i, 0)),
        out_shape=jax.ShapeDtypeStruct((t, d), F32),
        scratch_shapes=[pltpu.VMEM((tm, d), BF16)],
        compiler_params=_params(("parallel", "arbitrary")),
        name="ffn_second_final" if final_norm else "ffn_second",
    )(x, gain, w_in_bf, w_in_bf, w_out_bf, final_gain)


def _pool_group(x, halo, window, w, scale, t_first):
    tt = x.shape[0]
    ext = jnp.concatenate([halo, x], axis=0)
    row = lax.broadcasted_iota(jnp.int32, ext.shape, 0)
    acc = ext
    span = 1
    while span < window:
        acc = acc + jnp.where(row >= span, pltpu.roll(acc, span, axis=0), 0.0)
        span *= 2
    cnt = jnp.clip(row + (t_first + 1 - POOL_HALO), 1, window).astype(F32)
    pooled = (acc / cnt - ext)[POOL_HALO:, :].astype(BF16)
    return _dot(pooled, w) * scale


def _mixer_body(h_ref, gain_ref, w_ref, wgate_ref, wpool_ref, pscale_ref, wa_ref, ba_ref, gn_ref,
                y_ref,
                win_s, qd_s, ki_s, kt_s, bc_s, v_s, sg_s, st_s, halo_s,
                *, n_cast, nt, d_pool, dk, dv):
    s = pl.program_id(0)
    tt = h_ref.shape[0]
    wc = w_ref.shape[0]
    heads = GLA_HEADS
    group_dim = d_pool // N_POOL_GROUPS
    pool_chunks = d_pool // wc
    q_chunk = pool_chunks
    k_chunk = q_chunk + 1
    v_chunk0 = k_chunk + 1
    v_chunks = heads * dv // wc
    g_chunk0 = v_chunk0 + v_chunks

    @pl.when(s < n_cast)
    def _cast_weights():
        win_s[s] = w_ref[...].astype(BF16)

    @pl.when(s >= n_cast)
    def _compute():
        t = lax.rem(s - n_cast, nt)

        @pl.when(t == 0)
        def _reset():
            st_s[...] = jnp.zeros_like(st_s)
            halo_s[...] = jnp.zeros_like(halo_s)

        n = _rmsnorm(h_ref[...], gain_ref[...]).astype(BF16)

        per_chunk = wc // group_dim
        for c in range(pool_chunks):
            u_c = _dot_nt(n, win_s[c])
            for part in range(per_chunk):
                gi = c * per_chunk + part
                cols = slice(gi * group_dim, (gi + 1) * group_dim)
                x = u_c[:, part * group_dim:(part + 1) * group_dim]
                y = _pool_group(x, halo_s[:, cols], POOL_WINDOWS[gi], wpool_ref[gi].astype(BF16),
                                pscale_ref[:, cols], t * tt)
                halo_s[:, cols] = x[tt - POOL_HALO:, :]
                y_ref[:, cols] = y.astype(y_ref.dtype)

        gate = _dot_nt(n, wgate_ref[...])
        logits = _dot(gate.astype(BF16), wa_ref[...]) + ba_ref[...]
        log_alpha = _log_sigmoid(logits) * (1.0 / GATE_LOGIT_NORMALIZER)
        pos = lax.broadcasted_iota(jnp.int32, log_alpha.shape, 0) & (CHUNK - 1)
        bcum = log_alpha
        span = 1
        while span < CHUNK:
            bcum = bcum + jnp.where(pos >= span, pltpu.roll(bcum, span, axis=0), 0.0)
            span *= 2
        bc_s[...] = bcum
        n_chunks = tt // CHUNK
        b3 = bcum.reshape(n_chunks, CHUNK, heads * dk)
        b_last = jnp.broadcast_to(b3[:, CHUNK - 1:CHUNK, :], b3.shape).reshape(tt, heads * dk)
        q = _dot_nt(n, win_s[q_chunk])
        qd_s[...] = (q * (dk ** -0.5) * jnp.exp(bcum)).astype(BF16)
        k = _dot_nt(n, win_s[k_chunk])
        ki_s[...] = (k * jnp.exp(-bcum)).astype(BF16)
        kt_s[...] = (k * jnp.exp(b_last - bcum)).astype(BF16)
        for c in range(v_chunks):
            cols = slice(c * wc, (c + 1) * wc)
            v_s[:, cols] = _dot_nt(n, win_s[v_chunk0 + c]).astype(BF16)
            sg_s[:, cols] = _silu(_dot_nt(n, win_s[g_chunk0 + c]))

        sb_rows = SB_CHUNKS * CHUNK
        r_idx = lax.broadcasted_iota(jnp.int32, (sb_rows, sb_rows), 0)
        c_idx = lax.broadcasted_iota(jnp.int32, (sb_rows, sb_rows), 1)
        same_chunk_causal = ((r_idx // CHUNK) == (c_idx // CHUNK)) & (r_idx >= c_idx)

        def superblock(sb, carry):
            row0 = pl.multiple_of(sb * sb_rows, sb_rows)
            rows = pl.ds(row0, sb_rows)
            decays = [jnp.exp(bc_s[pl.ds(row0 + (c * CHUNK + CHUNK - 1), 1), :]) for c in range(SB_CHUNKS)]
            for hd in range(heads):
                kc = slice(hd * dk, (hd + 1) * dk)
                vc = slice(hd * dv, (hd + 1) * dv)
                qd = qd_s[rows, kc]
                kt = kt_s[rows, kc]
                v = v_s[rows, vc]
                scores = jnp.where(same_chunk_causal, _dot_nt(qd, ki_s[rows, kc]), 0.0).astype(BF16)
                o = _dot(scores, v)
                state_t = st_s[hd]
                inter = []
                for c in range(SB_CHUNKS):
                    cr = slice(c * CHUNK, (c + 1) * CHUNK)
                    inter.append(_dot_nt(qd[cr], state_t.astype(BF16)))
                    state_t = state_t * decays[c][:, kc] + _dot_tn(v[cr], kt[cr])
                st_s[hd] = state_t
                o = _rmsnorm(o + jnp.concatenate(inter, axis=0), gn_ref[...])
                y_ref[rows, d_pool + hd * dv:d_pool + (hd + 1) * dv] = (o * sg_s[rows, vc]).astype(y_ref.dtype)
            return carry

        lax.fori_loop(0, tt // sb_rows, superblock, 0)


def _mixer(h, gain, w_in_t, w_gate_t, w_pool, pool_scale, w_alpha, b_alpha, gla_norm, *,
           seq, d_pool, dk, dv, tt=512, wc=512):
    t, d = h.shape
    heads = GLA_HEADS
    d_gla = heads * dv
    n_main = d_pool + 2 * heads * dk + 2 * d_gla
    n_cast = n_main // wc
    nt = seq // tt
    n_tiles = t // tt
    group_dim = d_pool // N_POOL_GROUPS
    assert seq % tt == 0 and tt % (SB_CHUNKS * CHUNK) == 0 and n_main % wc == 0 and heads * dk == wc
    assert d_pool % wc == 0 and d_gla % wc == 0 and wc % group_dim == 0
    tile = lambda s: jnp.maximum(s - n_cast, 0)
    const = lambda s: (0, 0)
    return pl.pallas_call(
        functools.partial(_mixer_body, n_cast=n_cast, nt=nt, d_pool=d_pool, dk=dk, dv=dv),
        grid=(n_cast + n_tiles,),
        in_specs=[
            pl.BlockSpec((tt, d), lambda s: (tile(s), 0)),
            pl.BlockSpec((1, d), const),
            pl.BlockSpec((wc, d), lambda s: (jnp.minimum(s, n_cast - 1), 0)),
            pl.BlockSpec((LANES, d), const),
            pl.BlockSpec((N_POOL_GROUPS, group_dim, group_dim), lambda s: (0, 0, 0)),
            pl.BlockSpec((1, d_pool), const),
            pl.BlockSpec((LANES, heads * dk), const),
            pl.BlockSpec((1, heads * dk), const),
            pl.BlockSpec((1, dv), const),
        ],
        out_specs=pl.BlockSpec((tt, d_pool + d_gla), lambda s: (tile(s), 0)),
        out_shape=jax.ShapeDtypeStruct((t, d_pool + d_gla), BF16),
        scratch_shapes=[
            pltpu.VMEM((n_cast, wc, d), BF16),
            pltpu.VMEM((tt, heads * dk), BF16),
            pltpu.VMEM((tt, heads * dk), BF16),
            pltpu.VMEM((tt, heads * dk), BF16),
            pltpu.VMEM((tt, heads * dk), F32),
            pltpu.VMEM((tt, d_gla), BF16),
            pltpu.VMEM((tt, d_gla), F32),
            pltpu.VMEM((heads, dv, dk), F32),
            pltpu.VMEM((POOL_HALO, d_pool), F32),
        ],
        compiler_params=_params(("arbitrary",)),
        name="mixer",
    )(h, gain, w_in_t, w_gate_t, w_pool, pool_scale, w_alpha, b_alpha, gla_norm)


def _mix_out_body(h_ref, y_ref, w_ref, o_ref, w_bf_s):
    @pl.when(pl.program_id(0) == 0)
    def _cast_weights():
        w_bf_s[...] = w_ref[...].astype(BF16)

    o_ref[...] = h_ref[...] + _dot(y_ref[...], w_bf_s[...])


def _mix_out(h, y, w_out, *, tm=512):
    t, d = h.shape
    d_mix = y.shape[1]
    assert t % tm == 0 and w_out.shape == (d_mix, d)
    return pl.pallas_call(
        _mix_out_body,
        grid=(t // tm,),
        in_specs=[
            pl.BlockSpec((tm, d), lambda i: (i, 0)),
            pl.BlockSpec((tm, d_mix), lambda i: (i, 0)),
            pl.BlockSpec((d_mix, d), lambda i: (0, 0), pipeline_mode=pl.Buffered(1)),
        ],
        out_specs=pl.BlockSpec((tm, d), lambda i: (i, 0)),
        out_shape=jax.ShapeDtypeStruct((t, d), F32),
        scratch_shapes=[pltpu.VMEM((d_mix, d), BF16)],
        compiler_params=_params(("arbitrary",)),
        name="mix_out",
    )(h, y, w_out)


def kernel(x, ffn1_norm, ffn1_w_in, ffn1_w_out, mix_norm, w_in_mix, w_pool, pool_scale,
           w_alpha, b_alpha, gla_norm, w_out_mix, ffn2_norm, ffn2_w_in, ffn2_w_out, final_norm):
    b, s, d = x.shape
    depth = ffn1_norm.shape[0]
    d_pool = pool_scale.shape[1]
    dk_total = w_alpha.shape[2]
    dk = dk_total // GLA_HEADS
    dv = gla_norm.shape[1]
    d_gla = GLA_HEADS * dv
    n_main = d_pool + 2 * dk_total + 2 * d_gla
    assert w_in_mix.shape[2] == n_main + GLA_GATE_RANK and s % CHUNK == 0 and depth >= 1

    h = x.reshape(b * s, d)
    final_gain = final_norm.reshape(1, d)
    for l in range(depth):
        last = l == depth - 1
        h, w2_in_bf, w2_out_bf = _ffn_first(h, ffn1_norm[l].reshape(1, d), ffn1_w_in[l], ffn1_w_out[l],
                                            ffn2_w_in[l], ffn2_w_out[l])

        w_mix_t = jnp.swapaxes(w_in_mix[l], 0, 1)
        w_gate_t = jnp.pad(w_mix_t[n_main:, :], ((0, LANES - GLA_GATE_RANK), (0, 0))).astype(BF16)
        w_alpha_pad = jnp.pad(w_alpha[l], ((0, LANES - GLA_GATE_RANK), (0, 0))).astype(BF16)
        y = _mixer(h, mix_norm[l].reshape(1, d), w_mix_t, w_gate_t, w_pool[l],
                   pool_scale[l].reshape(1, d_pool), w_alpha_pad, b_alpha[l].reshape(1, dk_total),
                   gla_norm[l].reshape(1, dv), seq=s, d_pool=d_pool, dk=dk, dv=dv)
        h = _mix_out(h, y, w_out_mix[l])

        h = _ffn_second(h, ffn2_norm[l].reshape(1, d), w2_in_bf, w2_out_bf, final_gain,
                        final_norm=last)
    return h.reshape(b, s, d)
```

```python
import functools

import jax
import jax.numpy as jnp
from jax import lax
from jax.experimental import pallas as pl
from jax.experimental.pallas import tpu as pltpu

F32 = jnp.float32
BF16 = jnp.bfloat16

EPS = 1e-6
POOL_WINDOWS = (2, 4, 8, 16)
N_POOL_GROUPS = len(POOL_WINDOWS)
POOL_HALO = 16
GLA_HEADS = 4
GLA_GATE_RANK = 16
GATE_LOGIT_NORMALIZER = 16.0
CHUNK = 64
SB_CHUNKS = 4
HEAD_WEIGHT_SLOTS = 3

LANES = 128
BF16_SUBLANES = 16
VMEM_LIMIT_BYTES = 58 * 1024 * 1024


def _rmsnorm(x, gain):
    ms = jnp.mean(x * x, axis=-1, keepdims=True)
    return x * lax.rsqrt(ms + EPS) * gain


def _silu(x):
    return x * (1.0 / (1.0 + jnp.exp(-x)))


def _log_sigmoid(z):
    return jnp.minimum(z, 0.0) - jnp.log1p(jnp.exp(-jnp.abs(z)))


def _dot(a, b):
    return jnp.dot(a, b, preferred_element_type=F32)


def _dot_nt(a, b):
    return lax.dot_general(a, b, (((1,), (1,)), ((), ())), preferred_element_type=F32)


def _dot_tn(a, b):
    return lax.dot_general(a, b, (((0,), (0,)), ((), ())), preferred_element_type=F32)


def _params(semantics):
    return pltpu.CompilerParams(dimension_semantics=semantics, vmem_limit_bytes=VMEM_LIMIT_BYTES)


def _ffn_step(j, nj, x_ref, gain_ref, weights, fgain_ref, o_ref, n_ref, final_norm):
    def slice_update(n):
        wg, wu, wo = weights()
        gate = _dot(n, wg)
        up = _dot(n, wu)
        act = (0.5 * _silu(gate) * up).astype(BF16)
        return _dot(act, wo)

    @pl.when(j == 0)
    def _first():
        x = x_ref[...]
        n = _rmsnorm(x, gain_ref[...]).astype(BF16)
        n_ref[...] = n
        o_ref[...] = x + slice_update(n)

    @pl.when(j > 0)
    def _rest():
        o_ref[...] += slice_update(n_ref[...])

    if final_norm:

        @pl.when(j == nj - 1)
        def _finish():
            o_ref[...] = _rmsnorm(o_ref[...], fgain_ref[...])


def _ffn_head_body(x_ref, gain_ref, w_in_hbm, w_out_hbm,
                   o_ref, wg_bf_ref, wu_bf_ref, wo_bf_ref,
                   n_ref, wg_buf, wu_buf, wo_buf, sems):
    j = pl.program_id(0)
    nj = pl.num_programs(0)
    n_slots, _, tf = wg_buf.shape
    d_ff = w_out_hbm.shape[0]
    lookahead = n_slots - 1

    def slice_copies(step, slot):
        lo = pl.multiple_of(step * tf, tf)
        return (
            pltpu.make_async_copy(w_in_hbm.at[:, pl.ds(lo, tf)], wg_buf.at[slot], sems.at[0, slot]),
            pltpu.make_async_copy(w_in_hbm.at[:, pl.ds(d_ff + lo, tf)], wu_buf.at[slot], sems.at[1, slot]),
            pltpu.make_async_copy(w_out_hbm.at[pl.ds(lo, tf), :], wo_buf.at[slot], sems.at[2, slot]),
        )

    @pl.when(j == 0)
    def _prime():
        for step in range(lookahead):
            for c in slice_copies(step, step):
                c.start()

    @pl.when(j + lookahead < nj)
    def _prefetch():
        for c in slice_copies(j + lookahead, lax.rem(j + lookahead, n_slots)):
            c.start()

    slot = lax.rem(j, n_slots)
    for c in slice_copies(j, slot):
        c.wait()

    def weights():
        wg = wg_buf[slot].astype(BF16)
        wu = wu_buf[slot].astype(BF16)
        wo = wo_buf[slot].astype(BF16)
        wg_bf_ref[...] = wg
        wu_bf_ref[...] = wu
        wo_bf_ref[...] = wo
        return wg, wu, wo

    _ffn_step(j, nj, x_ref, gain_ref, weights, None, o_ref, n_ref, False)


def _ffn_tail_body(x_hbm, gain_ref, wg_ref, wu_ref, wo_ref, head_ref, nw_in_ref, nw_out_ref,
                   o_ref, nw_in_bf_ref, nw_out_bf_ref,
                   n_ref, x_buf, x_sem, *, n_copy):
    i = pl.program_id(0)
    j = pl.program_id(1)
    n_tiles = pl.num_programs(0)
    tm = x_buf.shape[0]
    rc = head_ref.shape[0]

    def cast_next_weights():
        nw_in_bf_ref[...] = nw_in_ref[...].astype(BF16)
        nw_out_bf_ref[...] = nw_out_ref[...].astype(BF16)

    def x_copy(tile):
        rows = pl.ds(pl.multiple_of(tile * tm, tm), tm)
        return pltpu.make_async_copy(x_hbm.at[rows, :], x_buf, x_sem)

    @pl.when((j == 1) & (i + 1 < n_tiles))
    def _prefetch_x():
        x_copy(i + 1).start()

    @pl.when(i == 0)
    def _head_tile():
        cast_next_weights()

        @pl.when(j < n_copy)
        def _copy_head_rows():
            o_ref[pl.ds(pl.multiple_of(j * rc, rc), rc), :] = head_ref[...]

    @pl.when((i > 0) & (j == 0))
    def _wait_x():
        x_copy(i).wait()

    def weights():
        cast_next_weights()
        return wg_ref[...], wu_ref[...], wo_ref[...]

    @pl.when(i > 0)
    def _compute():
        _ffn_step(j, pl.num_programs(1), x_buf, gain_ref, weights, None, o_ref, n_ref, False)


def _ffn_bf16_body(x_ref, gain_ref, wg_ref, wu_ref, wo_ref, fgain_ref, o_ref, n_ref, *, final_norm):
    _ffn_step(pl.program_id(1), pl.num_programs(1), x_ref, gain_ref,
              lambda: (wg_ref[...], wu_ref[...], wo_ref[...]), fgain_ref, o_ref, n_ref, final_norm)


def _cast_row_block(rows, steps):
    for rb in range(BF16_SUBLANES, rows + 1, BF16_SUBLANES):
        if rows % rb == 0 and rows // rb <= steps:
            return rb
    raise ValueError(f"cannot cast {rows} rows in {steps} steps")


def _ffn_first(x, gain, w_in, w_out, next_w_in, next_w_out, *, tm=1024, tf_head=256, tf_tail=512,
               copy_rows=128):
    t, d = x.shape
    d_ff = w_out.shape[0]
    assert t % tm == 0 and t // tm >= 2 and w_in.shape == (d, 2 * d_ff)
    assert d_ff % tf_head == 0 and d_ff % tf_tail == 0 and d_ff // tf_tail >= 2
    nj = d_ff // tf_head
    vec = pl.BlockSpec((1, d), lambda j: (0, 0))
    h_head, wg_bf, wu_bf, wo_bf = pl.pallas_call(
        _ffn_head_body,
        grid=(nj,),
        in_specs=[
            pl.BlockSpec((tm, d), lambda j: (0, 0), pipeline_mode=pl.Buffered(1)),
            vec,
            pl.BlockSpec(memory_space=pl.ANY),
            pl.BlockSpec(memory_space=pl.ANY),
        ],
        out_specs=[
            pl.BlockSpec((tm, d), lambda j: (0, 0)),
            pl.BlockSpec((d, tf_head), lambda j: (0, j)),
            pl.BlockSpec((d, tf_head), lambda j: (0, j)),
            pl.BlockSpec((tf_head, d), lambda j: (j, 0)),
        ],
        out_shape=[
            jax.ShapeDtypeStruct((tm, d), F32),
            jax.ShapeDtypeStruct((d, d_ff), BF16),
            jax.ShapeDtypeStruct((d, d_ff), BF16),
            jax.ShapeDtypeStruct((d_ff, d), BF16),
        ],
        scratch_shapes=[
            pltpu.VMEM((tm, d), BF16),
            pltpu.VMEM((HEAD_WEIGHT_SLOTS, d, tf_head), F32),
            pltpu.VMEM((HEAD_WEIGHT_SLOTS, d, tf_head), F32),
            pltpu.VMEM((HEAD_WEIGHT_SLOTS, tf_head, d), F32),
            pltpu.SemaphoreType.DMA((3, HEAD_WEIGHT_SLOTS)),
        ],
        compiler_params=_params(("arbitrary",)),
        name="ffn_head",
    )(x, gain, w_in, w_out)

    n_tiles = t // tm
    nj_tail = d_ff // tf_tail
    n_copy = tm // copy_rows
    assert tm % copy_rows == 0 and n_copy <= nj_tail
    steps = n_tiles * nj_tail
    nd, nd_ff2 = next_w_in.shape
    assert next_w_out.shape == (nd_ff2 // 2, nd)
    rb_in = _cast_row_block(nd, steps)
    rb_out = _cast_row_block(nd_ff2 // 2, steps)
    vec2 = pl.BlockSpec((1, d), lambda i, j: (0, 0))
    wcol = lambda i, j: (0, j * jnp.minimum(i, 1))
    wrow = lambda i, j: (j * jnp.minimum(i, 1), 0)
    head_rows = lambda i, j: (jnp.where(i == 0, jnp.minimum(j, n_copy - 1), n_copy - 1), 0)
    cast_in = lambda i, j: (jnp.minimum(i * nj_tail + j, nd // rb_in - 1), 0)
    cast_out = lambda i, j: (jnp.minimum(i * nj_tail + j, nd_ff2 // 2 // rb_out - 1), 0)
    return pl.pallas_call(
        functools.partial(_ffn_tail_body, n_copy=n_copy),
        grid=(n_tiles, nj_tail),
        in_specs=[
            pl.BlockSpec(memory_space=pl.ANY),
            vec2,
            pl.BlockSpec((d, tf_tail), wcol),
            pl.BlockSpec((d, tf_tail), wcol),
            pl.BlockSpec((tf_tail, d), wrow),
            pl.BlockSpec((copy_rows, d), head_rows, pipeline_mode=pl.Buffered(1)),
            pl.BlockSpec((rb_in, nd_ff2), cast_in),
            pl.BlockSpec((rb_out, nd), cast_out),
        ],
        out_specs=[
            pl.BlockSpec((tm, d), lambda i, j: (i, 0)),
            pl.BlockSpec((rb_in, nd_ff2), cast_in),
            pl.BlockSpec((rb_out, nd), cast_out),
        ],
        out_shape=[
            jax.ShapeDtypeStruct((t, d), F32),
            jax.ShapeDtypeStruct((nd, nd_ff2), BF16),
            jax.ShapeDtypeStruct((nd_ff2 // 2, nd), BF16),
        ],
        scratch_shapes=[
            pltpu.VMEM((tm, d), BF16),
            pltpu.VMEM((tm, d), F32),
            pltpu.SemaphoreType.DMA(()),
        ],
        compiler_params=_params(("arbitrary", "arbitrary")),
        name="ffn_tail",
    )(x, gain, wg_bf, wu_bf, wo_bf, h_head, next_w_in, next_w_out)


def _ffn_second(x, gain, w_in_bf, w_out_bf, final_gain, *, final_norm, tm=1024, tf=512):
    t, d = x.shape
    d_ff = w_out_bf.shape[0]
    nj = d_ff // tf
    assert t % tm == 0 and d_ff % tf == 0 and w_in_bf.shape == (d, 2 * d_ff)
    vec = pl.BlockSpec((1, d), lambda i, j: (0, 0))
    return pl.pallas_call(
        functools.partial(_ffn_bf16_body, final_norm=final_norm),
        grid=(t // tm, nj),
        in_specs=[
            pl.BlockSpec((tm, d), lambda i, j: (i, 0)),
            vec,
            pl.BlockSpec((d, tf), lambda i, j: (0, j)),
            pl.BlockSpec((d, tf), lambda i, j: (0, j + nj)),
            pl.BlockSpec((tf, d), lambda i, j: (j, 0)),
            vec,
        ],
        out_specs=pl.BlockSpec((tm, d), lambda i, j: (i, 0)),
        out_shape=jax.ShapeDtypeStruct((t, d), F32),
        scratch_shapes=[pltpu.VMEM((tm, d), BF16)],
        compiler_params=_params(("parallel", "arbitrary")),
        name="ffn_second_final" if final_norm else "ffn_second",
    )(x, gain, w_in_bf, w_in_bf, w_out_bf, final_gain)


def _pool_group(x, halo, window, w, scale, t_first):
    tt = x.shape[0]
    ext = jnp.concatenate([halo, x], axis=0)
    row = lax.broadcasted_iota(jnp.int32, ext.shape, 0)
    acc = ext
    span = 1
    while span < window:
        acc = acc + jnp.where(row >= span, pltpu.roll(acc, span, axis=0), 0.0)
        span *= 2
    cnt = jnp.clip(row + (t_first + 1 - POOL_HALO), 1, window).astype(F32)
    pooled = (acc / cnt - ext)[POOL_HALO:, :].astype(BF16)
    return _dot(pooled, w) * scale


def _mixer_body(h_ref, gain_ref, w_ref, wgate_ref, wpool_ref, pscale_ref, wa_ref, ba_ref, gn_ref,
                y_ref,
                win_s, qd_s, ki_s, kt_s, bc_s, v_s, sg_s, st_s, halo_s,
                *, n_cast, nt, d_pool, dk, dv):
    s = pl.program_id(0)
    tt = h_ref.shape[0]
    wc = w_ref.shape[0]
    heads = GLA_HEADS
    group_dim = d_pool // N_POOL_GROUPS
    pool_chunks = d_pool // wc
    q_chunk = pool_chunks
    k_chunk = q_chunk + 1
    v_chunk0 = k_chunk + 1
    v_chunks = heads * dv // wc
    g_chunk0 = v_chunk0 + v_chunks

    @pl.when(s < n_cast)
    def _cast_weights():
        win_s[s] = w_ref[...].astype(BF16)

    @pl.when(s >= n_cast)
    def _compute():
        t = lax.rem(s - n_cast, nt)

        @pl.when(t == 0)
        def _reset():
            st_s[...] = jnp.zeros_like(st_s)
            halo_s[...] = jnp.zeros_like(halo_s)

        n = _rmsnorm(h_ref[...], gain_ref[...]).astype(BF16)

        per_chunk = wc // group_dim
        for c in range(pool_chunks):
            u_c = _dot_nt(n, win_s[c])
            for part in range(per_chunk):
                gi = c * per_chunk + part
                cols = slice(gi * group_dim, (gi + 1) * group_dim)
                x = u_c[:, part * group_dim:(part + 1) * group_dim]
                y = _pool_group(x, halo_s[:, cols], POOL_WINDOWS[gi], wpool_ref[gi].astype(BF16),
                                pscale_ref[:, cols], t * tt)
                halo_s[:, cols] = x[tt - POOL_HALO:, :]
                y_ref[:, cols] = y.astype(y_ref.dtype)

        gate = _dot_nt(n, wgate_ref[...])
        logits = _dot(gate.astype(BF16), wa_ref[...]) + ba_ref[...]
        log_alpha = _log_sigmoid(logits) * (1.0 / GATE_LOGIT_NORMALIZER)
        pos = lax.broadcasted_iota(jnp.int32, log_alpha.shape, 0) & (CHUNK - 1)
        bcum = log_alpha
        span = 1
        while span < CHUNK:
            bcum = bcum + jnp.where(pos >= span, pltpu.roll(bcum, span, axis=0), 0.0)
            span *= 2
        bc_s[...] = bcum
        n_chunks = tt // CHUNK
        b3 = bcum.reshape(n_chunks, CHUNK, heads * dk)
        b_last = jnp.broadcast_to(b3[:, CHUNK - 1:CHUNK, :], b3.shape).reshape(tt, heads * dk)
        q = _dot_nt(n, win_s[q_chunk])
        qd_s[...] = (q * (dk ** -0.5) * jnp.exp(bcum)).astype(BF16)
        k = _dot_nt(n, win_s[k_chunk])
        ki_s[...] = (k * jnp.exp(-bcum)).astype(BF16)
        kt_s[...] = (k * jnp.exp(b_last - bcum)).astype(BF16)
        for c in range(v_chunks):
            cols = slice(c * wc, (c + 1) * wc)
            v_s[:, cols] = _dot_nt(n, win_s[v_chunk0 + c]).astype(BF16)
            sg_s[:, cols] = _silu(_dot_nt(n, win_s[g_chunk0 + c]))

        sb_rows = SB_CHUNKS * CHUNK
        r_idx = lax.broadcasted_iota(jnp.int32, (sb_rows, sb_rows), 0)
        c_idx = lax.broadcasted_iota(jnp.int32, (sb_rows, sb_rows), 1)
        same_chunk_causal = ((r_idx // CHUNK) == (c_idx // CHUNK)) & (r_idx >= c_idx)

        def superblock(sb, carry):
            row0 = pl.multiple_of(sb * sb_rows, sb_rows)
            rows = pl.ds(row0, sb_rows)
            decays = [jnp.exp(bc_s[pl.ds(row0 + (c * CHUNK + CHUNK - 1), 1), :]) for c in range(SB_CHUNKS)]
            for hd in range(heads):
                kc = slice(hd * dk, (hd + 1) * dk)
                vc = slice(hd * dv, (hd + 1) * dv)
                qd = qd_s[rows, kc]
                kt = kt_s[rows, kc]
                v = v_s[rows, vc]
                scores = jnp.where(same_chunk_causal, _dot_nt(qd, ki_s[rows, kc]), 0.0).astype(BF16)
                o = _dot(scores, v)
                state_t = st_s[hd]
                inter = []
                for c in range(SB_CHUNKS):
                    cr = slice(c * CHUNK, (c + 1) * CHUNK)
                    inter.append(_dot_nt(qd[cr], state_t.astype(BF16)))
                    state_t = state_t * decays[c][:, kc] + _dot_tn(v[cr], kt[cr])
                st_s[hd] = state_t
                o = _rmsnorm(o + jnp.concatenate(inter, axis=0), gn_ref[...])
                y_ref[rows, d_pool + hd * dv:d_pool + (hd + 1) * dv] = (o * sg_s[rows, vc]).astype(y_ref.dtype)
            return carry

        lax.fori_loop(0, tt // sb_rows, superblock, 0)


def _mixer(h, gain, w_in_t, w_gate_t, w_pool, pool_scale, w_alpha, b_alpha, gla_norm, *,
           seq, d_pool, dk, dv, tt=512, wc=512):
    t, d = h.shape
    heads = GLA_HEADS
    d_gla = heads * dv
    n_main = d_pool + 2 * heads * dk + 2 * d_gla
    n_cast = n_main // wc
    nt = seq // tt
    n_tiles = t // tt
    group_dim = d_pool // N_POOL_GROUPS
    assert seq % tt == 0 and tt % (SB_CHUNKS * CHUNK) == 0 and n_main % wc == 0 and heads * dk == wc
    assert d_pool % wc == 0 and d_gla % wc == 0 and wc % group_dim == 0
    tile = lambda s: jnp.maximum(s - n_cast, 0)
    const = lambda s: (0, 0)
    return pl.pallas_call(
        functools.partial(_mixer_body, n_cast=n_cast, nt=nt, d_pool=d_pool, dk=dk, dv=dv),
        grid=(n_cast + n_tiles,),
        in_specs=[
            pl.BlockSpec((tt, d), lambda s: (tile(s), 0)),
            pl.BlockSpec((1, d), const),
            pl.BlockSpec((wc, d), lambda s: (jnp.minimum(s, n_cast - 1), 0)),
            pl.BlockSpec((LANES, d), const),
            pl.BlockSpec((N_POOL_GROUPS, group_dim, group_dim), lambda s: (0, 0, 0)),
            pl.BlockSpec((1, d_pool), const),
            pl.BlockSpec((LANES, heads * dk), const),
            pl.BlockSpec((1, heads * dk), const),
            pl.BlockSpec((1, dv), const),
        ],
        out_specs=pl.BlockSpec((tt, d_pool + d_gla), lambda s: (tile(s), 0)),
        out_shape=jax.ShapeDtypeStruct((t, d_pool + d_gla), BF16),
        scratch_shapes=[
            pltpu.VMEM((n_cast, wc, d), BF16),
            pltpu.VMEM((tt, heads * dk), BF16),
            pltpu.VMEM((tt, heads * dk), BF16),
            pltpu.VMEM((tt, heads * dk), BF16),
            pltpu.VMEM((tt, heads * dk), F32),
            pltpu.VMEM((tt, d_gla), BF16),
            pltpu.VMEM((tt, d_gla), F32),
            pltpu.VMEM((heads, dv, dk), F32),
            pltpu.VMEM((POOL_HALO, d_pool), F32),
        ],
        compiler_params=_params(("arbitrary",)),
        name="mixer",
    )(h, gain, w_in_t, w_gate_t, w_pool, pool_scale, w_alpha, b_alpha, gla_norm)


def _mix_out_body(h_ref, y_ref, w_ref, o_ref, w_bf_s):
    @pl.when(pl.program_id(0) == 0)
    def _cast_weights():
        w_bf_s[...] = w_ref[...].astype(BF16)

    o_ref[...] = h_ref[...] + _dot(y_ref[...], w_bf_s[...])


def _mix_out(h, y, w_out, *, tm=512):
    t, d = h.shape
    d_mix = y.shape[1]
    assert t % tm == 0 and w_out.shape == (d_mix, d)
    return pl.pallas_call(
        _mix_out_body,
        grid=(t // tm,),
        in_specs=[
            pl.BlockSpec((tm, d), lambda i: (i, 0)),
            pl.BlockSpec((tm, d_mix), lambda i: (i, 0)),
            pl.BlockSpec((d_mix, d), lambda i: (0, 0), pipeline_mode=pl.Buffered(1)),
        ],
        out_specs=pl.BlockSpec((tm, d), lambda i: (i, 0)),
        out_shape=jax.ShapeDtypeStruct((t, d), F32),
        scratch_shapes=[pltpu.VMEM((d_mix, d), BF16)],
        compiler_params=_params(("arbitrary",)),
        name="mix_out",
    )(h, y, w_out)


def kernel(x, ffn1_norm, ffn1_w_in, ffn1_w_out, mix_norm, w_in_mix, w_pool, pool_scale,
           w_alpha, b_alpha, gla_norm, w_out_mix, ffn2_norm, ffn2_w_in, ffn2_w_out, final_norm):
    b, s, d = x.shape
    depth = ffn1_norm.shape[0]
    d_pool = pool_scale.shape[1]
    dk_total = w_alpha.shape[2]
    dk = dk_total // GLA_HEADS
    dv = gla_norm.shape[1]
    d_gla = GLA_HEADS * dv
    n_main = d_pool + 2 * dk_total + 2 * d_gla
    assert w_in_mix.shape[2] == n_main + GLA_GATE_RANK and s % CHUNK == 0 and depth >= 1

    h = x.reshape(b * s, d)
    final_gain = final_norm.reshape(1, d)
    for l in range(depth):
        last = l == depth - 1
        h, w2_in_bf, w2_out_bf = _ffn_first(h, ffn1_norm[l].reshape(1, d), ffn1_w_in[l], ffn1_w_out[l],
                                            ffn2_w_in[l], ffn2_w_out[l])

        w_mix_t = jnp.swapaxes(w_in_mix[l], 0, 1)
        w_gate_t = jnp.pad(w_mix_t[n_main:, :], ((0, LANES - GLA_GATE_RANK), (0, 0))).astype(BF16)
        w_alpha_pad = jnp.pad(w_alpha[l], ((0, LANES - GLA_GATE_RANK), (0, 0))).astype(BF16)
        y = _mixer(h, mix_norm[l].reshape(1, d), w_mix_t, w_gate_t, w_pool[l],
                   pool_scale[l].reshape(1, d_pool), w_alpha_pad, b_alpha[l].reshape(1, dk_total),
                   gla_norm[l].reshape(1, dv), seq=s, d_pool=d_pool, dk=dk, dv=dv)
        h = _mix_out(h, y, w_out_mix[l])

        h = _ffn_second(h, ffn2_norm[l].reshape(1, d), w2_in_bf, w2_out_bf, final_gain,
                        final_norm=last)
    return h.reshape(b, s, d)
```

```python
import functools

import jax
import jax.numpy as jnp
from jax import lax
from jax.experimental import pallas as pl
from jax.experimental.pallas import tpu as pltpu

F32 = jnp.float32
BF16 = jnp.bfloat16

EPS = 1e-6
POOL_WINDOWS = (2, 4, 8, 16)
N_POOL_GROUPS = len(POOL_WINDOWS)
POOL_HALO = 16
GLA_HEADS = 4
GLA_GATE_RANK = 16
GATE_LOGIT_NORMALIZER = 16.0
CHUNK = 64
SB_CHUNKS = 4
HEAD_WEIGHT_SLOTS = 3

LANES = 128
BF16_SUBLANES = 16
VMEM_LIMIT_BYTES = 58 * 1024 * 1024


def _rmsnorm(x, gain):
    ms = jnp.mean(x * x, axis=-1, keepdims=True)
    return x * lax.rsqrt(ms + EPS) * gain


def _silu(x):
    return x * (1.0 / (1.0 + jnp.exp(-x)))


def _log_sigmoid(z):
    return jnp.minimum(z, 0.0) - jnp.log1p(jnp.exp(-jnp.abs(z)))


def _dot(a, b):
    return jnp.dot(a, b, preferred_element_type=F32)


def _dot_nt(a, b):
    return lax.dot_general(a, b, (((1,), (1,)), ((), ())), preferred_element_type=F32)


def _dot_tn(a, b):
    return lax.dot_general(a, b, (((0,), (0,)), ((), ())), preferred_element_type=F32)


def _params(semantics):
    return pltpu.CompilerParams(dimension_semantics=semantics, vmem_limit_bytes=VMEM_LIMIT_BYTES)


def _ffn_step(j, nj, x_ref, gain_ref, weights, fgain_ref, o_ref, n_ref, final_norm):
    def slice_update(n):
        wg, wu, wo = weights()
        gate = _dot(n, wg)
        up = _dot(n, wu)
        act = (0.5 * _silu(gate) * up).astype(BF16)
        return _dot(act, wo)

    @pl.when(j == 0)
    def _first():
        x = x_ref[...]
        n = _rmsnorm(x, gain_ref[...]).astype(BF16)
        n_ref[...] = n
        o_ref[...] = x + slice_update(n)

    @pl.when(j > 0)
    def _rest():
        o_ref[...] += slice_update(n_ref[...])

    if final_norm:

        @pl.when(j == nj - 1)
        def _finish():
            o_ref[...] = _rmsnorm(o_ref[...], fgain_ref[...])


def _ffn_head_body(x_ref, gain_ref, w_in_hbm, w_out_hbm,
                   o_ref, wg_bf_ref, wu_bf_ref, wo_bf_ref,
                   n_ref, wg_buf, wu_buf, wo_buf, sems):
    j = pl.program_id(0)
    nj = pl.num_programs(0)
    n_slots, _, tf = wg_buf.shape
    d_ff = w_out_hbm.shape[0]
    lookahead = n_slots - 1

    def slice_copies(step, slot):
        lo = pl.multiple_of(step * tf, tf)
        return (
            pltpu.make_async_copy(w_in_hbm.at[:, pl.ds(lo, tf)], wg_buf.at[slot], sems.at[0, slot]),
            pltpu.make_async_copy(w_in_hbm.at[:, pl.ds(d_ff + lo, tf)], wu_buf.at[slot], sems.at[1, slot]),
            pltpu.make_async_copy(w_out_hbm.at[pl.ds(lo, tf), :], wo_buf.at[slot], sems.at[2, slot]),
        )

    @pl.when(j == 0)
    def _prime():
        for step in range(lookahead):
            for c in slice_copies(step, step):
                c.start()

    @pl.when(j + lookahead < nj)
    def _prefetch():
        for c in slice_copies(j + lookahead, lax.rem(j + lookahead, n_slots)):
            c.start()

    slot = lax.rem(j, n_slots)
    for c in slice_copies(j, slot):
        c.wait()

    def weights():
        wg = wg_buf[slot].astype(BF16)
        wu = wu_buf[slot].astype(BF16)
        wo = wo_buf[slot].astype(BF16)
        wg_bf_ref[...] = wg
        wu_bf_ref[...] = wu
        wo_bf_ref[...] = wo
        return wg, wu, wo

    _ffn_step(j, nj, x_ref, gain_ref, weights, None, o_ref, n_ref, False)


def _ffn_tail_body(x_hbm, gain_ref, wg_ref, wu_ref, wo_ref, head_ref, nw_in_ref, nw_out_ref,
                   o_ref, nw_in_bf_ref, nw_out_bf_ref,
                   n_ref, x_buf, x_sem, *, n_copy):
    i = pl.program_id(0)
    j = pl.program_id(1)
    n_tiles = pl.num_programs(0)
    tm = x_buf.shape[0]
    rc = head_ref.shape[0]

    nw_in_bf_ref[...] = nw_in_ref[...].astype(BF16)
    nw_out_bf_ref[...] = nw_out_ref[...].astype(BF16)

    def x_copy(tile):
        rows = pl.ds(pl.multiple_of(tile * tm, tm), tm)
        return pltpu.make_async_copy(x_hbm.at[rows, :], x_buf, x_sem)

    @pl.when((j == 1) & (i + 1 < n_tiles))
    def _prefetch_x():
        x_copy(i + 1).start()

    @pl.when((i == 0) & (j < n_copy))
    def _copy_head_rows():
        o_ref[pl.ds(pl.multiple_of(j * rc, rc), rc), :] = head_ref[...]

    @pl.when((i > 0) & (j == 0))
    def _wait_x():
        x_copy(i).wait()

    @pl.when(i > 0)
    def _compute():
        _ffn_step(j, pl.num_programs(1), x_buf, gain_ref,
                  lambda: (wg_ref[...], wu_ref[...], wo_ref[...]), None, o_ref, n_ref, False)


def _ffn_bf16_body(x_ref, gain_ref, wg_ref, wu_ref, wo_ref, fgain_ref, o_ref, n_ref, *, final_norm):
    _ffn_step(pl.program_id(1), pl.num_programs(1), x_ref, gain_ref,
              lambda: (wg_ref[...], wu_ref[...], wo_ref[...]), fgain_ref, o_ref, n_ref, final_norm)


def _cast_row_block(rows, steps):
    for rb in range(BF16_SUBLANES, rows + 1, BF16_SUBLANES):
        if rows % rb == 0 and rows // rb <= steps:
            return rb
    raise ValueError(f"cannot cast {rows} rows in {steps} steps")


def _ffn_first(x, gain, w_in, w_out, next_w_in, next_w_out, *, tm=1024, tf_head=256, tf_tail=512,
               copy_rows=128):
    t, d = x.shape
    d_ff = w_out.shape[0]
    assert t % tm == 0 and t // tm >= 2 and w_in.shape == (d, 2 * d_ff)
    assert d_ff % tf_head == 0 and d_ff % tf_tail == 0 and d_ff // tf_tail >= 2
    nj = d_ff // tf_head
    vec = pl.BlockSpec((1, d), lambda j: (0, 0))
    h_head, wg_bf, wu_bf, wo_bf = pl.pallas_call(
        _ffn_head_body,
        grid=(nj,),
        in_specs=[
            pl.BlockSpec((tm, d), lambda j: (0, 0), pipeline_mode=pl.Buffered(1)),
            vec,
            pl.BlockSpec(memory_space=pl.ANY),
            pl.BlockSpec(memory_space=pl.ANY),
        ],
        out_specs=[
            pl.BlockSpec((tm, d), lambda j: (0, 0)),
            pl.BlockSpec((d, tf_head), lambda j: (0, j)),
            pl.BlockSpec((d, tf_head), lambda j: (0, j)),
            pl.BlockSpec((tf_head, d), lambda j: (j, 0)),
        ],
        out_shape=[
            jax.ShapeDtypeStruct((tm, d), F32),
            jax.ShapeDtypeStruct((d, d_ff), BF16),
            jax.ShapeDtypeStruct((d, d_ff), BF16),
            jax.ShapeDtypeStruct((d_ff, d), BF16),
        ],
        scratch_shapes=[
            pltpu.VMEM((tm, d), BF16),
            pltpu.VMEM((HEAD_WEIGHT_SLOTS, d, tf_head), F32),
            pltpu.VMEM((HEAD_WEIGHT_SLOTS, d, tf_head), F32),
            pltpu.VMEM((HEAD_WEIGHT_SLOTS, tf_head, d), F32),
            pltpu.SemaphoreType.DMA((3, HEAD_WEIGHT_SLOTS)),
        ],
        compiler_params=_params(("arbitrary",)),
        name="ffn_head",
    )(x, gain, w_in, w_out)

    n_tiles = t // tm
    nj_tail = d_ff // tf_tail
    n_copy = tm // copy_rows
    assert tm % copy_rows == 0 and n_copy <= nj_tail
    steps = n_tiles * nj_tail
    nd, nd_ff2 = next_w_in.shape
    assert next_w_out.shape == (nd_ff2 // 2, nd)
    rb_in = _cast_row_block(nd, steps)
    rb_out = _cast_row_block(nd_ff2 // 2, steps // 2)
    vec2 = pl.BlockSpec((1, d), lambda i, j: (0, 0))
    wcol = lambda i, j: (0, j * jnp.minimum(i, 1))
    wrow = lambda i, j: (j * jnp.minimum(i, 1), 0)
    head_rows = lambda i, j: (jnp.where(i == 0, jnp.minimum(j, n_copy - 1), n_copy - 1), 0)
    cast_in = lambda i, j: (jnp.minimum(i * nj_tail + j, nd // rb_in - 1), 0)
    cast_out = lambda i, j: (jnp.minimum(i * nj_tail + j, nd_ff2 // 2 // rb_out - 1), 0)
    return pl.pallas_call(
        functools.partial(_ffn_tail_body, n_copy=n_copy),
        grid=(n_tiles, nj_tail),
        in_specs=[
            pl.BlockSpec(memory_space=pl.ANY),
            vec2,
            pl.BlockSpec((d, tf_tail), wcol),
            pl.BlockSpec((d, tf_tail), wcol),
            pl.BlockSpec((tf_tail, d), wrow),
            pl.BlockSpec((copy_rows, d), head_rows, pipeline_mode=pl.Buffered(1)),
            pl.BlockSpec((rb_in, nd_ff2), cast_in),
            pl.BlockSpec((rb_out, nd), cast_out),
        ],
        out_specs=[
            pl.BlockSpec((tm, d), lambda i, j: (i, 0)),
            pl.BlockSpec((rb_in, nd_ff2), cast_in),
            pl.BlockSpec((rb_out, nd), cast_out),
        ],
        out_shape=[
            jax.ShapeDtypeStruct((t, d), F32),
            jax.ShapeDtypeStruct((nd, nd_ff2), BF16),
            jax.ShapeDtypeStruct((nd_ff2 // 2, nd), BF16),
        ],
        scratch_shapes=[
            pltpu.VMEM((tm, d), BF16),
            pltpu.VMEM((tm, d), F32),
            pltpu.SemaphoreType.DMA(()),
        ],
        compiler_params=_params(("arbitrary", "arbitrary")),
        name="ffn_tail",
    )(x, gain, wg_bf, wu_bf, wo_bf, h_head, next_w_in, next_w_out)


def _ffn_second(x, gain, w_in_bf, w_out_bf, final_gain, *, final_norm, tm=1024, tf=512):
    t, d = x.shape
    d_ff = w_out_bf.shape[0]
    nj = d_ff // tf
    assert t % tm == 0 and d_ff % tf == 0 and w_in_bf.shape == (d, 2 * d_ff)
    vec = pl.BlockSpec((1, d), lambda i, j: (0, 0))
    return pl.pallas_call(
        functools.partial(_ffn_bf16_body, final_norm=final_norm),
        grid=(t // tm, nj),
        in_specs=[
            pl.BlockSpec((tm, d), lambda i, j: (i, 0)),
            vec,
            pl.BlockSpec((d, tf), lambda i, j: (0, j)),
            pl.BlockSpec((d, tf), lambda i, j: (0, j + nj)),
            pl.BlockSpec((tf, d), lambda i, j: (j, 0)),
            vec,
        ],
        out_specs=pl.BlockSpec((tm, d), lambda i, j: (i, 0)),
        out_shape=jax.ShapeDtypeStruct((t, d), F32),
        scratch_shapes=[pltpu.VMEM((tm, d), BF16)],
        compiler_params=_params(("parallel", "arbitrary")),
        name="ffn_second_final" if final_norm else "ffn_second",
    )(x, gain, w_in_bf, w_in_bf, w_out_bf, final_gain)


def _pool_group(x, halo, window, w, scale, t_first):
    tt = x.shape[0]
    ext = jnp.concatenate([halo, x], axis=0)
    row = lax.broadcasted_iota(jnp.int32, ext.shape, 0)
    acc = ext
    span = 1
    while span < window:
        acc = acc + jnp.where(row >= span, pltpu.roll(acc, span, axis=0), 0.0)
        span *= 2
    cnt = jnp.clip(row + (t_first + 1 - POOL_HALO), 1, window).astype(F32)
    pooled = (acc / cnt - ext)[POOL_HALO:, :].astype(BF16)
    return _dot(pooled, w) * scale


def _mixer_body(h_ref, gain_ref, w_ref, wgate_ref, wpool_ref, pscale_ref, wa_ref, ba_ref, gn_ref,
                y_ref,
                win_s, qd_s, ki_s, kt_s, bc_s, v_s, sg_s, st_s, halo_s,
                *, n_cast, nt, d_pool, dk, dv):
    s = pl.program_id(0)
    tt = h_ref.shape[0]
    wc = w_ref.shape[0]
    heads = GLA_HEADS
    group_dim = d_pool // N_POOL_GROUPS
    pool_chunks = d_pool // wc
    q_chunk = pool_chunks
    k_chunk = q_chunk + 1
    v_chunk0 = k_chunk + 1
    v_chunks = heads * dv // wc
    g_chunk0 = v_chunk0 + v_chunks

    @pl.when(s < n_cast)
    def _cast_weights():
        win_s[s] = w_ref[...].astype(BF16)

    @pl.when(s >= n_cast)
    def _compute():
        t = lax.rem(s - n_cast, nt)

        @pl.when(t == 0)
        def _reset():
            st_s[...] = jnp.zeros_like(st_s)
            halo_s[...] = jnp.zeros_like(halo_s)

        n = _rmsnorm(h_ref[...], gain_ref[...]).astype(BF16)

        per_chunk = wc // group_dim
        for c in range(pool_chunks):
            u_c = _dot_nt(n, win_s[c])
            for part in range(per_chunk):
                gi = c * per_chunk + part
                cols = slice(gi * group_dim, (gi + 1) * group_dim)
                x = u_c[:, part * group_dim:(part + 1) * group_dim]
                y = _pool_group(x, halo_s[:, cols], POOL_WINDOWS[gi], wpool_ref[gi].astype(BF16),
                                pscale_ref[:, cols], t * tt)
                halo_s[:, cols] = x[tt - POOL_HALO:, :]
                y_ref[:, cols] = y.astype(y_ref.dtype)

        gate = _dot_nt(n, wgate_ref[...])
        logits = _dot(gate.astype(BF16), wa_ref[...]) + ba_ref[...]
        log_alpha = _log_sigmoid(logits) * (1.0 / GATE_LOGIT_NORMALIZER)
        pos = lax.broadcasted_iota(jnp.int32, log_alpha.shape, 0) & (CHUNK - 1)
        bcum = log_alpha
        span = 1
        while span < CHUNK:
            bcum = bcum + jnp.where(pos >= span, pltpu.roll(bcum, span, axis=0), 0.0)
            span *= 2
        bc_s[...] = bcum
        n_chunks = tt // CHUNK
        b3 = bcum.reshape(n_chunks, CHUNK, heads * dk)
        b_last = jnp.broadcast_to(b3[:, CHUNK - 1:CHUNK, :], b3.shape).reshape(tt, heads * dk)
        q = _dot_nt(n, win_s[q_chunk])
        qd_s[...] = (q * (dk ** -0.5) * jnp.exp(bcum)).astype(BF16)
        k = _dot_nt(n, win_s[k_chunk])
        ki_s[...] = (k * jnp.exp(-bcum)).astype(BF16)
        kt_s[...] = (k * jnp.exp(b_last - bcum)).astype(BF16)
        for c in range(v_chunks):
            cols = slice(c * wc, (c + 1) * wc)
            v_s[:, cols] = _dot_nt(n, win_s[v_chunk0 + c]).astype(BF16)
            sg_s[:, cols] = _silu(_dot_nt(n, win_s[g_chunk0 + c]))

        sb_rows = SB_CHUNKS * CHUNK
        r_idx = lax.broadcasted_iota(jnp.int32, (sb_rows, sb_rows), 0)
        c_idx = lax.broadcasted_iota(jnp.int32, (sb_rows, sb_rows), 1)
        same_chunk_causal = ((r_idx // CHUNK) == (c_idx // CHUNK)) & (r_idx >= c_idx)

        def superblock(sb, carry):
            row0 = pl.multiple_of(sb * sb_rows, sb_rows)
            rows = pl.ds(row0, sb_rows)
            decays = [jnp.exp(bc_s[pl.ds(row0 + (c * CHUNK + CHUNK - 1), 1), :]) for c in range(SB_CHUNKS)]
            for hd in range(heads):
                kc = slice(hd * dk, (hd + 1) * dk)
                vc = slice(hd * dv, (hd + 1) * dv)
                qd = qd_s[rows, kc]
                kt = kt_s[rows, kc]
                v = v_s[rows, vc]
                scores = jnp.where(same_chunk_causal, _dot_nt(qd, ki_s[rows, kc]), 0.0).astype(BF16)
                o = _dot(scores, v)
                state_t = st_s[hd]
                inter = []
                for c in range(SB_CHUNKS):
                    cr = slice(c * CHUNK, (c + 1) * CHUNK)
                    inter.append(_dot_nt(qd[cr], state_t.astype(BF16)))
                    state_t = state_t * decays[c][:, kc] + _dot_tn(v[cr], kt[cr])
                st_s[hd] = state_t
                o = _rmsnorm(o + jnp.concatenate(inter, axis=0), gn_ref[...])
                y_ref[rows, d_pool + hd * dv:d_pool + (hd + 1) * dv] = (o * sg_s[rows, vc]).astype(y_ref.dtype)
            return carry

        lax.fori_loop(0, tt // sb_rows, superblock, 0)


def _mixer(h, gain, w_in_t, w_gate_t, w_pool, pool_scale, w_alpha, b_alpha, gla_norm, *,
           seq, d_pool, dk, dv, tt=512, wc=512):
    t, d = h.shape
    heads = GLA_HEADS
    d_gla = heads * dv
    n_main = d_pool + 2 * heads * dk + 2 * d_gla
    n_cast = n_main // wc
    nt = seq // tt
    n_tiles = t // tt
    group_dim = d_pool // N_POOL_GROUPS
    assert seq % tt == 0 and tt % (SB_CHUNKS * CHUNK) == 0 and n_main % wc == 0 and heads * dk == wc
    assert d_pool % wc == 0 and d_gla % wc == 0 and wc % group_dim == 0
    tile = lambda s: jnp.maximum(s - n_cast, 0)
    const = lambda s: (0, 0)
    return pl.pallas_call(
        functools.partial(_mixer_body, n_cast=n_cast, nt=nt, d_pool=d_pool, dk=dk, dv=dv),
        grid=(n_cast + n_tiles,),
        in_specs=[
            pl.BlockSpec((tt, d), lambda s: (tile(s), 0)),
            pl.BlockSpec((1, d), const),
            pl.BlockSpec((wc, d), lambda s: (jnp.minimum(s, n_cast - 1), 0)),
            pl.BlockSpec((LANES, d), const),
            pl.BlockSpec((N_POOL_GROUPS, group_dim, group_dim), lambda s: (0, 0, 0)),
            pl.BlockSpec((1, d_pool), const),
            pl.BlockSpec((LANES, heads * dk), const),
            pl.BlockSpec((1, heads * dk), const),
            pl.BlockSpec((1, dv), const),
        ],
        out_specs=pl.BlockSpec((tt, d_pool + d_gla), lambda s: (tile(s), 0)),
        out_shape=jax.ShapeDtypeStruct((t, d_pool + d_gla), BF16),
        scratch_shapes=[
            pltpu.VMEM((n_cast, wc, d), BF16),
            pltpu.VMEM((tt, heads * dk), BF16),
            pltpu.VMEM((tt, heads * dk), BF16),
            pltpu.VMEM((tt, heads * dk), BF16),
            pltpu.VMEM((tt, heads * dk), F32),
            pltpu.VMEM((tt, d_gla), BF16),
            pltpu.VMEM((tt, d_gla), F32),
            pltpu.VMEM((heads, dv, dk), F32),
            pltpu.VMEM((POOL_HALO, d_pool), F32),
        ],
        compiler_params=_params(("arbitrary",)),
        name="mixer",
    )(h, gain, w_in_t, w_gate_t, w_pool, pool_scale, w_alpha, b_alpha, gla_norm)


def _mix_out_body(h_ref, y_ref, w_ref, o_ref, w_bf_s):
    @pl.when(pl.program_id(0) == 0)
    def _cast_weights():
        w_bf_s[...] = w_ref[...].astype(BF16)

    o_ref[...] = h_ref[...] + _dot(y_ref[...], w_bf_s[...])


def _mix_out(h, y, w_out, *, tm=512):
    t, d = h.shape
    d_mix = y.shape[1]
    assert t % tm == 0 and w_out.shape == (d_mix, d)
    return pl.pallas_call(
        _mix_out_body,
        grid=(t // tm,),
        in_specs=[
            pl.BlockSpec((tm, d), lambda i: (i, 0)),
            pl.BlockSpec((tm, d_mix), lambda i: (i, 0)),
            pl.BlockSpec((d_mix, d), lambda i: (0, 0), pipeline_mode=pl.Buffered(1)),
        ],
        out_specs=pl.BlockSpec((tm, d), lambda i: (i, 0)),
        out_shape=jax.ShapeDtypeStruct((t, d), F32),
        scratch_shapes=[pltpu.VMEM((d_mix, d), BF16)],
        compiler_params=_params(("arbitrary",)),
        name="mix_out",
    )(h, y, w_out)


def kernel(x, ffn1_norm, ffn1_w_in, ffn1_w_out, mix_norm, w_in_mix, w_pool, pool_scale,
           w_alpha, b_alpha, gla_norm, w_out_mix, ffn2_norm, ffn2_w_in, ffn2_w_out, final_norm):
    b, s, d = x.shape
    depth = ffn1_norm.shape[0]
    d_pool = pool_scale.shape[1]
    dk_total = w_alpha.shape[2]
    dk = dk_total // GLA_HEADS
    dv = gla_norm.shape[1]
    d_gla = GLA_HEADS * dv
    n_main = d_pool + 2 * dk_total + 2 * d_gla
    assert w_in_mix.shape[2] == n_main + GLA_GATE_RANK and s % CHUNK == 0 and depth >= 1

    h = x.reshape(b * s, d)
    final_gain = final_norm.reshape(1, d)
    for l in range(depth):
        last = l == depth - 1
        h, w2_in_bf, w2_out_bf = _ffn_first(h, ffn1_norm[l].reshape(1, d), ffn1_w_in[l], ffn1_w_out[l],
                                            ffn2_w_in[l], ffn2_w_out[l])

        w_mix_t = jnp.swapaxes(w_in_mix[l], 0, 1)
        w_gate_t = jnp.pad(w_mix_t[n_main:, :], ((0, LANES - GLA_GATE_RANK), (0, 0))).astype(BF16)
        w_alpha_pad = jnp.pad(w_alpha[l], ((0, LANES - GLA_GATE_RANK), (0, 0))).astype(BF16)
        y = _mixer(h, mix_norm[l].reshape(1, d), w_mix_t, w_gate_t, w_pool[l],
                   pool_scale[l].reshape(1, d_pool), w_alpha_pad, b_alpha[l].reshape(1, dk_total),
                   gla_norm[l].reshape(1, dv), seq=s, d_pool=d_pool, dk=dk, dv=dv)
        h = _mix_out(h, y, w_out_mix[l])

        h = _ffn_second(h, ffn2_norm[l].reshape(1, d), w2_in_bf, w2_out_bf, final_gain,
                        final_norm=last)
    return h.reshape(b, s, d)
```

```python
import functools

import jax
import jax.numpy as jnp
from jax import lax
from jax.experimental import pallas as pl
from jax.experimental.pallas import tpu as pltpu

F32 = jnp.float32
BF16 = jnp.bfloat16

EPS = 1e-6
POOL_WINDOWS = (2, 4, 8, 16)
N_POOL_GROUPS = len(POOL_WINDOWS)
POOL_HALO = 16
GLA_HEADS = 4
GLA_GATE_RANK = 16
GATE_LOGIT_NORMALIZER = 16.0
CHUNK = 64
SB_CHUNKS = 4
HEAD_WEIGHT_SLOTS = 3

LANES = 128
BF16_SUBLANES = 16
VMEM_LIMIT_BYTES = 58 * 1024 * 1024


def _rmsnorm(x, gain):
    ms = jnp.mean(x * x, axis=-1, keepdims=True)
    return x * lax.rsqrt(ms + EPS) * gain


def _silu(x):
    return x * (1.0 / (1.0 + jnp.exp(-x)))


def _log_sigmoid(z):
    return jnp.minimum(z, 0.0) - jnp.log1p(jnp.exp(-jnp.abs(z)))


def _dot(a, b):
    return jnp.dot(a, b, preferred_element_type=F32)


def _dot_nt(a, b):
    return lax.dot_general(a, b, (((1,), (1,)), ((), ())), preferred_element_type=F32)


def _dot_tn(a, b):
    return lax.dot_general(a, b, (((0,), (0,)), ((), ())), preferred_element_type=F32)


def _params(semantics):
    return pltpu.CompilerParams(dimension_semantics=semantics, vmem_limit_bytes=VMEM_LIMIT_BYTES)


def _ffn_step(j, nj, x_ref, gain_ref, weights, fgain_ref, o_ref, n_ref, final_norm,
              last_weights=None):
    def slice_update(n, get_weights):
        wg, wu, wo = get_weights()
        gate = _dot(n, wg)
        up = _dot(n, wu)
        act = (0.5 * _silu(gate) * up).astype(BF16)
        return _dot(act, wo)

    @pl.when(j == 0)
    def _first():
        x = x_ref[...]
        n = _rmsnorm(x, gain_ref[...]).astype(BF16)
        n_ref[...] = n
        o_ref[...] = x + slice_update(n, weights)

    if last_weights is None:

        @pl.when(j > 0)
        def _rest():
            o_ref[...] += slice_update(n_ref[...], weights)

    else:

        @pl.when((j > 0) & (j < nj - 1))
        def _middle():
            o_ref[...] += slice_update(n_ref[...], weights)

        @pl.when(j == nj - 1)
        def _last():
            o_ref[...] += slice_update(n_ref[...], last_weights)

    if final_norm:

        @pl.when(j == nj - 1)
        def _finish():
            o_ref[...] = _rmsnorm(o_ref[...], fgain_ref[...])


def _ffn_head_body(x_ref, gain_ref, w_in_hbm, w_out_hbm,
                   o_ref, wg_bf_ref, wu_bf_ref, wo_bf_ref,
                   n_ref, wg_buf, wu_buf, wo_buf, sems):
    j = pl.program_id(0)
    nj = pl.num_programs(0)
    n_slots, _, tf = wg_buf.shape
    d_ff = w_out_hbm.shape[0]
    lookahead = n_slots - 1

    def slice_copies(step, slot):
        lo = pl.multiple_of(step * tf, tf)
        return (
            pltpu.make_async_copy(w_in_hbm.at[:, pl.ds(lo, tf)], wg_buf.at[slot], sems.at[0, slot]),
            pltpu.make_async_copy(w_in_hbm.at[:, pl.ds(d_ff + lo, tf)], wu_buf.at[slot], sems.at[1, slot]),
            pltpu.make_async_copy(w_out_hbm.at[pl.ds(lo, tf), :], wo_buf.at[slot], sems.at[2, slot]),
        )

    @pl.when(j == 0)
    def _prime():
        for step in range(lookahead):
            for c in slice_copies(step, step):
                c.start()

    @pl.when(j + lookahead < nj)
    def _prefetch():
        for c in slice_copies(j + lookahead, lax.rem(j + lookahead, n_slots)):
            c.start()

    slot = lax.rem(j, n_slots)
    for c in slice_copies(j, slot):
        c.wait()

    def weights():
        wg = wg_buf[slot].astype(BF16)
        wu = wu_buf[slot].astype(BF16)
        wo = wo_buf[slot].astype(BF16)
        wg_bf_ref[...] = wg
        wu_bf_ref[...] = wu
        wo_bf_ref[...] = wo
        return wg, wu, wo

    _ffn_step(j, nj, x_ref, gain_ref, weights, None, o_ref, n_ref, False)


def _ffn_tail_body(x_hbm, gain_ref, wg_ref, wu_ref, wo_ref, head_ref, nw_in_ref, nw_out_ref,
                   o_ref, nwg_bf_ref, nwu_bf_ref, nwo_bf_ref,
                   n_ref, x_buf, x_sem, *, n_copy):
    i = pl.program_id(0)
    j = pl.program_id(1)
    n_tiles = pl.num_programs(0)
    tm = x_buf.shape[0]
    rc = head_ref.shape[0]

    nd_ff = nwg_bf_ref.shape[1]
    nwg_bf_ref[...] = nw_in_ref[:, :nd_ff].astype(BF16)
    nwu_bf_ref[...] = nw_in_ref[:, nd_ff:].astype(BF16)
    nwo_bf_ref[...] = nw_out_ref[...].astype(BF16)

    def x_copy(tile):
        rows = pl.ds(pl.multiple_of(tile * tm, tm), tm)
        return pltpu.make_async_copy(x_hbm.at[rows, :], x_buf, x_sem)

    @pl.when((j == 1) & (i + 1 < n_tiles))
    def _prefetch_x():
        x_copy(i + 1).start()

    @pl.when((i == 0) & (j < n_copy))
    def _copy_head_rows():
        o_ref[pl.ds(pl.multiple_of(j * rc, rc), rc), :] = head_ref[...]

    @pl.when((i > 0) & (j == 0))
    def _wait_x():
        x_copy(i).wait()

    @pl.when(i > 0)
    def _compute():
        _ffn_step(j, pl.num_programs(1), x_buf, gain_ref,
                  lambda: (wg_ref[...], wu_ref[...], wo_ref[...]), None, o_ref, n_ref, False)


def _ffn_bf16_body(x_hbm, gain_ref, wg_ref, wu_ref, wo_ref, fgain_ref, o_ref, n_ref, x_buf, x_sem,
                   *, final_norm, last_width):
    i = pl.program_id(0)
    j = pl.program_id(1)
    tm = x_buf.shape[0]

    def x_copy(tile):
        rows = pl.ds(pl.multiple_of(tile * tm, tm), tm)
        return pltpu.make_async_copy(x_hbm.at[rows, :], x_buf, x_sem)

    @pl.when((i == 0) & (j == 0))
    def _first_x():
        x_copy(0).start()

    @pl.when((j == 1) & (i + 1 < pl.num_programs(0)))
    def _prefetch_x():
        x_copy(i + 1).start()

    @pl.when(j == 0)
    def _wait_x():
        x_copy(i).wait()

    last_weights = None
    if last_width is not None:
        last_weights = lambda: (wg_ref[:, :last_width], wu_ref[:, :last_width], wo_ref[:last_width, :])
    _ffn_step(j, pl.num_programs(1), x_buf, gain_ref,
              lambda: (wg_ref[...], wu_ref[...], wo_ref[...]), fgain_ref, o_ref, n_ref, final_norm,
              last_weights=last_weights)


def _cast_row_block(rows, steps):
    for rb in range(BF16_SUBLANES, rows + 1, BF16_SUBLANES):
        if rows % rb == 0 and rows // rb <= steps:
            return rb
    raise ValueError(f"cannot cast {rows} rows in {steps} steps")


def _ffn_first(x, gain, w_in, w_out, next_w_in, next_w_out, *, tm=1024, tf_head=256, tf_tail=512,
               copy_rows=128):
    t, d = x.shape
    d_ff = w_out.shape[0]
    assert t % tm == 0 and t // tm >= 2 and w_in.shape == (d, 2 * d_ff)
    assert d_ff % tf_head == 0 and d_ff % tf_tail == 0 and d_ff // tf_tail >= 2
    nj = d_ff // tf_head
    vec = pl.BlockSpec((1, d), lambda j: (0, 0))
    h_head, wg_bf, wu_bf, wo_bf = pl.pallas_call(
        _ffn_head_body,
        grid=(nj,),
        in_specs=[
            pl.BlockSpec((tm, d), lambda j: (0, 0), pipeline_mode=pl.Buffered(1)),
            vec,
            pl.BlockSpec(memory_space=pl.ANY),
            pl.BlockSpec(memory_space=pl.ANY),
        ],
        out_specs=[
            pl.BlockSpec((tm, d), lambda j: (0, 0)),
            pl.BlockSpec((d, tf_head), lambda j: (0, j)),
            pl.BlockSpec((d, tf_head), lambda j: (0, j)),
            pl.BlockSpec((tf_head, d), lambda j: (j, 0)),
        ],
        out_shape=[
            jax.ShapeDtypeStruct((tm, d), F32),
            jax.ShapeDtypeStruct((d, d_ff), BF16),
            jax.ShapeDtypeStruct((d, d_ff), BF16),
            jax.ShapeDtypeStruct((d_ff, d), BF16),
        ],
        scratch_shapes=[
            pltpu.VMEM((tm, d), BF16),
            pltpu.VMEM((HEAD_WEIGHT_SLOTS, d, tf_head), F32),
            pltpu.VMEM((HEAD_WEIGHT_SLOTS, d, tf_head), F32),
            pltpu.VMEM((HEAD_WEIGHT_SLOTS, tf_head, d), F32),
            pltpu.SemaphoreType.DMA((3, HEAD_WEIGHT_SLOTS)),
        ],
        compiler_params=_params(("arbitrary",)),
        name="ffn_head",
    )(x, gain, w_in, w_out)

    n_tiles = t // tm
    nj_tail = d_ff // tf_tail
    n_copy = tm // copy_rows
    assert tm % copy_rows == 0 and n_copy <= nj_tail
    steps = n_tiles * nj_tail
    nd, nd_ff2 = next_w_in.shape
    nd_ff = nd_ff2 // 2
    assert next_w_out.shape == (nd_ff, nd) and nd_ff % LANES == 0
    rb_in = _cast_row_block(nd, steps)
    rb_out = _cast_row_block(nd_ff, steps)
    vec2 = pl.BlockSpec((1, d), lambda i, j: (0, 0))
    wcol = lambda i, j: (0, j * jnp.minimum(i, 1))
    wrow = lambda i, j: (j * jnp.minimum(i, 1), 0)
    head_rows = lambda i, j: (jnp.where(i == 0, jnp.minimum(j, n_copy - 1), n_copy - 1), 0)
    cast_in = lambda i, j: (jnp.minimum(i * nj_tail + j, nd // rb_in - 1), 0)
    cast_out = lambda i, j: (jnp.minimum(i * nj_tail + j, nd_ff // rb_out - 1), 0)
    return pl.pallas_call(
        functools.partial(_ffn_tail_body, n_copy=n_copy),
        grid=(n_tiles, nj_tail),
        in_specs=[
            pl.BlockSpec(memory_space=pl.ANY),
            vec2,
            pl.BlockSpec((d, tf_tail), wcol),
            pl.BlockSpec((d, tf_tail), wcol),
            pl.BlockSpec((tf_tail, d), wrow),
            pl.BlockSpec((copy_rows, d), head_rows, pipeline_mode=pl.Buffered(1)),
            pl.BlockSpec((rb_in, nd_ff2), cast_in),
            pl.BlockSpec((rb_out, nd), cast_out),
        ],
        out_specs=[
            pl.BlockSpec((tm, d), lambda i, j: (i, 0)),
            pl.BlockSpec((rb_in, nd_ff), cast_in),
            pl.BlockSpec((rb_in, nd_ff), cast_in),
            pl.BlockSpec((rb_out, nd), cast_out),
        ],
        out_shape=[
            jax.ShapeDtypeStruct((t, d), F32),
            jax.ShapeDtypeStruct((nd, nd_ff), BF16),
            jax.ShapeDtypeStruct((nd, nd_ff), BF16),
            jax.ShapeDtypeStruct((nd_ff, nd), BF16),
        ],
        scratch_shapes=[
            pltpu.VMEM((tm, d), BF16),
            pltpu.VMEM((tm, d), F32),
            pltpu.SemaphoreType.DMA(()),
        ],
        compiler_params=_params(("arbitrary", "arbitrary")),
        name="ffn_tail",
    )(x, gain, wg_bf, wu_bf, wo_bf, h_head, next_w_in, next_w_out)


def _ffn_second(x, gain, wg_bf, wu_bf, wo_bf, final_gain, *, final_norm, tm=1024, tf=768):
    t, d = x.shape
    d_ff = wo_bf.shape[0]
    nj = pl.cdiv(d_ff, tf)
    last_width = d_ff - (nj - 1) * tf
    assert t % tm == 0 and nj >= 2 and wg_bf.shape == wu_bf.shape == (d, d_ff)
    assert last_width % LANES == 0
    vec = pl.BlockSpec((1, d), lambda i, j: (0, 0))
    return pl.pallas_call(
        functools.partial(_ffn_bf16_body, final_norm=final_norm,
                          last_width=None if last_width == tf else last_width),
        grid=(t // tm, nj),
        in_specs=[
            pl.BlockSpec(memory_space=pl.ANY),
            vec,
            pl.BlockSpec((d, tf), lambda i, j: (0, j)),
            pl.BlockSpec((d, tf), lambda i, j: (0, j)),
            pl.BlockSpec((tf, d), lambda i, j: (j, 0)),
            vec,
        ],
        out_specs=pl.BlockSpec((tm, d), lambda i, j: (i, 0)),
        out_shape=jax.ShapeDtypeStruct((t, d), F32),
        scratch_shapes=[
            pltpu.VMEM((tm, d), BF16),
            pltpu.VMEM((tm, d), F32),
            pltpu.SemaphoreType.DMA(()),
        ],
        compiler_params=_params(("arbitrary", "arbitrary")),
        name="ffn_second_final" if final_norm else "ffn_second",
    )(x, gain, wg_bf, wu_bf, wo_bf, final_gain)


def _pool_group(x, halo, window, w, scale, t_first):
    tt = x.shape[0]
    ext = jnp.concatenate([halo, x], axis=0)
    row = lax.broadcasted_iota(jnp.int32, ext.shape, 0)
    acc = ext
    span = 1
    while span < window:
        acc = acc + jnp.where(row >= span, pltpu.roll(acc, span, axis=0), 0.0)
        span *= 2
    cnt = jnp.clip(row + (t_first + 1 - POOL_HALO), 1, window).astype(F32)
    pooled = (acc / cnt - ext)[POOL_HALO:, :].astype(BF16)
    return _dot(pooled, w) * scale


def _mixer_body(h_ref, gain_ref, w_ref, wgate_ref, wpool_ref, pscale_ref, wa_ref, ba_ref, gn_ref,
                y_ref,
                win_s, qd_s, ki_s, kt_s, bc_s, v_s, sg_s, st_s, halo_s,
                *, n_cast, nt, d_pool, dk, dv):
    s = pl.program_id(0)
    tt = h_ref.shape[0]
    wc = w_ref.shape[0]
    heads = GLA_HEADS
    group_dim = d_pool // N_POOL_GROUPS
    pool_chunks = d_pool // wc
    q_chunk = pool_chunks
    k_chunk = q_chunk + 1
    v_chunk0 = k_chunk + 1
    v_chunks = heads * dv // wc
    g_chunk0 = v_chunk0 + v_chunks

    @pl.when(s < n_cast)
    def _cast_weights():
        win_s[s] = w_ref[...].astype(BF16)

    @pl.when(s >= n_cast)
    def _compute():
        t = lax.rem(s - n_cast, nt)

        @pl.when(t == 0)
        def _reset():
            st_s[...] = jnp.zeros_like(st_s)
            halo_s[...] = jnp.zeros_like(halo_s)

        n = _rmsnorm(h_ref[...], gain_ref[...]).astype(BF16)

        per_chunk = wc // group_dim
        for c in range(pool_chunks):
            u_c = _dot_nt(n, win_s[c])
            for part in range(per_chunk):
                gi = c * per_chunk + part
                cols = slice(gi * group_dim, (gi + 1) * group_dim)
                x = u_c[:, part * group_dim:(part + 1) * group_dim]
                y = _pool_group(x, halo_s[:, cols], POOL_WINDOWS[gi], wpool_ref[gi].astype(BF16),
                                pscale_ref[:, cols], t * tt)
                halo_s[:, cols] = x[tt - POOL_HALO:, :]
                y_ref[:, cols] = y.astype(y_ref.dtype)

        gate = _dot_nt(n, wgate_ref[...])
        logits = _dot(gate.astype(BF16), wa_ref[...]) + ba_ref[...]
        log_alpha = _log_sigmoid(logits) * (1.0 / GATE_LOGIT_NORMALIZER)
        pos = lax.broadcasted_iota(jnp.int32, log_alpha.shape, 0) & (CHUNK - 1)
        bcum = log_alpha
        span = 1
        while span < CHUNK:
            bcum = bcum + jnp.where(pos >= span, pltpu.roll(bcum, span, axis=0), 0.0)
            span *= 2
        bc_s[...] = bcum
        n_chunks = tt // CHUNK
        b3 = bcum.reshape(n_chunks, CHUNK, heads * dk)
        b_last = jnp.broadcast_to(b3[:, CHUNK - 1:CHUNK, :], b3.shape).reshape(tt, heads * dk)
        q = _dot_nt(n, win_s[q_chunk])
        qd_s[...] = (q * (dk ** -0.5) * jnp.exp(bcum)).astype(BF16)
        k = _dot_nt(n, win_s[k_chunk])
        ki_s[...] = (k * jnp.exp(-bcum)).astype(BF16)
        kt_s[...] = (k * jnp.exp(b_last - bcum)).astype(BF16)
        for c in range(v_chunks):
            cols = slice(c * wc, (c + 1) * wc)
            v_s[:, cols] = _dot_nt(n, win_s[v_chunk0 + c]).astype(BF16)
            sg_s[:, cols] = _silu(_dot_nt(n, win_s[g_chunk0 + c]))

        sb_rows = SB_CHUNKS * CHUNK
        r_idx = lax.broadcasted_iota(jnp.int32, (sb_rows, sb_rows), 0)
        c_idx = lax.broadcasted_iota(jnp.int32, (sb_rows, sb_rows), 1)
        same_chunk_causal = ((r_idx // CHUNK) == (c_idx // CHUNK)) & (r_idx >= c_idx)

        def superblock(sb, carry):
            row0 = pl.multiple_of(sb * sb_rows, sb_rows)
            rows = pl.ds(row0, sb_rows)
            decays = [jnp.exp(bc_s[pl.ds(row0 + (c * CHUNK + CHUNK - 1), 1), :]) for c in range(SB_CHUNKS)]
            for hd in range(heads):
                kc = slice(hd * dk, (hd + 1) * dk)
                vc = slice(hd * dv, (hd + 1) * dv)
                qd = qd_s[rows, kc]
                kt = kt_s[rows, kc]
                v = v_s[rows, vc]
                scores = jnp.where(same_chunk_causal, _dot_nt(qd, ki_s[rows, kc]), 0.0).astype(BF16)
                o = _dot(scores, v)
                state_t = st_s[hd]
                inter = []
                for c in range(SB_CHUNKS):
                    cr = slice(c * CHUNK, (c + 1) * CHUNK)
                    inter.append(_dot_nt(qd[cr], state_t.astype(BF16)))
                    state_t = state_t * decays[c][:, kc] + _dot_tn(v[cr], kt[cr])
                st_s[hd] = state_t
                o = _rmsnorm(o + jnp.concatenate(inter, axis=0), gn_ref[...])
                y_ref[rows, d_pool + hd * dv:d_pool + (hd + 1) * dv] = (o * sg_s[rows, vc]).astype(y_ref.dtype)
            return carry

        lax.fori_loop(0, tt // sb_rows, superblock, 0)


def _mixer(h, gain, w_in_t, w_gate_t, w_pool, pool_scale, w_alpha, b_alpha, gla_norm, *,
           seq, d_pool, dk, dv, tt=512, wc=512):
    t, d = h.shape
    heads = GLA_HEADS
    d_gla = heads * dv
    n_main = d_pool + 2 * heads * dk + 2 * d_gla
    n_cast = n_main // wc
    nt = seq // tt
    n_tiles = t // tt
    group_dim = d_pool // N_POOL_GROUPS
    assert seq % tt == 0 and tt % (SB_CHUNKS * CHUNK) == 0 and n_main % wc == 0 and heads * dk == wc
    assert d_pool % wc == 0 and d_gla % wc == 0 and wc % group_dim == 0
    tile = lambda s: jnp.maximum(s - n_cast, 0)
    const = lambda s: (0, 0)
    return pl.pallas_call(
        functools.partial(_mixer_body, n_cast=n_cast, nt=nt, d_pool=d_pool, dk=dk, dv=dv),
        grid=(n_cast + n_tiles,),
        in_specs=[
            pl.BlockSpec((tt, d), lambda s: (tile(s), 0)),
            pl.BlockSpec((1, d), const),
            pl.BlockSpec((wc, d), lambda s: (jnp.minimum(s, n_cast - 1), 0)),
            pl.BlockSpec((LANES, d), const),
            pl.BlockSpec((N_POOL_GROUPS, group_dim, group_dim), lambda s: (0, 0, 0)),
            pl.BlockSpec((1, d_pool), const),
            pl.BlockSpec((LANES, heads * dk), const),
            pl.BlockSpec((1, heads * dk), const),
            pl.BlockSpec((1, dv), const),
        ],
        out_specs=pl.BlockSpec((tt, d_pool + d_gla), lambda s: (tile(s), 0)),
        out_shape=jax.ShapeDtypeStruct((t, d_pool + d_gla), BF16),
        scratch_shapes=[
            pltpu.VMEM((n_cast, wc, d), BF16),
            pltpu.VMEM((tt, heads * dk), BF16),
            pltpu.VMEM((tt, heads * dk), BF16),
            pltpu.VMEM((tt, heads * dk), BF16),
            pltpu.VMEM((tt, heads * dk), F32),
            pltpu.VMEM((tt, d_gla), BF16),
            pltpu.VMEM((tt, d_gla), F32),
            pltpu.VMEM((heads, dv, dk), F32),
            pltpu.VMEM((POOL_HALO, d_pool), F32),
        ],
        compiler_params=_params(("arbitrary",)),
        name="mixer",
    )(h, gain, w_in_t, w_gate_t, w_pool, pool_scale, w_alpha, b_alpha, gla_norm)


def _mix_out_body(h_ref, y_ref, w_ref, o_ref, w_bf_s):
    @pl.when(pl.program_id(0) == 0)
    def _cast_weights():
        w_bf_s[...] = w_ref[...].astype(BF16)

    o_ref[...] = h_ref[...] + _dot(y_ref[...], w_bf_s[...])


def _mix_out(h, y, w_out, *, tm=512):
    t, d = h.shape
    d_mix = y.shape[1]
    assert t % tm == 0 and w_out.shape == (d_mix, d)
    return pl.pallas_call(
        _mix_out_body,
        grid=(t // tm,),
        in_specs=[
            pl.BlockSpec((tm, d), lambda i: (i, 0)),
            pl.BlockSpec((tm, d_mix), lambda i: (i, 0)),
            pl.BlockSpec((d_mix, d), lambda i: (0, 0), pipeline_mode=pl.Buffered(1)),
        ],
        out_specs=pl.BlockSpec((tm, d), lambda i: (i, 0)),
        out_shape=jax.ShapeDtypeStruct((t, d), F32),
        scratch_shapes=[pltpu.VMEM((d_mix, d), BF16)],
        compiler_params=_params(("arbitrary",)),
        name="mix_out",
    )(h, y, w_out)


def kernel(x, ffn1_norm, ffn1_w_in, ffn1_w_out, mix_norm, w_in_mix, w_pool, pool_scale,
           w_alpha, b_alpha, gla_norm, w_out_mix, ffn2_norm, ffn2_w_in, ffn2_w_out, final_norm):
    b, s, d = x.shape
    depth = ffn1_norm.shape[0]
    d_pool = pool_scale.shape[1]
    dk_total = w_alpha.shape[2]
    dk = dk_total // GLA_HEADS
    dv = gla_norm.shape[1]
    d_gla = GLA_HEADS * dv
    n_main = d_pool + 2 * dk_total + 2 * d_gla
    assert w_in_mix.shape[2] == n_main + GLA_GATE_RANK and s % CHUNK == 0 and depth >= 1

    h = x.reshape(b * s, d)
    final_gain = final_norm.reshape(1, d)
    for l in range(depth):
        last = l == depth - 1
        h, wg2_bf, wu2_bf, wo2_bf = _ffn_first(h, ffn1_norm[l].reshape(1, d), ffn1_w_in[l], ffn1_w_out[l],
                                            ffn2_w_in[l], ffn2_w_out[l])

        w_mix_t = jnp.swapaxes(w_in_mix[l], 0, 1)
        w_gate_t = jnp.pad(w_mix_t[n_main:, :], ((0, LANES - GLA_GATE_RANK), (0, 0))).astype(BF16)
        w_alpha_pad = jnp.pad(w_alpha[l], ((0, LANES - GLA_GATE_RANK), (0, 0))).astype(BF16)
        y = _mixer(h, mix_norm[l].reshape(1, d), w_mix_t, w_gate_t, w_pool[l],
                   pool_scale[l].reshape(1, d_pool), w_alpha_pad, b_alpha[l].reshape(1, dk_total),
                   gla_norm[l].reshape(1, dv), seq=s, d_pool=d_pool, dk=dk, dv=dv)
        h = _mix_out(h, y, w_out_mix[l])

        h = _ffn_second(h, ffn2_norm[l].reshape(1, d), wg2_bf, wu2_bf, wo2_bf, final_gain,
                        final_norm=last)
    return h.reshape(b, s, d)
```

```python
import functools

import jax
import jax.numpy as jnp
from jax import lax
from jax.experimental import pallas as pl
from jax.experimental.pallas import tpu as pltpu

F32 = jnp.float32
BF16 = jnp.bfloat16

EPS = 1e-6
POOL_WINDOWS = (2, 4, 8, 16)
N_POOL_GROUPS = len(POOL_WINDOWS)
POOL_HALO = 16
GLA_HEADS = 4
GLA_GATE_RANK = 16
GATE_LOGIT_NORMALIZER = 16.0
CHUNK = 64
SB_CHUNKS = 4
HEAD_WEIGHT_SLOTS = 3

LANES = 128
BF16_SUBLANES = 16
VMEM_LIMIT_BYTES = 58 * 1024 * 1024
VMEM_LIMIT_TAIL_BYTES = 62 * 1024 * 1024


def _rmsnorm(x, gain):
    ms = jnp.mean(x * x, axis=-1, keepdims=True)
    return x * lax.rsqrt(ms + EPS) * gain


def _silu(x):
    return x * (1.0 / (1.0 + jnp.exp(-x)))


def _log_sigmoid(z):
    return jnp.minimum(z, 0.0) - jnp.log1p(jnp.exp(-jnp.abs(z)))


def _dot(a, b):
    return jnp.dot(a, b, preferred_element_type=F32)


def _dot_nt(a, b):
    return lax.dot_general(a, b, (((1,), (1,)), ((), ())), preferred_element_type=F32)


def _dot_tn(a, b):
    return lax.dot_general(a, b, (((0,), (0,)), ((), ())), preferred_element_type=F32)


def _params(semantics, vmem_limit_bytes=VMEM_LIMIT_BYTES):
    return pltpu.CompilerParams(dimension_semantics=semantics, vmem_limit_bytes=vmem_limit_bytes)


def _ffn_step(j, nj, x_ref, gain_ref, weights, fgain_ref, o_ref, n_ref, final_norm,
              last_weights=None):
    def slice_update(n, get_weights):
        wg, wu, wo = get_weights()
        gate = _dot(n, wg)
        up = _dot(n, wu)
        act = (0.5 * _silu(gate) * up).astype(BF16)
        return _dot(act, wo)

    @pl.when(j == 0)
    def _first():
        x = x_ref[...]
        n = _rmsnorm(x, gain_ref[...]).astype(BF16)
        n_ref[...] = n
        o_ref[...] = x + slice_update(n, weights)

    if last_weights is None:

        @pl.when(j > 0)
        def _rest():
            o_ref[...] += slice_update(n_ref[...], weights)

    else:

        @pl.when((j > 0) & (j < nj - 1))
        def _middle():
            o_ref[...] += slice_update(n_ref[...], weights)

        @pl.when(j == nj - 1)
        def _last():
            o_ref[...] += slice_update(n_ref[...], last_weights)

    if final_norm:

        @pl.when(j == nj - 1)
        def _finish():
            o_ref[...] = _rmsnorm(o_ref[...], fgain_ref[...])


def _ffn_head_body(x_ref, gain_ref, w_in_hbm, w_out_hbm,
                   o_ref, wg_bf_ref, wu_bf_ref, wo_bf_ref,
                   n_ref, wg_buf, wu_buf, wo_buf, sems):
    j = pl.program_id(0)
    nj = pl.num_programs(0)
    n_slots, _, tf = wg_buf.shape
    d_ff = w_out_hbm.shape[0]
    lookahead = n_slots - 1

    def slice_copies(step, slot):
        lo = pl.multiple_of(step * tf, tf)
        return (
            pltpu.make_async_copy(w_in_hbm.at[:, pl.ds(lo, tf)], wg_buf.at[slot], sems.at[0, slot]),
            pltpu.make_async_copy(w_in_hbm.at[:, pl.ds(d_ff + lo, tf)], wu_buf.at[slot], sems.at[1, slot]),
            pltpu.make_async_copy(w_out_hbm.at[pl.ds(lo, tf), :], wo_buf.at[slot], sems.at[2, slot]),
        )

    @pl.when(j == 0)
    def _prime():
        for step in range(lookahead):
            for c in slice_copies(step, step):
                c.start()

    @pl.when(j + lookahead < nj)
    def _prefetch():
        for c in slice_copies(j + lookahead, lax.rem(j + lookahead, n_slots)):
            c.start()

    slot = lax.rem(j, n_slots)
    for c in slice_copies(j, slot):
        c.wait()

    def weights():
        wg = wg_buf[slot].astype(BF16)
        wu = wu_buf[slot].astype(BF16)
        wo = wo_buf[slot].astype(BF16)
        wg_bf_ref[...] = wg
        wu_bf_ref[...] = wu
        wo_bf_ref[...] = wo
        return wg, wu, wo

    _ffn_step(j, nj, x_ref, gain_ref, weights, None, o_ref, n_ref, False)


def _ffn_tail_body(x_hbm, gain_ref, wg_ref, wu_ref, wo_ref, head_ref, nw_in_ref, nw_out_ref,
                   o_ref, nwg_bf_ref, nwu_bf_ref, nwo_bf_ref,
                   n_ref, x_buf, x_sem, *, n_copy, last_width):
    i = pl.program_id(0)
    j = pl.program_id(1)
    n_tiles = pl.num_programs(0)
    tm = x_buf.shape[0]
    rc = head_ref.shape[0]

    nd_ff = nwg_bf_ref.shape[1]
    nwg_bf_ref[...] = nw_in_ref[:, :nd_ff].astype(BF16)
    nwu_bf_ref[...] = nw_in_ref[:, nd_ff:].astype(BF16)
    nwo_bf_ref[...] = nw_out_ref[...].astype(BF16)

    def x_copy(tile):
        rows = pl.ds(pl.multiple_of(tile * tm, tm), tm)
        return pltpu.make_async_copy(x_hbm.at[rows, :], x_buf, x_sem)

    @pl.when((j == 1) & (i + 1 < n_tiles))
    def _prefetch_x():
        x_copy(i + 1).start()

    @pl.when((i == 0) & (j < n_copy))
    def _copy_head_rows():
        o_ref[pl.ds(pl.multiple_of(j * rc, rc), rc), :] = head_ref[...]

    @pl.when((i > 0) & (j == 0))
    def _wait_x():
        x_copy(i).wait()

    last_weights = None
    if last_width is not None:
        last_weights = lambda: (wg_ref[:, :last_width], wu_ref[:, :last_width], wo_ref[:last_width, :])

    @pl.when(i > 0)
    def _compute():
        _ffn_step(j, pl.num_programs(1), x_buf, gain_ref,
                  lambda: (wg_ref[...], wu_ref[...], wo_ref[...]), None, o_ref, n_ref, False,
                  last_weights=last_weights)


def _ffn_bf16_body(x_hbm, gain_ref, wg_ref, wu_ref, wo_ref, fgain_ref, o_ref, n_ref, x_buf, x_sem,
                   *, final_norm, last_width):
    i = pl.program_id(0)
    j = pl.program_id(1)
    tm = x_buf.shape[0]

    def x_copy(tile):
        rows = pl.ds(pl.multiple_of(tile * tm, tm), tm)
        return pltpu.make_async_copy(x_hbm.at[rows, :], x_buf, x_sem)

    @pl.when((i == 0) & (j == 0))
    def _first_x():
        x_copy(0).start()

    @pl.when((j == 1) & (i + 1 < pl.num_programs(0)))
    def _prefetch_x():
        x_copy(i + 1).start()

    @pl.when(j == 0)
    def _wait_x():
        x_copy(i).wait()

    last_weights = None
    if last_width is not None:
        last_weights = lambda: (wg_ref[:, :last_width], wu_ref[:, :last_width], wo_ref[:last_width, :])
    _ffn_step(j, pl.num_programs(1), x_buf, gain_ref,
              lambda: (wg_ref[...], wu_ref[...], wo_ref[...]), fgain_ref, o_ref, n_ref, final_norm,
              last_weights=last_weights)


def _cast_row_block(rows, steps):
    for rb in range(BF16_SUBLANES, rows + 1, BF16_SUBLANES):
        if rows % rb == 0 and rows // rb <= steps:
            return rb
    raise ValueError(f"cannot cast {rows} rows in {steps} steps")


def _ffn_first(x, gain, w_in, w_out, next_w_in, next_w_out, *, tm=1024, tf_head=256, tf_tail=768,
               copy_rows=128):
    t, d = x.shape
    d_ff = w_out.shape[0]
    assert t % tm == 0 and t // tm >= 2 and w_in.shape == (d, 2 * d_ff)
    assert d_ff % tf_head == 0
    nj = d_ff // tf_head
    vec = pl.BlockSpec((1, d), lambda j: (0, 0))
    h_head, wg_bf, wu_bf, wo_bf = pl.pallas_call(
        _ffn_head_body,
        grid=(nj,),
        in_specs=[
            pl.BlockSpec((tm, d), lambda j: (0, 0), pipeline_mode=pl.Buffered(1)),
            vec,
            pl.BlockSpec(memory_space=pl.ANY),
            pl.BlockSpec(memory_space=pl.ANY),
        ],
        out_specs=[
            pl.BlockSpec((tm, d), lambda j: (0, 0)),
            pl.BlockSpec((d, tf_head), lambda j: (0, j)),
            pl.BlockSpec((d, tf_head), lambda j: (0, j)),
            pl.BlockSpec((tf_head, d), lambda j: (j, 0)),
        ],
        out_shape=[
            jax.ShapeDtypeStruct((tm, d), F32),
            jax.ShapeDtypeStruct((d, d_ff), BF16),
            jax.ShapeDtypeStruct((d, d_ff), BF16),
            jax.ShapeDtypeStruct((d_ff, d), BF16),
        ],
        scratch_shapes=[
            pltpu.VMEM((tm, d), BF16),
            pltpu.VMEM((HEAD_WEIGHT_SLOTS, d, tf_head), F32),
            pltpu.VMEM((HEAD_WEIGHT_SLOTS, d, tf_head), F32),
            pltpu.VMEM((HEAD_WEIGHT_SLOTS, tf_head, d), F32),
            pltpu.SemaphoreType.DMA((3, HEAD_WEIGHT_SLOTS)),
        ],
        compiler_params=_params(("arbitrary",)),
        name="ffn_head",
    )(x, gain, w_in, w_out)

    n_tiles = t // tm
    nj_tail = pl.cdiv(d_ff, tf_tail)
    last_width = d_ff - (nj_tail - 1) * tf_tail
    n_copy = tm // copy_rows
    assert tm % copy_rows == 0 and n_copy <= nj_tail and nj_tail >= 2 and last_width % LANES == 0
    steps = n_tiles * nj_tail
    nd, nd_ff2 = next_w_in.shape
    nd_ff = nd_ff2 // 2
    assert next_w_out.shape == (nd_ff, nd) and nd_ff % LANES == 0
    rb_in = _cast_row_block(nd, steps)
    rb_out = _cast_row_block(nd_ff, steps)
    vec2 = pl.BlockSpec((1, d), lambda i, j: (0, 0))
    wcol = lambda i, j: (0, j * jnp.minimum(i, 1))
    wrow = lambda i, j: (j * jnp.minimum(i, 1), 0)
    head_rows = lambda i, j: (jnp.where(i == 0, jnp.minimum(j, n_copy - 1), n_copy - 1), 0)
    cast_in = lambda i, j: (jnp.minimum(i * nj_tail + j, nd // rb_in - 1), 0)
    cast_out = lambda i, j: (jnp.minimum(i * nj_tail + j, nd_ff // rb_out - 1), 0)
    return pl.pallas_call(
        functools.partial(_ffn_tail_body, n_copy=n_copy,
                          last_width=None if last_width == tf_tail else last_width),
        grid=(n_tiles, nj_tail),
        in_specs=[
            pl.BlockSpec(memory_space=pl.ANY),
            vec2,
            pl.BlockSpec((d, tf_tail), wcol),
            pl.BlockSpec((d, tf_tail), wcol),
            pl.BlockSpec((tf_tail, d), wrow),
            pl.BlockSpec((copy_rows, d), head_rows, pipeline_mode=pl.Buffered(1)),
            pl.BlockSpec((rb_in, nd_ff2), cast_in),
            pl.BlockSpec((rb_out, nd), cast_out),
        ],
        out_specs=[
            pl.BlockSpec((tm, d), lambda i, j: (i, 0)),
            pl.BlockSpec((rb_in, nd_ff), cast_in),
            pl.BlockSpec((rb_in, nd_ff), cast_in),
            pl.BlockSpec((rb_out, nd), cast_out),
        ],
        out_shape=[
            jax.ShapeDtypeStruct((t, d), F32),
            jax.ShapeDtypeStruct((nd, nd_ff), BF16),
            jax.ShapeDtypeStruct((nd, nd_ff), BF16),
            jax.ShapeDtypeStruct((nd_ff, nd), BF16),
        ],
        scratch_shapes=[
            pltpu.VMEM((tm, d), BF16),
            pltpu.VMEM((tm, d), F32),
            pltpu.SemaphoreType.DMA(()),
        ],
        compiler_params=_params(("arbitrary", "arbitrary"), VMEM_LIMIT_TAIL_BYTES),
        name="ffn_tail",
    )(x, gain, wg_bf, wu_bf, wo_bf, h_head, next_w_in, next_w_out)


def _ffn_second(x, gain, wg_bf, wu_bf, wo_bf, final_gain, *, final_norm, tm=1024, tf=768):
    t, d = x.shape
    d_ff = wo_bf.shape[0]
    nj = pl.cdiv(d_ff, tf)
    last_width = d_ff - (nj - 1) * tf
    assert t % tm == 0 and nj >= 2 and wg_bf.shape == wu_bf.shape == (d, d_ff)
    assert last_width % LANES == 0
    vec = pl.BlockSpec((1, d), lambda i, j: (0, 0))
    return pl.pallas_call(
        functools.partial(_ffn_bf16_body, final_norm=final_norm,
                          last_width=None if last_width == tf else last_width),
        grid=(t // tm, nj),
        in_specs=[
            pl.BlockSpec(memory_space=pl.ANY),
            vec,
            pl.BlockSpec((d, tf), lambda i, j: (0, j)),
            pl.BlockSpec((d, tf), lambda i, j: (0, j)),
            pl.BlockSpec((tf, d), lambda i, j: (j, 0)),
            vec,
        ],
        out_specs=pl.BlockSpec((tm, d), lambda i, j: (i, 0)),
        out_shape=jax.ShapeDtypeStruct((t, d), F32),
        scratch_shapes=[
            pltpu.VMEM((tm, d), BF16),
            pltpu.VMEM((tm, d), F32),
            pltpu.SemaphoreType.DMA(()),
        ],
        compiler_params=_params(("arbitrary", "arbitrary")),
        name="ffn_second_final" if final_norm else "ffn_second",
    )(x, gain, wg_bf, wu_bf, wo_bf, final_gain)


def _pool_group(x, halo, window, w, scale, t_first):
    tt = x.shape[0]
    ext = jnp.concatenate([halo, x], axis=0)
    row = lax.broadcasted_iota(jnp.int32, ext.shape, 0)
    acc = ext
    span = 1
    while span < window:
        acc = acc + jnp.where(row >= span, pltpu.roll(acc, span, axis=0), 0.0)
        span *= 2
    cnt = jnp.clip(row + (t_first + 1 - POOL_HALO), 1, window).astype(F32)
    pooled = (acc / cnt - ext)[POOL_HALO:, :].astype(BF16)
    return _dot(pooled, w) * scale


def _mixer_body(h_ref, gain_ref, w_ref, wgate_ref, wpool_ref, pscale_ref, wa_ref, ba_ref, gn_ref,
                y_ref,
                win_s, qd_s, ki_s, kt_s, bc_s, v_s, sg_s, st_s, halo_s,
                *, n_cast, nt, d_pool, dk, dv):
    s = pl.program_id(0)
    tt = h_ref.shape[0]
    wc = w_ref.shape[0]
    heads = GLA_HEADS
    group_dim = d_pool // N_POOL_GROUPS
    pool_chunks = d_pool // wc
    q_chunk = pool_chunks
    k_chunk = q_chunk + 1
    v_chunk0 = k_chunk + 1
    v_chunks = heads * dv // wc
    g_chunk0 = v_chunk0 + v_chunks

    @pl.when(s < n_cast)
    def _cast_weights():
        win_s[s] = w_ref[...].astype(BF16)

    @pl.when(s >= n_cast)
    def _compute():
        t = lax.rem(s - n_cast, nt)

        @pl.when(t == 0)
        def _reset():
            st_s[...] = jnp.zeros_like(st_s)
            halo_s[...] = jnp.zeros_like(halo_s)

        n = _rmsnorm(h_ref[...], gain_ref[...]).astype(BF16)

        per_chunk = wc // group_dim
        for c in range(pool_chunks):
            u_c = _dot_nt(n, win_s[c])
            for part in range(per_chunk):
                gi = c * per_chunk + part
                cols = slice(gi * group_dim, (gi + 1) * group_dim)
                x = u_c[:, part * group_dim:(part + 1) * group_dim]
                y = _pool_group(x, halo_s[:, cols], POOL_WINDOWS[gi], wpool_ref[gi].astype(BF16),
                                pscale_ref[:, cols], t * tt)
                halo_s[:, cols] = x[tt - POOL_HALO:, :]
                y_ref[:, cols] = y.astype(y_ref.dtype)

        gate = _dot_nt(n, wgate_ref[...])
        logits = _dot(gate.astype(BF16), wa_ref[...]) + ba_ref[...]
        log_alpha = _log_sigmoid(logits) * (1.0 / GATE_LOGIT_NORMALIZER)
        pos = lax.broadcasted_iota(jnp.int32, log_alpha.shape, 0) & (CHUNK - 1)
        bcum = log_alpha
        span = 1
        while span < CHUNK:
            bcum = bcum + jnp.where(pos >= span, pltpu.roll(bcum, span, axis=0), 0.0)
            span *= 2
        bc_s[...] = bcum
        n_chunks = tt // CHUNK
        b3 = bcum.reshape(n_chunks, CHUNK, heads * dk)
        b_last = jnp.broadcast_to(b3[:, CHUNK - 1:CHUNK, :], b3.shape).reshape(tt, heads * dk)
        q = _dot_nt(n, win_s[q_chunk])
        qd_s[...] = (q * (dk ** -0.5) * jnp.exp(bcum)).astype(BF16)
        k = _dot_nt(n, win_s[k_chunk])
        ki_s[...] = (k * jnp.exp(-bcum)).astype(BF16)
        kt_s[...] = (k * jnp.exp(b_last - bcum)).astype(BF16)
        for c in range(v_chunks):
            cols = slice(c * wc, (c + 1) * wc)
            v_s[:, cols] = _dot_nt(n, win_s[v_chunk0 + c]).astype(BF16)
            sg_s[:, cols] = _silu(_dot_nt(n, win_s[g_chunk0 + c]))

        sb_rows = SB_CHUNKS * CHUNK
        r_idx = lax.broadcasted_iota(jnp.int32, (sb_rows, sb_rows), 0)
        c_idx = lax.broadcasted_iota(jnp.int32, (sb_rows, sb_rows), 1)
        same_chunk_causal = ((r_idx // CHUNK) == (c_idx // CHUNK)) & (r_idx >= c_idx)

        def superblock(sb, carry):
            row0 = pl.multiple_of(sb * sb_rows, sb_rows)
            rows = pl.ds(row0, sb_rows)
            decays = [jnp.exp(bc_s[pl.ds(row0 + (c * CHUNK + CHUNK - 1), 1), :]) for c in range(SB_CHUNKS)]
            for hd in range(heads):
                kc = slice(hd * dk, (hd + 1) * dk)
                vc = slice(hd * dv, (hd + 1) * dv)
                qd = qd_s[rows, kc]
                kt = kt_s[rows, kc]
                v = v_s[rows, vc]
                scores = jnp.where(same_chunk_causal, _dot_nt(qd, ki_s[rows, kc]), 0.0).astype(BF16)
                o = _dot(scores, v)
                state_t = st_s[hd]
                inter = []
                for c in range(SB_CHUNKS):
                    cr = slice(c * CHUNK, (c + 1) * CHUNK)
                    inter.append(_dot_nt(qd[cr], state_t.astype(BF16)))
                    state_t = state_t * decays[c][:, kc] + _dot_tn(v[cr], kt[cr])
                st_s[hd] = state_t
                o = _rmsnorm(o + jnp.concatenate(inter, axis=0), gn_ref[...])
                y_ref[rows, d_pool + hd * dv:d_pool + (hd + 1) * dv] = (o * sg_s[rows, vc]).astype(y_ref.dtype)
            return carry

        lax.fori_loop(0, tt // sb_rows, superblock, 0)


def _mixer(h, gain, w_in_t, w_gate_t, w_pool, pool_scale, w_alpha, b_alpha, gla_norm, *,
           seq, d_pool, dk, dv, tt=512, wc=512):
    t, d = h.shape
    heads = GLA_HEADS
    d_gla = heads * dv
    n_main = d_pool + 2 * heads * dk + 2 * d_gla
    n_cast = n_main // wc
    nt = seq // tt
    n_tiles = t // tt
    group_dim = d_pool // N_POOL_GROUPS
    assert seq % tt == 0 and tt % (SB_CHUNKS * CHUNK) == 0 and n_main % wc == 0 and heads * dk == wc
    assert d_pool % wc == 0 and d_gla % wc == 0 and wc % group_dim == 0
    tile = lambda s: jnp.maximum(s - n_cast, 0)
    const = lambda s: (0, 0)
    return pl.pallas_call(
        functools.partial(_mixer_body, n_cast=n_cast, nt=nt, d_pool=d_pool, dk=dk, dv=dv),
        grid=(n_cast + n_tiles,),
        in_specs=[
            pl.BlockSpec((tt, d), lambda s: (tile(s), 0)),
            pl.BlockSpec((1, d), const),
            pl.BlockSpec((wc, d), lambda s: (jnp.minimum(s, n_cast - 1), 0)),
            pl.BlockSpec((LANES, d), const),
            pl.BlockSpec((N_POOL_GROUPS, group_dim, group_dim), lambda s: (0, 0, 0)),
            pl.BlockSpec((1, d_pool), const),
            pl.BlockSpec((LANES, heads * dk), const),
            pl.BlockSpec((1, heads * dk), const),
            pl.BlockSpec((1, dv), const),
        ],
        out_specs=pl.BlockSpec((tt, d_pool + d_gla), lambda s: (tile(s), 0)),
        out_shape=jax.ShapeDtypeStruct((t, d_pool + d_gla), BF16),
        scratch_shapes=[
            pltpu.VMEM((n_cast, wc, d), BF16),
            pltpu.VMEM((tt, heads * dk), BF16),
            pltpu.VMEM((tt, heads * dk), BF16),
            pltpu.VMEM((tt, heads * dk), BF16),
            pltpu.VMEM((tt, heads * dk), F32),
            pltpu.VMEM((tt, d_gla), BF16),
            pltpu.VMEM((tt, d_gla), F32),
            pltpu.VMEM((heads, dv, dk), F32),
            pltpu.VMEM((POOL_HALO, d_pool), F32),
        ],
        compiler_params=_params(("arbitrary",)),
        name="mixer",
    )(h, gain, w_in_t, w_gate_t, w_pool, pool_scale, w_alpha, b_alpha, gla_norm)


def _mix_out_body(h_ref, y_ref, w_ref, o_ref, w_bf_s):
    @pl.when(pl.program_id(0) == 0)
    def _cast_weights():
        w_bf_s[...] = w_ref[...].astype(BF16)

    o_ref[...] = h_ref[...] + _dot(y_ref[...], w_bf_s[...])


def _mix_out(h, y, w_out, *, tm=512):
    t, d = h.shape
    d_mix = y.shape[1]
    assert t % tm == 0 and w_out.shape == (d_mix, d)
    return pl.pallas_call(
        _mix_out_body,
        grid=(t // tm,),
        in_specs=[
            pl.BlockSpec((tm, d), lambda i: (i, 0)),
            pl.BlockSpec((tm, d_mix), lambda i: (i, 0)),
            pl.BlockSpec((d_mix, d), lambda i: (0, 0), pipeline_mode=pl.Buffered(1)),
        ],
        out_specs=pl.BlockSpec((tm, d), lambda i: (i, 0)),
        out_shape=jax.ShapeDtypeStruct((t, d), F32),
        scratch_shapes=[pltpu.VMEM((d_mix, d), BF16)],
        compiler_params=_params(("arbitrary",)),
        name="mix_out",
    )(h, y, w_out)


def kernel(x, ffn1_norm, ffn1_w_in, ffn1_w_out, mix_norm, w_in_mix, w_pool, pool_scale,
           w_alpha, b_alpha, gla_norm, w_out_mix, ffn2_norm, ffn2_w_in, ffn2_w_out, final_norm):
    b, s, d = x.shape
    depth = ffn1_norm.shape[0]
    d_pool = pool_scale.shape[1]
    dk_total = w_alpha.shape[2]
    dk = dk_total // GLA_HEADS
    dv = gla_norm.shape[1]
    d_gla = GLA_HEADS * dv
    n_main = d_pool + 2 * dk_total + 2 * d_gla
    assert w_in_mix.shape[2] == n_main + GLA_GATE_RANK and s % CHUNK == 0 and depth >= 1

    h = x.reshape(b * s, d)
    final_gain = final_norm.reshape(1, d)
    for l in range(depth):
        last = l == depth - 1
        h, wg2_bf, wu2_bf, wo2_bf = _ffn_first(h, ffn1_norm[l].reshape(1, d), ffn1_w_in[l], ffn1_w_out[l],
                                            ffn2_w_in[l], ffn2_w_out[l])

        w_mix_t = jnp.swapaxes(w_in_mix[l], 0, 1)
        w_gate_t = jnp.pad(w_mix_t[n_main:, :], ((0, LANES - GLA_GATE_RANK), (0, 0))).astype(BF16)
        w_alpha_pad = jnp.pad(w_alpha[l], ((0, LANES - GLA_GATE_RANK), (0, 0))).astype(BF16)
        y = _mixer(h, mix_norm[l].reshape(1, d), w_mix_t, w_gate_t, w_pool[l],
                   pool_scale[l].reshape(1, d_pool), w_alpha_pad, b_alpha[l].reshape(1, dk_total),
                   gla_norm[l].reshape(1, dv), seq=s, d_pool=d_pool, dk=dk, dv=dv)
        h = _mix_out(h, y, w_out_mix[l])

        h = _ffn_second(h, ffn2_norm[l].reshape(1, d), wg2_bf, wu2_bf, wo2_bf, final_gain,
                        final_norm=last)
    return h.reshape(b, s, d)
```

```python
import functools

import jax
import jax.numpy as jnp
from jax import lax
from jax.experimental import pallas as pl
from jax.experimental.pallas import tpu as pltpu

F32 = jnp.float32
BF16 = jnp.bfloat16

EPS = 1e-6
POOL_WINDOWS = (2, 4, 8, 16)
N_POOL_GROUPS = len(POOL_WINDOWS)
POOL_HALO = 16
GLA_HEADS = 4
GLA_GATE_RANK = 16
GATE_LOGIT_NORMALIZER = 16.0
CHUNK = 64
SB_CHUNKS = 4
HEAD_WEIGHT_SLOTS = 3

LANES = 128
BF16_SUBLANES = 16
VMEM_LIMIT_BYTES = 58 * 1024 * 1024
VMEM_LIMIT_WIDE_BYTES = 62 * 1024 * 1024


def _rmsnorm(x, gain):
    ms = jnp.mean(x * x, axis=-1, keepdims=True)
    return x * lax.rsqrt(ms + EPS) * gain


def _silu(x):
    return x * (1.0 / (1.0 + jnp.exp(-x)))


def _log_sigmoid(z):
    return jnp.minimum(z, 0.0) - jnp.log1p(jnp.exp(-jnp.abs(z)))


def _dot(a, b):
    return jnp.dot(a, b, preferred_element_type=F32)


def _dot_nt(a, b):
    return lax.dot_general(a, b, (((1,), (1,)), ((), ())), preferred_element_type=F32)


def _dot_tn(a, b):
    return lax.dot_general(a, b, (((0,), (0,)), ((), ())), preferred_element_type=F32)


def _params(semantics, vmem_limit_bytes=VMEM_LIMIT_BYTES):
    return pltpu.CompilerParams(dimension_semantics=semantics, vmem_limit_bytes=vmem_limit_bytes)


def _ffn_step(j, nj, x_ref, gain_ref, weights, fgain_ref, o_ref, n_ref, final_norm,
              last_weights=None, sub_width=None):
    def slice_update(n, get_weights):
        wg, wu, wo = get_weights()
        width = wg.shape[1]
        sw = width if sub_width is None else min(sub_width, width)
        total = None
        for c0 in range(0, width, sw):
            gate = _dot(n, wg[:, c0:c0 + sw])
            up = _dot(n, wu[:, c0:c0 + sw])
            act = (0.5 * _silu(gate) * up).astype(BF16)
            part = _dot(act, wo[c0:c0 + sw, :])
            total = part if total is None else total + part
        return total

    @pl.when(j == 0)
    def _first():
        x = x_ref[...]
        n = _rmsnorm(x, gain_ref[...]).astype(BF16)
        n_ref[...] = n
        o_ref[...] = x + slice_update(n, weights)

    if last_weights is None:

        @pl.when(j > 0)
        def _rest():
            o_ref[...] += slice_update(n_ref[...], weights)

    else:

        @pl.when((j > 0) & (j < nj - 1))
        def _middle():
            o_ref[...] += slice_update(n_ref[...], weights)

        @pl.when(j == nj - 1)
        def _last():
            o_ref[...] += slice_update(n_ref[...], last_weights)

    if final_norm:

        @pl.when(j == nj - 1)
        def _finish():
            o_ref[...] = _rmsnorm(o_ref[...], fgain_ref[...])


def _ffn_head_body(x_ref, gain_ref, w_in_hbm, w_out_hbm,
                   o_ref, wg_bf_ref, wu_bf_ref, wo_bf_ref,
                   n_ref, wg_buf, wu_buf, wo_buf, sems):
    j = pl.program_id(0)
    nj = pl.num_programs(0)
    n_slots, _, tf = wg_buf.shape
    d_ff = w_out_hbm.shape[0]
    lookahead = n_slots - 1

    def slice_copies(step, slot):
        lo = pl.multiple_of(step * tf, tf)
        return (
            pltpu.make_async_copy(w_in_hbm.at[:, pl.ds(lo, tf)], wg_buf.at[slot], sems.at[0, slot]),
            pltpu.make_async_copy(w_in_hbm.at[:, pl.ds(d_ff + lo, tf)], wu_buf.at[slot], sems.at[1, slot]),
            pltpu.make_async_copy(w_out_hbm.at[pl.ds(lo, tf), :], wo_buf.at[slot], sems.at[2, slot]),
        )

    @pl.when(j == 0)
    def _prime():
        for step in range(lookahead):
            for c in slice_copies(step, step):
                c.start()

    @pl.when(j + lookahead < nj)
    def _prefetch():
        for c in slice_copies(j + lookahead, lax.rem(j + lookahead, n_slots)):
            c.start()

    slot = lax.rem(j, n_slots)
    for c in slice_copies(j, slot):
        c.wait()

    def weights():
        wg = wg_buf[slot].astype(BF16)
        wu = wu_buf[slot].astype(BF16)
        wo = wo_buf[slot].astype(BF16)
        wg_bf_ref[...] = wg
        wu_bf_ref[...] = wu
        wo_bf_ref[...] = wo
        return wg, wu, wo

    _ffn_step(j, nj, x_ref, gain_ref, weights, None, o_ref, n_ref, False)


def _ffn_tail_body(x_hbm, gain_ref, wg_ref, wu_ref, wo_ref, head_ref, nw_in_ref, nw_out_ref,
                   o_ref, nwg_bf_ref, nwu_bf_ref, nwo_bf_ref,
                   n_ref, x_buf, x_sem, *, n_copy):
    i = pl.program_id(0)
    j = pl.program_id(1)
    n_tiles = pl.num_programs(0)
    tm = x_buf.shape[0]
    rc = head_ref.shape[0]

    nd_ff = nwg_bf_ref.shape[1]
    nwg_bf_ref[...] = nw_in_ref[:, :nd_ff].astype(BF16)
    nwu_bf_ref[...] = nw_in_ref[:, nd_ff:].astype(BF16)
    nwo_bf_ref[...] = nw_out_ref[...].astype(BF16)

    def x_copy(tile):
        rows = pl.ds(pl.multiple_of(tile * tm, tm), tm)
        return pltpu.make_async_copy(x_hbm.at[rows, :], x_buf, x_sem)

    @pl.when((j == 1) & (i + 1 < n_tiles))
    def _prefetch_x():
        x_copy(i + 1).start()

    @pl.when((i == 0) & (j < n_copy))
    def _copy_head_rows():
        o_ref[pl.ds(pl.multiple_of(j * rc, rc), rc), :] = head_ref[...]

    @pl.when((i > 0) & (j == 0))
    def _wait_x():
        x_copy(i).wait()

    @pl.when(i > 0)
    def _compute():
        _ffn_step(j, pl.num_programs(1), x_buf, gain_ref,
                  lambda: (wg_ref[...], wu_ref[...], wo_ref[...]), None, o_ref, n_ref, False)


def _ffn_bf16_body(x_hbm, gain_ref, wg_ref, wu_ref, wo_ref, fgain_ref, o_ref, n_ref, x_buf, x_sem,
                   *, final_norm, last_width, sub_width):
    i = pl.program_id(0)
    j = pl.program_id(1)
    tm = x_buf.shape[0]

    def x_copy(tile):
        rows = pl.ds(pl.multiple_of(tile * tm, tm), tm)
        return pltpu.make_async_copy(x_hbm.at[rows, :], x_buf, x_sem)

    @pl.when((i == 0) & (j == 0))
    def _first_x():
        x_copy(0).start()

    @pl.when((j == 1) & (i + 1 < pl.num_programs(0)))
    def _prefetch_x():
        x_copy(i + 1).start()

    @pl.when(j == 0)
    def _wait_x():
        x_copy(i).wait()

    last_weights = None
    if last_width is not None:
        last_weights = lambda: (wg_ref[:, :last_width], wu_ref[:, :last_width], wo_ref[:last_width, :])
    _ffn_step(j, pl.num_programs(1), x_buf, gain_ref,
              lambda: (wg_ref[...], wu_ref[...], wo_ref[...]), fgain_ref, o_ref, n_ref, final_norm,
              last_weights=last_weights, sub_width=sub_width)


def _cast_row_block(rows, steps):
    for rb in range(BF16_SUBLANES, rows + 1, BF16_SUBLANES):
        if rows % rb == 0 and rows // rb <= steps:
            return rb
    raise ValueError(f"cannot cast {rows} rows in {steps} steps")


def _ffn_first(x, gain, w_in, w_out, next_w_in, next_w_out, *, tm=1024, tf_head=256, tf_tail=512,
               copy_rows=128):
    t, d = x.shape
    d_ff = w_out.shape[0]
    assert t % tm == 0 and t // tm >= 2 and w_in.shape == (d, 2 * d_ff)
    assert d_ff % tf_head == 0 and d_ff % tf_tail == 0 and d_ff // tf_tail >= 2
    nj = d_ff // tf_head
    vec = pl.BlockSpec((1, d), lambda j: (0, 0))
    h_head, wg_bf, wu_bf, wo_bf = pl.pallas_call(
        _ffn_head_body,
        grid=(nj,),
        in_specs=[
            pl.BlockSpec((tm, d), lambda j: (0, 0), pipeline_mode=pl.Buffered(1)),
            vec,
            pl.BlockSpec(memory_space=pl.ANY),
            pl.BlockSpec(memory_space=pl.ANY),
        ],
        out_specs=[
            pl.BlockSpec((tm, d), lambda j: (0, 0)),
            pl.BlockSpec((d, tf_head), lambda j: (0, j)),
            pl.BlockSpec((d, tf_head), lambda j: (0, j)),
            pl.BlockSpec((tf_head, d), lambda j: (j, 0)),
        ],
        out_shape=[
            jax.ShapeDtypeStruct((tm, d), F32),
            jax.ShapeDtypeStruct((d, d_ff), BF16),
            jax.ShapeDtypeStruct((d, d_ff), BF16),
            jax.ShapeDtypeStruct((d_ff, d), BF16),
        ],
        scratch_shapes=[
            pltpu.VMEM((tm, d), BF16),
            pltpu.VMEM((HEAD_WEIGHT_SLOTS, d, tf_head), F32),
            pltpu.VMEM((HEAD_WEIGHT_SLOTS, d, tf_head), F32),
            pltpu.VMEM((HEAD_WEIGHT_SLOTS, tf_head, d), F32),
            pltpu.SemaphoreType.DMA((3, HEAD_WEIGHT_SLOTS)),
        ],
        compiler_params=_params(("arbitrary",)),
        name="ffn_head",
    )(x, gain, w_in, w_out)

    n_tiles = t // tm
    nj_tail = d_ff // tf_tail
    n_copy = tm // copy_rows
    assert tm % copy_rows == 0 and n_copy <= nj_tail
    steps = n_tiles * nj_tail
    nd, nd_ff2 = next_w_in.shape
    nd_ff = nd_ff2 // 2
    assert next_w_out.shape == (nd_ff, nd) and nd_ff % LANES == 0
    rb_in = _cast_row_block(nd, steps)
    rb_out = _cast_row_block(nd_ff, steps)
    vec2 = pl.BlockSpec((1, d), lambda i, j: (0, 0))
    wcol = lambda i, j: (0, j * jnp.minimum(i, 1))
    wrow = lambda i, j: (j * jnp.minimum(i, 1), 0)
    head_rows = lambda i, j: (jnp.where(i == 0, jnp.minimum(j, n_copy - 1), n_copy - 1), 0)
    cast_in = lambda i, j: (jnp.minimum(i * nj_tail + j, nd // rb_in - 1), 0)
    cast_out = lambda i, j: (jnp.minimum(i * nj_tail + j, nd_ff // rb_out - 1), 0)
    return pl.pallas_call(
        functools.partial(_ffn_tail_body, n_copy=n_copy),
        grid=(n_tiles, nj_tail),
        in_specs=[
            pl.BlockSpec(memory_space=pl.ANY),
            vec2,
            pl.BlockSpec((d, tf_tail), wcol),
            pl.BlockSpec((d, tf_tail), wcol),
            pl.BlockSpec((tf_tail, d), wrow),
            pl.BlockSpec((copy_rows, d), head_rows, pipeline_mode=pl.Buffered(1)),
            pl.BlockSpec((rb_in, nd_ff2), cast_in),
            pl.BlockSpec((rb_out, nd), cast_out),
        ],
        out_specs=[
            pl.BlockSpec((tm, d), lambda i, j: (i, 0)),
            pl.BlockSpec((rb_in, nd_ff), cast_in),
            pl.BlockSpec((rb_in, nd_ff), cast_in),
            pl.BlockSpec((rb_out, nd), cast_out),
        ],
        out_shape=[
            jax.ShapeDtypeStruct((t, d), F32),
            jax.ShapeDtypeStruct((nd, nd_ff), BF16),
            jax.ShapeDtypeStruct((nd, nd_ff), BF16),
            jax.ShapeDtypeStruct((nd_ff, nd), BF16),
        ],
        scratch_shapes=[
            pltpu.VMEM((tm, d), BF16),
            pltpu.VMEM((tm, d), F32),
            pltpu.SemaphoreType.DMA(()),
        ],
        compiler_params=_params(("arbitrary", "arbitrary")),
        name="ffn_tail",
    )(x, gain, wg_bf, wu_bf, wo_bf, h_head, next_w_in, next_w_out)


def _ffn_second(x, gain, wg_bf, wu_bf, wo_bf, final_gain, *, final_norm, tm=1024, tf=1024,
                sub_width=None):
    t, d = x.shape
    d_ff = wo_bf.shape[0]
    nj = pl.cdiv(d_ff, tf)
    last_width = d_ff - (nj - 1) * tf
    assert t % tm == 0 and nj >= 2 and wg_bf.shape == wu_bf.shape == (d, d_ff)
    assert last_width % LANES == 0
    vec = pl.BlockSpec((1, d), lambda i, j: (0, 0))
    return pl.pallas_call(
        functools.partial(_ffn_bf16_body, final_norm=final_norm, sub_width=sub_width,
                          last_width=None if last_width == tf else last_width),
        grid=(t // tm, nj),
        in_specs=[
            pl.BlockSpec(memory_space=pl.ANY),
            vec,
            pl.BlockSpec((d, tf), lambda i, j: (0, j)),
            pl.BlockSpec((d, tf), lambda i, j: (0, j)),
            pl.BlockSpec((tf, d), lambda i, j: (j, 0)),
            vec,
        ],
        out_specs=pl.BlockSpec((tm, d), lambda i, j: (i, 0)),
        out_shape=jax.ShapeDtypeStruct((t, d), F32),
        scratch_shapes=[
            pltpu.VMEM((tm, d), BF16),
            pltpu.VMEM((tm, d), F32),
            pltpu.SemaphoreType.DMA(()),
        ],
        compiler_params=_params(("arbitrary", "arbitrary"), VMEM_LIMIT_WIDE_BYTES),
        name="ffn_second_final" if final_norm else "ffn_second",
    )(x, gain, wg_bf, wu_bf, wo_bf, final_gain)


def _pool_group(x, halo, window, w, scale, t_first):
    tt = x.shape[0]
    ext = jnp.concatenate([halo, x], axis=0)
    row = lax.broadcasted_iota(jnp.int32, ext.shape, 0)
    acc = ext
    span = 1
    while span < window:
        acc = acc + jnp.where(row >= span, pltpu.roll(acc, span, axis=0), 0.0)
        span *= 2
    cnt = jnp.clip(row + (t_first + 1 - POOL_HALO), 1, window).astype(F32)
    pooled = (acc / cnt - ext)[POOL_HALO:, :].astype(BF16)
    return _dot(pooled, w) * scale


def _mixer_body(h_ref, gain_ref, w_ref, wgate_ref, wpool_ref, pscale_ref, wa_ref, ba_ref, gn_ref,
                y_ref,
                win_s, qd_s, ki_s, kt_s, bc_s, v_s, sg_s, st_s, halo_s,
                *, n_cast, nt, d_pool, dk, dv):
    s = pl.program_id(0)
    tt = h_ref.shape[0]
    wc = w_ref.shape[0]
    heads = GLA_HEADS
    group_dim = d_pool // N_POOL_GROUPS
    pool_chunks = d_pool // wc
    q_chunk = pool_chunks
    k_chunk = q_chunk + 1
    v_chunk0 = k_chunk + 1
    v_chunks = heads * dv // wc
    g_chunk0 = v_chunk0 + v_chunks

    @pl.when(s < n_cast)
    def _cast_weights():
        win_s[s] = w_ref[...].astype(BF16)

    @pl.when(s >= n_cast)
    def _compute():
        t = lax.rem(s - n_cast, nt)

        @pl.when(t == 0)
        def _reset():
            st_s[...] = jnp.zeros_like(st_s)
            halo_s[...] = jnp.zeros_like(halo_s)

        n = _rmsnorm(h_ref[...], gain_ref[...]).astype(BF16)

        per_chunk = wc // group_dim
        for c in range(pool_chunks):
            u_c = _dot_nt(n, win_s[c])
            for part in range(per_chunk):
                gi = c * per_chunk + part
                cols = slice(gi * group_dim, (gi + 1) * group_dim)
                x = u_c[:, part * group_dim:(part + 1) * group_dim]
                y = _pool_group(x, halo_s[:, cols], POOL_WINDOWS[gi], wpool_ref[gi].astype(BF16),
                                pscale_ref[:, cols], t * tt)
                halo_s[:, cols] = x[tt - POOL_HALO:, :]
                y_ref[:, cols] = y.astype(y_ref.dtype)

        gate = _dot_nt(n, wgate_ref[...])
        logits = _dot(gate.astype(BF16), wa_ref[...]) + ba_ref[...]
        log_alpha = _log_sigmoid(logits) * (1.0 / GATE_LOGIT_NORMALIZER)
        pos = lax.broadcasted_iota(jnp.int32, log_alpha.shape, 0) & (CHUNK - 1)
        bcum = log_alpha
        span = 1
        while span < CHUNK:
            bcum = bcum + jnp.where(pos >= span, pltpu.roll(bcum, span, axis=0), 0.0)
            span *= 2
        bc_s[...] = bcum
        n_chunks = tt // CHUNK
        b3 = bcum.reshape(n_chunks, CHUNK, heads * dk)
        b_last = jnp.broadcast_to(b3[:, CHUNK - 1:CHUNK, :], b3.shape).reshape(tt, heads * dk)
        q = _dot_nt(n, win_s[q_chunk])
        qd_s[...] = (q * (dk ** -0.5) * jnp.exp(bcum)).astype(BF16)
        k = _dot_nt(n, win_s[k_chunk])
        ki_s[...] = (k * jnp.exp(-bcum)).astype(BF16)
        kt_s[...] = (k * jnp.exp(b_last - bcum)).astype(BF16)
        for c in range(v_chunks):
            cols = slice(c * wc, (c + 1) * wc)
            v_s[:, cols] = _dot_nt(n, win_s[v_chunk0 + c]).astype(BF16)
            sg_s[:, cols] = _silu(_dot_nt(n, win_s[g_chunk0 + c]))

        sb_rows = SB_CHUNKS * CHUNK
        r_idx = lax.broadcasted_iota(jnp.int32, (sb_rows, sb_rows), 0)
        c_idx = lax.broadcasted_iota(jnp.int32, (sb_rows, sb_rows), 1)
        same_chunk_causal = ((r_idx // CHUNK) == (c_idx // CHUNK)) & (r_idx >= c_idx)

        def superblock(sb, carry):
            row0 = pl.multiple_of(sb * sb_rows, sb_rows)
            rows = pl.ds(row0, sb_rows)
            decays = [jnp.exp(bc_s[pl.ds(row0 + (c * CHUNK + CHUNK - 1), 1), :]) for c in range(SB_CHUNKS)]
            for hd in range(heads):
                kc = slice(hd * dk, (hd + 1) * dk)
                vc = slice(hd * dv, (hd + 1) * dv)
                qd = qd_s[rows, kc]
                kt = kt_s[rows, kc]
                v = v_s[rows, vc]
                scores = jnp.where(same_chunk_causal, _dot_nt(qd, ki_s[rows, kc]), 0.0).astype(BF16)
                o = _dot(scores, v)
                state_t = st_s[hd]
                inter = []
                for c in range(SB_CHUNKS):
                    cr = slice(c * CHUNK, (c + 1) * CHUNK)
                    inter.append(_dot_nt(qd[cr], state_t.astype(BF16)))
                    state_t = state_t * decays[c][:, kc] + _dot_tn(v[cr], kt[cr])
                st_s[hd] = state_t
                o = _rmsnorm(o + jnp.concatenate(inter, axis=0), gn_ref[...])
                y_ref[rows, d_pool + hd * dv:d_pool + (hd + 1) * dv] = (o * sg_s[rows, vc]).astype(y_ref.dtype)
            return carry

        lax.fori_loop(0, tt // sb_rows, superblock, 0)


def _mixer(h, gain, w_in_t, w_gate_t, w_pool, pool_scale, w_alpha, b_alpha, gla_norm, *,
           seq, d_pool, dk, dv, tt=512, wc=512):
    t, d = h.shape
    heads = GLA_HEADS
    d_gla = heads * dv
    n_main = d_pool + 2 * heads * dk + 2 * d_gla
    n_cast = n_main // wc
    nt = seq // tt
    n_tiles = t // tt
    group_dim = d_pool // N_POOL_GROUPS
    assert seq % tt == 0 and tt % (SB_CHUNKS * CHUNK) == 0 and n_main % wc == 0 and heads * dk == wc
    assert d_pool % wc == 0 and d_gla % wc == 0 and wc % group_dim == 0
    tile = lambda s: jnp.maximum(s - n_cast, 0)
    const = lambda s: (0, 0)
    return pl.pallas_call(
        functools.partial(_mixer_body, n_cast=n_cast, nt=nt, d_pool=d_pool, dk=dk, dv=dv),
        grid=(n_cast + n_tiles,),
        in_specs=[
            pl.BlockSpec((tt, d), lambda s: (tile(s), 0)),
            pl.BlockSpec((1, d), const),
            pl.BlockSpec((wc, d), lambda s: (jnp.minimum(s, n_cast - 1), 0)),
            pl.BlockSpec((LANES, d), const),
            pl.BlockSpec((N_POOL_GROUPS, group_dim, group_dim), lambda s: (0, 0, 0)),
            pl.BlockSpec((1, d_pool), const),
            pl.BlockSpec((LANES, heads * dk), const),
            pl.BlockSpec((1, heads * dk), const),
            pl.BlockSpec((1, dv), const),
        ],
        out_specs=pl.BlockSpec((tt, d_pool + d_gla), lambda s: (tile(s), 0)),
        out_shape=jax.ShapeDtypeStruct((t, d_pool + d_gla), BF16),
        scratch_shapes=[
            pltpu.VMEM((n_cast, wc, d), BF16),
            pltpu.VMEM((tt, heads * dk), BF16),
            pltpu.VMEM((tt, heads * dk), BF16),
            pltpu.VMEM((tt, heads * dk), BF16),
            pltpu.VMEM((tt, heads * dk), F32),
            pltpu.VMEM((tt, d_gla), BF16),
            pltpu.VMEM((tt, d_gla), F32),
            pltpu.VMEM((heads, dv, dk), F32),
            pltpu.VMEM((POOL_HALO, d_pool), F32),
        ],
        compiler_params=_params(("arbitrary",)),
        name="mixer",
    )(h, gain, w_in_t, w_gate_t, w_pool, pool_scale, w_alpha, b_alpha, gla_norm)


def _mix_out_body(h_ref, y_ref, w_ref, o_ref, w_bf_s):
    @pl.when(pl.program_id(0) == 0)
    def _cast_weights():
        w_bf_s[...] = w_ref[...].astype(BF16)

    o_ref[...] = h_ref[...] + _dot(y_ref[...], w_bf_s[...])


def _mix_out(h, y, w_out, *, tm=512):
    t, d = h.shape
    d_mix = y.shape[1]
    assert t % tm == 0 and w_out.shape == (d_mix, d)
    return pl.pallas_call(
        _mix_out_body,
        grid=(t // tm,),
        in_specs=[
            pl.BlockSpec((tm, d), lambda i: (i, 0)),
            pl.BlockSpec((tm, d_mix), lambda i: (i, 0)),
            pl.BlockSpec((d_mix, d), lambda i: (0, 0), pipeline_mode=pl.Buffered(1)),
        ],
        out_specs=pl.BlockSpec((tm, d), lambda i: (i, 0)),
        out_shape=jax.ShapeDtypeStruct((t, d), F32),
        scratch_shapes=[pltpu.VMEM((d_mix, d), BF16)],
        compiler_params=_params(("arbitrary",)),
        name="mix_out",
    )(h, y, w_out)


def kernel(x, ffn1_norm, ffn1_w_in, ffn1_w_out, mix_norm, w_in_mix, w_pool, pool_scale,
           w_alpha, b_alpha, gla_norm, w_out_mix, ffn2_norm, ffn2_w_in, ffn2_w_out, final_norm):
    b, s, d = x.shape
    depth = ffn1_norm.shape[0]
    d_pool = pool_scale.shape[1]
    dk_total = w_alpha.shape[2]
    dk = dk_total // GLA_HEADS
    dv = gla_norm.shape[1]
    d_gla = GLA_HEADS * dv
    n_main = d_pool + 2 * dk_total + 2 * d_gla
    assert w_in_mix.shape[2] == n_main + GLA_GATE_RANK and s % CHUNK == 0 and depth >= 1

    h = x.reshape(b * s, d)
    final_gain = final_norm.reshape(1, d)
    for l in range(depth):
        last = l == depth - 1
        h, wg2_bf, wu2_bf, wo2_bf = _ffn_first(h, ffn1_norm[l].reshape(1, d), ffn1_w_in[l], ffn1_w_out[l],
                                            ffn2_w_in[l], ffn2_w_out[l])

        w_mix_t = jnp.swapaxes(w_in_mix[l], 0, 1)
        w_gate_t = jnp.pad(w_mix_t[n_main:, :], ((0, LANES - GLA_GATE_RANK), (0, 0))).astype(BF16)
        w_alpha_pad = jnp.pad(w_alpha[l], ((0, LANES - GLA_GATE_RANK), (0, 0))).astype(BF16)
        y = _mixer(h, mix_norm[l].reshape(1, d), w_mix_t, w_gate_t, w_pool[l],
                   pool_scale[l].reshape(1, d_pool), w_alpha_pad, b_alpha[l].reshape(1, dk_total),
                   gla_norm[l].reshape(1, dv), seq=s, d_pool=d_pool, dk=dk, dv=dv)
        h = _mix_out(h, y, w_out_mix[l])

        h = _ffn_second(h, ffn2_norm[l].reshape(1, d), wg2_bf, wu2_bf, wo2_bf, final_gain,
                        final_norm=last)
    return h.reshape(b, s, d)
```

```python
import functools

import jax
import jax.numpy as jnp
from jax import lax
from jax.experimental import pallas as pl
from jax.experimental.pallas import tpu as pltpu

F32 = jnp.float32
BF16 = jnp.bfloat16

EPS = 1e-6
POOL_WINDOWS = (2, 4, 8, 16)
N_POOL_GROUPS = len(POOL_WINDOWS)
POOL_HALO = 16
GLA_HEADS = 4
GLA_GATE_RANK = 16
GATE_LOGIT_NORMALIZER = 16.0
CHUNK = 64
SB_CHUNKS = 4
HEAD_WEIGHT_SLOTS = 3

LANES = 128
BF16_SUBLANES = 16
VMEM_LIMIT_BYTES = 58 * 1024 * 1024


def _rmsnorm(x, gain):
    ms = jnp.mean(x * x, axis=-1, keepdims=True)
    return x * lax.rsqrt(ms + EPS) * gain


def _silu(x):
    return x * (1.0 / (1.0 + jnp.exp(-x)))


def _log_sigmoid(z):
    return jnp.minimum(z, 0.0) - jnp.log1p(jnp.exp(-jnp.abs(z)))


def _dot(a, b):
    return jnp.dot(a, b, preferred_element_type=F32)


def _dot_nt(a, b):
    return lax.dot_general(a, b, (((1,), (1,)), ((), ())), preferred_element_type=F32)


def _dot_tn(a, b):
    return lax.dot_general(a, b, (((0,), (0,)), ((), ())), preferred_element_type=F32)


def _params(semantics):
    return pltpu.CompilerParams(dimension_semantics=semantics, vmem_limit_bytes=VMEM_LIMIT_BYTES)


def _ffn_step(j, nj, x_ref, gain_ref, weights, fgain_ref, o_ref, n_ref, final_norm,
              last_weights=None):
    def slice_update(n, get_weights):
        wg, wu, wo = get_weights()
        gate = _dot(n, wg)
        up = _dot(n, wu)
        act = (0.5 * _silu(gate) * up).astype(BF16)
        return _dot(act, wo)

    @pl.when(j == 0)
    def _first():
        x = x_ref[...]
        n = _rmsnorm(x, gain_ref[...]).astype(BF16)
        n_ref[...] = n
        o_ref[...] = x + slice_update(n, weights)

    if last_weights is None:

        @pl.when(j > 0)
        def _rest():
            o_ref[...] += slice_update(n_ref[...], weights)

    else:

        @pl.when((j > 0) & (j < nj - 1))
        def _middle():
            o_ref[...] += slice_update(n_ref[...], weights)

        @pl.when(j == nj - 1)
        def _last():
            o_ref[...] += slice_update(n_ref[...], last_weights)

    if final_norm:

        @pl.when(j == nj - 1)
        def _finish():
            o_ref[...] = _rmsnorm(o_ref[...], fgain_ref[...])


def _ffn_head_body(x_ref, gain_ref, w_in_hbm, w_out_hbm,
                   o_ref, wg_bf_ref, wu_bf_ref, wo_bf_ref,
                   n_ref, wg_buf, wu_buf, wo_buf, sems):
    j = pl.program_id(0)
    nj = pl.num_programs(0)
    n_slots, _, tf = wg_buf.shape
    d_ff = w_out_hbm.shape[0]
    lookahead = n_slots - 1

    def slice_copies(step, slot):
        lo = pl.multiple_of(step * tf, tf)
        return (
            pltpu.make_async_copy(w_in_hbm.at[:, pl.ds(lo, tf)], wg_buf.at[slot], sems.at[0, slot]),
            pltpu.make_async_copy(w_in_hbm.at[:, pl.ds(d_ff + lo, tf)], wu_buf.at[slot], sems.at[1, slot]),
            pltpu.make_async_copy(w_out_hbm.at[pl.ds(lo, tf), :], wo_buf.at[slot], sems.at[2, slot]),
        )

    @pl.when(j == 0)
    def _prime():
        for step in range(lookahead):
            for c in slice_copies(step, step):
                c.start()

    @pl.when(j + lookahead < nj)
    def _prefetch():
        for c in slice_copies(j + lookahead, lax.rem(j + lookahead, n_slots)):
            c.start()

    slot = lax.rem(j, n_slots)
    for c in slice_copies(j, slot):
        c.wait()

    def weights():
        wg = wg_buf[slot].astype(BF16)
        wu = wu_buf[slot].astype(BF16)
        wo = wo_buf[slot].astype(BF16)
        wg_bf_ref[...] = wg
        wu_bf_ref[...] = wu
        wo_bf_ref[...] = wo
        return wg, wu, wo

    _ffn_step(j, nj, x_ref, gain_ref, weights, None, o_ref, n_ref, False)


def _ffn_tail_body(*refs, n_copy, n_casts):
    x_hbm, gain_ref, wg_ref, wu_ref, wo_ref, head_ref, nw_in_ref = refs[:7]
    cast_in_refs = refs[7:7 + n_casts]
    o_ref, nwg_bf_ref, nwu_bf_ref = refs[7 + n_casts:10 + n_casts]
    cast_out_refs = refs[10 + n_casts:10 + 2 * n_casts]
    n_ref, x_buf, x_sem = refs[10 + 2 * n_casts:]
    i = pl.program_id(0)
    j = pl.program_id(1)
    n_tiles = pl.num_programs(0)
    tm = x_buf.shape[0]
    rc = head_ref.shape[0]

    nd_ff = nwg_bf_ref.shape[1]
    nwg_bf_ref[...] = nw_in_ref[:, :nd_ff].astype(BF16)
    nwu_bf_ref[...] = nw_in_ref[:, nd_ff:].astype(BF16)
    for src_ref, dst_ref in zip(cast_in_refs, cast_out_refs):
        dst_ref[...] = src_ref[...].astype(BF16)

    def x_copy(tile):
        rows = pl.ds(pl.multiple_of(tile * tm, tm), tm)
        return pltpu.make_async_copy(x_hbm.at[rows, :], x_buf, x_sem)

    @pl.when((j == 1) & (i + 1 < n_tiles))
    def _prefetch_x():
        x_copy(i + 1).start()

    @pl.when((i == 0) & (j < n_copy))
    def _copy_head_rows():
        o_ref[pl.ds(pl.multiple_of(j * rc, rc), rc), :] = head_ref[...]

    @pl.when((i > 0) & (j == 0))
    def _wait_x():
        x_copy(i).wait()

    @pl.when(i > 0)
    def _compute():
        _ffn_step(j, pl.num_programs(1), x_buf, gain_ref,
                  lambda: (wg_ref[...], wu_ref[...], wo_ref[...]), None, o_ref, n_ref, False)


def _ffn_bf16_body(x_hbm, gain_ref, wg_ref, wu_ref, wo_ref, fgain_ref, o_ref, n_ref, x_buf, x_sem,
                   *, final_norm, last_width):
    i = pl.program_id(0)
    j = pl.program_id(1)
    tm = x_buf.shape[0]

    def x_copy(tile):
        rows = pl.ds(pl.multiple_of(tile * tm, tm), tm)
        return pltpu.make_async_copy(x_hbm.at[rows, :], x_buf, x_sem)

    @pl.when((i == 0) & (j == 0))
    def _first_x():
        x_copy(0).start()

    @pl.when((j == 1) & (i + 1 < pl.num_programs(0)))
    def _prefetch_x():
        x_copy(i + 1).start()

    @pl.when(j == 0)
    def _wait_x():
        x_copy(i).wait()

    last_weights = None
    if last_width is not None:
        last_weights = lambda: (wg_ref[:, :last_width], wu_ref[:, :last_width], wo_ref[:last_width, :])
    _ffn_step(j, pl.num_programs(1), x_buf, gain_ref,
              lambda: (wg_ref[...], wu_ref[...], wo_ref[...]), fgain_ref, o_ref, n_ref, final_norm,
              last_weights=last_weights)


def _cast_row_block(rows, steps):
    for rb in range(BF16_SUBLANES, rows + 1, BF16_SUBLANES):
        if rows % rb == 0 and rows // rb <= steps:
            return rb
    raise ValueError(f"cannot cast {rows} rows in {steps} steps")


def _ffn_first(x, gain, w_in, w_out, next_w_in, casts, *, tm=1024, tf_head=256, tf_tail=512,
               copy_rows=128):
    t, d = x.shape
    d_ff = w_out.shape[0]
    assert t % tm == 0 and t // tm >= 2 and w_in.shape == (d, 2 * d_ff)
    assert d_ff % tf_head == 0 and d_ff % tf_tail == 0 and d_ff // tf_tail >= 2
    nj = d_ff // tf_head
    vec = pl.BlockSpec((1, d), lambda j: (0, 0))
    h_head, wg_bf, wu_bf, wo_bf = pl.pallas_call(
        _ffn_head_body,
        grid=(nj,),
        in_specs=[
            pl.BlockSpec((tm, d), lambda j: (0, 0), pipeline_mode=pl.Buffered(1)),
            vec,
            pl.BlockSpec(memory_space=pl.ANY),
            pl.BlockSpec(memory_space=pl.ANY),
        ],
        out_specs=[
            pl.BlockSpec((tm, d), lambda j: (0, 0)),
            pl.BlockSpec((d, tf_head), lambda j: (0, j)),
            pl.BlockSpec((d, tf_head), lambda j: (0, j)),
            pl.BlockSpec((tf_head, d), lambda j: (j, 0)),
        ],
        out_shape=[
            jax.ShapeDtypeStruct((tm, d), F32),
            jax.ShapeDtypeStruct((d, d_ff), BF16),
            jax.ShapeDtypeStruct((d, d_ff), BF16),
            jax.ShapeDtypeStruct((d_ff, d), BF16),
        ],
        scratch_shapes=[
            pltpu.VMEM((tm, d), BF16),
            pltpu.VMEM((HEAD_WEIGHT_SLOTS, d, tf_head), F32),
            pltpu.VMEM((HEAD_WEIGHT_SLOTS, d, tf_head), F32),
            pltpu.VMEM((HEAD_WEIGHT_SLOTS, tf_head, d), F32),
            pltpu.SemaphoreType.DMA((3, HEAD_WEIGHT_SLOTS)),
        ],
        compiler_params=_params(("arbitrary",)),
        name="ffn_head",
    )(x, gain, w_in, w_out)

    n_tiles = t // tm
    nj_tail = d_ff // tf_tail
    n_copy = tm // copy_rows
    assert tm % copy_rows == 0 and n_copy <= nj_tail
    steps = n_tiles * nj_tail
    nd, nd_ff2 = next_w_in.shape
    nd_ff = nd_ff2 // 2
    assert nd_ff % LANES == 0
    rb_in = _cast_row_block(nd, steps)
    step_block = lambda n_blocks: (lambda i, j: (jnp.minimum(i * nj_tail + j, n_blocks - 1), 0))
    cast_specs, cast_shapes = [], []
    for mat, rows in casts:
        rb = _cast_row_block(rows, steps)
        cast_specs.append(pl.BlockSpec((rb, mat.shape[1]), step_block(rows // rb)))
        cast_shapes.append(jax.ShapeDtypeStruct((rows, mat.shape[1]), BF16))
    vec2 = pl.BlockSpec((1, d), lambda i, j: (0, 0))
    wcol = lambda i, j: (0, j * jnp.minimum(i, 1))
    wrow = lambda i, j: (j * jnp.minimum(i, 1), 0)
    head_rows = lambda i, j: (jnp.where(i == 0, jnp.minimum(j, n_copy - 1), n_copy - 1), 0)
    cast_in = step_block(nd // rb_in)
    outs = pl.pallas_call(
        functools.partial(_ffn_tail_body, n_copy=n_copy, n_casts=len(casts)),
        grid=(n_tiles, nj_tail),
        in_specs=[
            pl.BlockSpec(memory_space=pl.ANY),
            vec2,
            pl.BlockSpec((d, tf_tail), wcol),
            pl.BlockSpec((d, tf_tail), wcol),
            pl.BlockSpec((tf_tail, d), wrow),
            pl.BlockSpec((copy_rows, d), head_rows, pipeline_mode=pl.Buffered(1)),
            pl.BlockSpec((rb_in, nd_ff2), cast_in),
            *cast_specs,
        ],
        out_specs=[
            pl.BlockSpec((tm, d), lambda i, j: (i, 0)),
            pl.BlockSpec((rb_in, nd_ff), cast_in),
            pl.BlockSpec((rb_in, nd_ff), cast_in),
            *cast_specs,
        ],
        out_shape=[
            jax.ShapeDtypeStruct((t, d), F32),
            jax.ShapeDtypeStruct((nd, nd_ff), BF16),
            jax.ShapeDtypeStruct((nd, nd_ff), BF16),
            *cast_shapes,
        ],
        scratch_shapes=[
            pltpu.VMEM((tm, d), BF16),
            pltpu.VMEM((tm, d), F32),
            pltpu.SemaphoreType.DMA(()),
        ],
        compiler_params=_params(("arbitrary", "arbitrary")),
        name="ffn_tail",
    )(x, gain, wg_bf, wu_bf, wo_bf, h_head, next_w_in, *[mat for mat, _ in casts])
    return outs[0], outs[1], outs[2], list(outs[3:])


def _ffn_second(x, gain, wg_bf, wu_bf, wo_bf, final_gain, *, final_norm, tm=1024, tf=768):
    t, d = x.shape
    d_ff = wo_bf.shape[0]
    nj = pl.cdiv(d_ff, tf)
    last_width = d_ff - (nj - 1) * tf
    assert t % tm == 0 and nj >= 2 and wg_bf.shape == wu_bf.shape == (d, d_ff)
    assert last_width % LANES == 0
    vec = pl.BlockSpec((1, d), lambda i, j: (0, 0))
    return pl.pallas_call(
        functools.partial(_ffn_bf16_body, final_norm=final_norm,
                          last_width=None if last_width == tf else last_width),
        grid=(t // tm, nj),
        in_specs=[
            pl.BlockSpec(memory_space=pl.ANY),
            vec,
            pl.BlockSpec((d, tf), lambda i, j: (0, j)),
            pl.BlockSpec((d, tf), lambda i, j: (0, j)),
            pl.BlockSpec((tf, d), lambda i, j: (j, 0)),
            vec,
        ],
        out_specs=pl.BlockSpec((tm, d), lambda i, j: (i, 0)),
        out_shape=jax.ShapeDtypeStruct((t, d), F32),
        scratch_shapes=[
            pltpu.VMEM((tm, d), BF16),
            pltpu.VMEM((tm, d), F32),
            pltpu.SemaphoreType.DMA(()),
        ],
        compiler_params=_params(("arbitrary", "arbitrary")),
        name="ffn_second_final" if final_norm else "ffn_second",
    )(x, gain, wg_bf, wu_bf, wo_bf, final_gain)


def _pool_group(x, halo, window, w, scale, t_first):
    tt = x.shape[0]
    ext = jnp.concatenate([halo, x], axis=0)
    row = lax.broadcasted_iota(jnp.int32, ext.shape, 0)
    acc = ext
    span = 1
    while span < window:
        acc = acc + jnp.where(row >= span, pltpu.roll(acc, span, axis=0), 0.0)
        span *= 2
    cnt = jnp.clip(row + (t_first + 1 - POOL_HALO), 1, window).astype(F32)
    pooled = (acc / cnt - ext)[POOL_HALO:, :].astype(BF16)
    return _dot(pooled, w) * scale


def _mixer_body(h_ref, gain_ref, w_ref, wgate_ref, wpool_ref, pscale_ref, wa_ref, ba_ref, gn_ref,
                y_ref,
                qd_s, ki_s, kt_s, bc_s, v_s, sg_s, st_s, halo_s,
                *, nt, wc, d_pool, dk, dv):
    s = pl.program_id(0)
    tt = h_ref.shape[0]
    heads = GLA_HEADS
    group_dim = d_pool // N_POOL_GROUPS
    pool_chunks = d_pool // wc
    q_chunk = pool_chunks
    k_chunk = q_chunk + 1
    v_chunk0 = k_chunk + 1
    v_chunks = heads * dv // wc
    g_chunk0 = v_chunk0 + v_chunks

    def w_chunk(c):
        return w_ref[c * wc:(c + 1) * wc, :]

    t = lax.rem(s, nt)

    @pl.when(t == 0)
    def _reset():
        st_s[...] = jnp.zeros_like(st_s)
        halo_s[...] = jnp.zeros_like(halo_s)

    n = _rmsnorm(h_ref[...], gain_ref[...]).astype(BF16)

    per_chunk = wc // group_dim
    for c in range(pool_chunks):
        u_c = _dot_nt(n, w_chunk(c))
        for part in range(per_chunk):
            gi = c * per_chunk + part
            cols = slice(gi * group_dim, (gi + 1) * group_dim)
            x = u_c[:, part * group_dim:(part + 1) * group_dim]
            y = _pool_group(x, halo_s[:, cols], POOL_WINDOWS[gi], wpool_ref[gi].astype(BF16),
                            pscale_ref[:, cols], t * tt)
            halo_s[:, cols] = x[tt - POOL_HALO:, :]
            y_ref[:, cols] = y.astype(y_ref.dtype)

    gate = _dot_nt(n, wgate_ref[...])
    logits = _dot(gate.astype(BF16), wa_ref[...]) + ba_ref[...]
    log_alpha = _log_sigmoid(logits) * (1.0 / GATE_LOGIT_NORMALIZER)
    pos = lax.broadcasted_iota(jnp.int32, log_alpha.shape, 0) & (CHUNK - 1)
    bcum = log_alpha
    span = 1
    while span < CHUNK:
        bcum = bcum + jnp.where(pos >= span, pltpu.roll(bcum, span, axis=0), 0.0)
        span *= 2
    bc_s[...] = bcum
    n_chunks = tt // CHUNK
    b3 = bcum.reshape(n_chunks, CHUNK, heads * dk)
    b_last = jnp.broadcast_to(b3[:, CHUNK - 1:CHUNK, :], b3.shape).reshape(tt, heads * dk)
    q = _dot_nt(n, w_chunk(q_chunk))
    qd_s[...] = (q * (dk ** -0.5) * jnp.exp(bcum)).astype(BF16)
    k = _dot_nt(n, w_chunk(k_chunk))
    ki_s[...] = (k * jnp.exp(-bcum)).astype(BF16)
    kt_s[...] = (k * jnp.exp(b_last - bcum)).astype(BF16)
    for c in range(v_chunks):
        cols = slice(c * wc, (c + 1) * wc)
        v_s[:, cols] = _dot_nt(n, w_chunk(v_chunk0 + c)).astype(BF16)
        sg_s[:, cols] = _silu(_dot_nt(n, w_chunk(g_chunk0 + c)))

    sb_rows = SB_CHUNKS * CHUNK
    r_idx = lax.broadcasted_iota(jnp.int32, (sb_rows, sb_rows), 0)
    c_idx = lax.broadcasted_iota(jnp.int32, (sb_rows, sb_rows), 1)
    same_chunk_causal = ((r_idx // CHUNK) == (c_idx // CHUNK)) & (r_idx >= c_idx)

    def superblock(sb, carry):
        row0 = pl.multiple_of(sb * sb_rows, sb_rows)
        rows = pl.ds(row0, sb_rows)
        decays = [jnp.exp(bc_s[pl.ds(row0 + (c * CHUNK + CHUNK - 1), 1), :]) for c in range(SB_CHUNKS)]
        for hd in range(heads):
            kc = slice(hd * dk, (hd + 1) * dk)
            vc = slice(hd * dv, (hd + 1) * dv)
            qd = qd_s[rows, kc]
            kt = kt_s[rows, kc]
            v = v_s[rows, vc]
            scores = jnp.where(same_chunk_causal, _dot_nt(qd, ki_s[rows, kc]), 0.0).astype(BF16)
            o = _dot(scores, v)
            state_t = st_s[hd]
            inter = []
            for c in range(SB_CHUNKS):
                cr = slice(c * CHUNK, (c + 1) * CHUNK)
                inter.append(_dot_nt(qd[cr], state_t.astype(BF16)))
                state_t = state_t * decays[c][:, kc] + _dot_tn(v[cr], kt[cr])
            st_s[hd] = state_t
            o = _rmsnorm(o + jnp.concatenate(inter, axis=0), gn_ref[...])
            y_ref[rows, d_pool + hd * dv:d_pool + (hd + 1) * dv] = (o * sg_s[rows, vc]).astype(y_ref.dtype)
        return carry

    lax.fori_loop(0, tt // sb_rows, superblock, 0)


def _mixer(h, gain, w_in_t_bf, w_gate_t, w_pool, pool_scale, w_alpha, b_alpha, gla_norm, *,
           seq, d_pool, dk, dv, tt=512, wc=512):
    t, d = h.shape
    heads = GLA_HEADS
    d_gla = heads * dv
    n_main = d_pool + 2 * heads * dk + 2 * d_gla
    assert w_in_t_bf.shape == (n_main, d)
    nt = seq // tt
    n_tiles = t // tt
    group_dim = d_pool // N_POOL_GROUPS
    assert seq % tt == 0 and tt % (SB_CHUNKS * CHUNK) == 0 and n_main % wc == 0 and heads * dk == wc
    assert d_pool % wc == 0 and d_gla % wc == 0 and wc % group_dim == 0
    const = lambda s: (0, 0)
    return pl.pallas_call(
        functools.partial(_mixer_body, nt=nt, wc=wc, d_pool=d_pool, dk=dk, dv=dv),
        grid=(n_tiles,),
        in_specs=[
            pl.BlockSpec((tt, d), lambda s: (s, 0)),
            pl.BlockSpec((1, d), const),
            pl.BlockSpec((n_main, d), const, pipeline_mode=pl.Buffered(1)),
            pl.BlockSpec((LANES, d), const),
            pl.BlockSpec((N_POOL_GROUPS, group_dim, group_dim), lambda s: (0, 0, 0)),
            pl.BlockSpec((1, d_pool), const),
            pl.BlockSpec((LANES, heads * dk), const),
            pl.BlockSpec((1, heads * dk), const),
            pl.BlockSpec((1, dv), const),
        ],
        out_specs=pl.BlockSpec((tt, d_pool + d_gla), lambda s: (s, 0)),
        out_shape=jax.ShapeDtypeStruct((t, d_pool + d_gla), BF16),
        scratch_shapes=[
            pltpu.VMEM((tt, heads * dk), BF16),
            pltpu.VMEM((tt, heads * dk), BF16),
            pltpu.VMEM((tt, heads * dk), BF16),
            pltpu.VMEM((tt, heads * dk), F32),
            pltpu.VMEM((tt, d_gla), BF16),
            pltpu.VMEM((tt, d_gla), F32),
            pltpu.VMEM((heads, dv, dk), F32),
            pltpu.VMEM((POOL_HALO, d_pool), F32),
        ],
        compiler_params=_params(("arbitrary",)),
        name="mixer",
    )(h, gain, w_in_t_bf, w_gate_t, w_pool, pool_scale, w_alpha, b_alpha, gla_norm)


def _mix_out_body(h_ref, y_ref, w_ref, o_ref):
    o_ref[...] = h_ref[...] + _dot(y_ref[...], w_ref[...])


def _mix_out(h, y, w_out, *, tm=1024):
    t, d = h.shape
    d_mix = y.shape[1]
    assert t % tm == 0 and w_out.shape == (d_mix, d)
    return pl.pallas_call(
        _mix_out_body,
        grid=(t // tm,),
        in_specs=[
            pl.BlockSpec((tm, d), lambda i: (i, 0)),
            pl.BlockSpec((tm, d_mix), lambda i: (i, 0)),
            pl.BlockSpec((d_mix, d), lambda i: (0, 0), pipeline_mode=pl.Buffered(1)),
        ],
        out_specs=pl.BlockSpec((tm, d), lambda i: (i, 0)),
        out_shape=jax.ShapeDtypeStruct((t, d), F32),
        compiler_params=_params(("parallel",)),
        name="mix_out",
    )(h, y, w_out)


def kernel(x, ffn1_norm, ffn1_w_in, ffn1_w_out, mix_norm, w_in_mix, w_pool, pool_scale,
           w_alpha, b_alpha, gla_norm, w_out_mix, ffn2_norm, ffn2_w_in, ffn2_w_out, final_norm):
    b, s, d = x.shape
    depth = ffn1_norm.shape[0]
    d_pool = pool_scale.shape[1]
    dk_total = w_alpha.shape[2]
    dk = dk_total // GLA_HEADS
    dv = gla_norm.shape[1]
    d_gla = GLA_HEADS * dv
    n_main = d_pool + 2 * dk_total + 2 * d_gla
    assert w_in_mix.shape[2] == n_main + GLA_GATE_RANK and s % CHUNK == 0 and depth >= 1

    h = x.reshape(b * s, d)
    final_gain = final_norm.reshape(1, d)
    for l in range(depth):
        last = l == depth - 1
        w_mix_t = jnp.swapaxes(w_in_mix[l], 0, 1)
        h, wg2_bf, wu2_bf, (wo2_bf, w_mix_t_bf, w_out_mix_bf) = _ffn_first(
            h, ffn1_norm[l].reshape(1, d), ffn1_w_in[l], ffn1_w_out[l], ffn2_w_in[l],
            [(ffn2_w_out[l], ffn2_w_out.shape[1]), (w_mix_t, n_main), (w_out_mix[l], w_out_mix.shape[1])])
        w_gate_t = jnp.pad(w_mix_t[n_main:, :], ((0, LANES - GLA_GATE_RANK), (0, 0))).astype(BF16)
        w_alpha_pad = jnp.pad(w_alpha[l], ((0, LANES - GLA_GATE_RANK), (0, 0))).astype(BF16)
        y = _mixer(h, mix_norm[l].reshape(1, d), w_mix_t_bf, w_gate_t, w_pool[l],
                   pool_scale[l].reshape(1, d_pool), w_alpha_pad, b_alpha[l].reshape(1, dk_total),
                   gla_norm[l].reshape(1, dv), seq=s, d_pool=d_pool, dk=dk, dv=dv)
        h = _mix_out(h, y, w_out_mix_bf)

        h = _ffn_second(h, ffn2_norm[l].reshape(1, d), wg2_bf, wu2_bf, wo2_bf, final_gain,
                        final_norm=last)
    return h.reshape(b, s, d)
```

```python
import functools

import jax
import jax.numpy as jnp
from jax import lax
from jax.experimental import pallas as pl
from jax.experimental.pallas import tpu as pltpu

F32 = jnp.float32
BF16 = jnp.bfloat16

EPS = 1e-6
POOL_WINDOWS = (2, 4, 8, 16)
N_POOL_GROUPS = len(POOL_WINDOWS)
POOL_HALO = 16
GLA_HEADS = 4
GLA_GATE_RANK = 16
GATE_LOGIT_NORMALIZER = 16.0
CHUNK = 64
SB_CHUNKS = 4
HEAD_WEIGHT_SLOTS = 3

LANES = 128
BF16_SUBLANES = 16
VMEM_LIMIT_BYTES = 58 * 1024 * 1024


def _rmsnorm(x, gain):
    ms = jnp.mean(x * x, axis=-1, keepdims=True)
    return x * lax.rsqrt(ms + EPS) * gain


def _silu(x):
    return x * (1.0 / (1.0 + jnp.exp(-x)))


def _log_sigmoid(z):
    return jnp.minimum(z, 0.0) - jnp.log1p(jnp.exp(-jnp.abs(z)))


def _dot(a, b):
    return jnp.dot(a, b, preferred_element_type=F32)


def _dot_nt(a, b):
    return lax.dot_general(a, b, (((1,), (1,)), ((), ())), preferred_element_type=F32)


def _dot_tn(a, b):
    return lax.dot_general(a, b, (((0,), (0,)), ((), ())), preferred_element_type=F32)


def _params(semantics):
    return pltpu.CompilerParams(dimension_semantics=semantics, vmem_limit_bytes=VMEM_LIMIT_BYTES)


def _ffn_step(j, nj, x_ref, gain_ref, weights, fgain_ref, o_ref, n_ref, final_norm,
              last_weights=None):
    def slice_update(n, get_weights):
        wg, wu, wo = get_weights()
        gate = _dot(n, wg)
        up = _dot(n, wu)
        act = (0.5 * _silu(gate) * up).astype(BF16)
        return _dot(act, wo)

    @pl.when(j == 0)
    def _first():
        x = x_ref[...]
        n = _rmsnorm(x, gain_ref[...]).astype(BF16)
        n_ref[...] = n
        o_ref[...] = x + slice_update(n, weights)

    if last_weights is None:

        @pl.when(j > 0)
        def _rest():
            o_ref[...] += slice_update(n_ref[...], weights)

    else:

        @pl.when((j > 0) & (j < nj - 1))
        def _middle():
            o_ref[...] += slice_update(n_ref[...], weights)

        @pl.when(j == nj - 1)
        def _last():
            o_ref[...] += slice_update(n_ref[...], last_weights)

    if final_norm:

        @pl.when(j == nj - 1)
        def _finish():
            o_ref[...] = _rmsnorm(o_ref[...], fgain_ref[...])


def _ffn_head_body(x_ref, gain_ref, w_in_hbm, w_out_hbm,
                   o_ref, wg_bf_ref, wu_bf_ref, wo_bf_ref,
                   n_ref, wg_buf, wu_buf, wo_buf, sems):
    j = pl.program_id(0)
    nj = pl.num_programs(0)
    n_slots, _, tf = wg_buf.shape
    d_ff = w_out_hbm.shape[0]
    lookahead = n_slots - 1

    def slice_copies(step, slot):
        lo = pl.multiple_of(step * tf, tf)
        return (
            pltpu.make_async_copy(w_in_hbm.at[:, pl.ds(lo, tf)], wg_buf.at[slot], sems.at[0, slot]),
            pltpu.make_async_copy(w_in_hbm.at[:, pl.ds(d_ff + lo, tf)], wu_buf.at[slot], sems.at[1, slot]),
            pltpu.make_async_copy(w_out_hbm.at[pl.ds(lo, tf), :], wo_buf.at[slot], sems.at[2, slot]),
        )

    @pl.when(j == 0)
    def _prime():
        for step in range(lookahead):
            for c in slice_copies(step, step):
                c.start()

    @pl.when(j + lookahead < nj)
    def _prefetch():
        for c in slice_copies(j + lookahead, lax.rem(j + lookahead, n_slots)):
            c.start()

    slot = lax.rem(j, n_slots)
    for c in slice_copies(j, slot):
        c.wait()

    def weights():
        wg = wg_buf[slot].astype(BF16)
        wu = wu_buf[slot].astype(BF16)
        wo = wo_buf[slot].astype(BF16)
        wg_bf_ref[...] = wg
        wu_bf_ref[...] = wu
        wo_bf_ref[...] = wo
        return wg, wu, wo

    _ffn_step(j, nj, x_ref, gain_ref, weights, None, o_ref, n_ref, False)


def _ffn_tail_body(*refs, n_copy, n_casts):
    x_hbm, gain_ref, wg_ref, wu_ref, wo_ref, head_ref, nw_in_ref = refs[:7]
    cast_in_refs = refs[7:7 + n_casts]
    o_ref, nwg_bf_ref, nwu_bf_ref = refs[7 + n_casts:10 + n_casts]
    cast_out_refs = refs[10 + n_casts:10 + 2 * n_casts]
    n_ref, x_buf, x_sem = refs[10 + 2 * n_casts:]
    i = pl.program_id(0)
    j = pl.program_id(1)
    n_tiles = pl.num_programs(0)
    tm = x_buf.shape[0]
    rc = head_ref.shape[0]

    nd_ff = nwg_bf_ref.shape[1]
    nwg_bf_ref[...] = nw_in_ref[:, :nd_ff].astype(BF16)
    nwu_bf_ref[...] = nw_in_ref[:, nd_ff:].astype(BF16)
    for src_ref, dst_ref in zip(cast_in_refs, cast_out_refs):
        dst_ref[...] = src_ref[...].astype(BF16)

    def x_copy(tile):
        rows = pl.ds(pl.multiple_of(tile * tm, tm), tm)
        return pltpu.make_async_copy(x_hbm.at[rows, :], x_buf, x_sem)

    @pl.when((j == 1) & (i + 1 < n_tiles))
    def _prefetch_x():
        x_copy(i + 1).start()

    @pl.when((i == 0) & (j < n_copy))
    def _copy_head_rows():
        o_ref[pl.ds(pl.multiple_of(j * rc, rc), rc), :] = head_ref[...]

    @pl.when((i > 0) & (j == 0))
    def _wait_x():
        x_copy(i).wait()

    @pl.when(i > 0)
    def _compute():
        _ffn_step(j, pl.num_programs(1), x_buf, gain_ref,
                  lambda: (wg_ref[...], wu_ref[...], wo_ref[...]), None, o_ref, n_ref, False)


def _ffn_bf16_body(x_hbm, gain_ref, wg_ref, wu_ref, wo_ref, fgain_ref, o_ref, n_ref, x_buf, x_sem,
                   *, final_norm, last_width):
    i = pl.program_id(0)
    j = pl.program_id(1)
    tm = x_buf.shape[0]

    def x_copy(tile):
        rows = pl.ds(pl.multiple_of(tile * tm, tm), tm)
        return pltpu.make_async_copy(x_hbm.at[rows, :], x_buf, x_sem)

    @pl.when((i == 0) & (j == 0))
    def _first_x():
        x_copy(0).start()

    @pl.when((j == 1) & (i + 1 < pl.num_programs(0)))
    def _prefetch_x():
        x_copy(i + 1).start()

    @pl.when(j == 0)
    def _wait_x():
        x_copy(i).wait()

    last_weights = None
    if last_width is not None:
        last_weights = lambda: (wg_ref[:, :last_width], wu_ref[:, :last_width], wo_ref[:last_width, :])
    _ffn_step(j, pl.num_programs(1), x_buf, gain_ref,
              lambda: (wg_ref[...], wu_ref[...], wo_ref[...]), fgain_ref, o_ref, n_ref, final_norm,
              last_weights=last_weights)


def _cast_row_block(rows, steps):
    for rb in range(BF16_SUBLANES, rows + 1, BF16_SUBLANES):
        if rows % rb == 0 and rows // rb <= steps:
            return rb
    raise ValueError(f"cannot cast {rows} rows in {steps} steps")


def _ffn_first(x, gain, w_in, w_out, next_w_in, casts, *, tm=1024, tf_head=256, tf_tail=512,
               copy_rows=128):
    t, d = x.shape
    d_ff = w_out.shape[0]
    assert t % tm == 0 and t // tm >= 2 and w_in.shape == (d, 2 * d_ff)
    assert d_ff % tf_head == 0 and d_ff % tf_tail == 0 and d_ff // tf_tail >= 2
    nj = d_ff // tf_head
    vec = pl.BlockSpec((1, d), lambda j: (0, 0))
    h_head, wg_bf, wu_bf, wo_bf = pl.pallas_call(
        _ffn_head_body,
        grid=(nj,),
        in_specs=[
            pl.BlockSpec((tm, d), lambda j: (0, 0), pipeline_mode=pl.Buffered(1)),
            vec,
            pl.BlockSpec(memory_space=pl.ANY),
            pl.BlockSpec(memory_space=pl.ANY),
        ],
        out_specs=[
            pl.BlockSpec((tm, d), lambda j: (0, 0)),
            pl.BlockSpec((d, tf_head), lambda j: (0, j)),
            pl.BlockSpec((d, tf_head), lambda j: (0, j)),
            pl.BlockSpec((tf_head, d), lambda j: (j, 0)),
        ],
        out_shape=[
            jax.ShapeDtypeStruct((tm, d), F32),
            jax.ShapeDtypeStruct((d, d_ff), BF16),
            jax.ShapeDtypeStruct((d, d_ff), BF16),
            jax.ShapeDtypeStruct((d_ff, d), BF16),
        ],
        scratch_shapes=[
            pltpu.VMEM((tm, d), BF16),
            pltpu.VMEM((HEAD_WEIGHT_SLOTS, d, tf_head), F32),
            pltpu.VMEM((HEAD_WEIGHT_SLOTS, d, tf_head), F32),
            pltpu.VMEM((HEAD_WEIGHT_SLOTS, tf_head, d), F32),
            pltpu.SemaphoreType.DMA((3, HEAD_WEIGHT_SLOTS)),
        ],
        compiler_params=_params(("arbitrary",)),
        name="ffn_head",
    )(x, gain, w_in, w_out)

    n_tiles = t // tm
    nj_tail = d_ff // tf_tail
    n_copy = tm // copy_rows
    assert tm % copy_rows == 0 and n_copy <= nj_tail
    steps = n_tiles * nj_tail
    nd, nd_ff2 = next_w_in.shape
    nd_ff = nd_ff2 // 2
    assert nd_ff % LANES == 0
    rb_in = _cast_row_block(nd, steps)
    step_block = lambda n_blocks: (lambda i, j: (jnp.minimum(i * nj_tail + j, n_blocks - 1), 0))
    cast_specs, cast_shapes = [], []
    for mat, rows in casts:
        rb = _cast_row_block(rows, steps)
        cast_specs.append(pl.BlockSpec((rb, mat.shape[1]), step_block(rows // rb)))
        cast_shapes.append(jax.ShapeDtypeStruct((rows, mat.shape[1]), BF16))
    vec2 = pl.BlockSpec((1, d), lambda i, j: (0, 0))
    wcol = lambda i, j: (0, j * jnp.minimum(i, 1))
    wrow = lambda i, j: (j * jnp.minimum(i, 1), 0)
    head_rows = lambda i, j: (jnp.where(i == 0, jnp.minimum(j, n_copy - 1), n_copy - 1), 0)
    cast_in = step_block(nd // rb_in)
    outs = pl.pallas_call(
        functools.partial(_ffn_tail_body, n_copy=n_copy, n_casts=len(casts)),
        grid=(n_tiles, nj_tail),
        in_specs=[
            pl.BlockSpec(memory_space=pl.ANY),
            vec2,
            pl.BlockSpec((d, tf_tail), wcol),
            pl.BlockSpec((d, tf_tail), wcol),
            pl.BlockSpec((tf_tail, d), wrow),
            pl.BlockSpec((copy_rows, d), head_rows, pipeline_mode=pl.Buffered(1)),
            pl.BlockSpec((rb_in, nd_ff2), cast_in),
            *cast_specs,
        ],
        out_specs=[
            pl.BlockSpec((tm, d), lambda i, j: (i, 0)),
            pl.BlockSpec((rb_in, nd_ff), cast_in),
            pl.BlockSpec((rb_in, nd_ff), cast_in),
            *cast_specs,
        ],
        out_shape=[
            jax.ShapeDtypeStruct((t, d), F32),
            jax.ShapeDtypeStruct((nd, nd_ff), BF16),
            jax.ShapeDtypeStruct((nd, nd_ff), BF16),
            *cast_shapes,
        ],
        scratch_shapes=[
            pltpu.VMEM((tm, d), BF16),
            pltpu.VMEM((tm, d), F32),
            pltpu.SemaphoreType.DMA(()),
        ],
        compiler_params=_params(("arbitrary", "arbitrary")),
        name="ffn_tail",
    )(x, gain, wg_bf, wu_bf, wo_bf, h_head, next_w_in, *[mat for mat, _ in casts])
    return outs[0], outs[1], outs[2], list(outs[3:])


def _ffn_second(x, gain, wg_bf, wu_bf, wo_bf, final_gain, *, final_norm, tm=1024, tf=768):
    t, d = x.shape
    d_ff = wo_bf.shape[0]
    nj = pl.cdiv(d_ff, tf)
    last_width = d_ff - (nj - 1) * tf
    assert t % tm == 0 and nj >= 2 and wg_bf.shape == wu_bf.shape == (d, d_ff)
    assert last_width % LANES == 0
    vec = pl.BlockSpec((1, d), lambda i, j: (0, 0))
    return pl.pallas_call(
        functools.partial(_ffn_bf16_body, final_norm=final_norm,
                          last_width=None if last_width == tf else last_width),
        grid=(t // tm, nj),
        in_specs=[
            pl.BlockSpec(memory_space=pl.ANY),
            vec,
            pl.BlockSpec((d, tf), lambda i, j: (0, j)),
            pl.BlockSpec((d, tf), lambda i, j: (0, j)),
            pl.BlockSpec((tf, d), lambda i, j: (j, 0)),
            vec,
        ],
        out_specs=pl.BlockSpec((tm, d), lambda i, j: (i, 0)),
        out_shape=jax.ShapeDtypeStruct((t, d), F32),
        scratch_shapes=[
            pltpu.VMEM((tm, d), BF16),
            pltpu.VMEM((tm, d), F32),
            pltpu.SemaphoreType.DMA(()),
        ],
        compiler_params=_params(("arbitrary", "arbitrary")),
        name="ffn_second_final" if final_norm else "ffn_second",
    )(x, gain, wg_bf, wu_bf, wo_bf, final_gain)


def _pool_group(x, halo, window, w, scale, t_first):
    tt = x.shape[0]
    ext = jnp.concatenate([halo, x], axis=0)
    row = lax.broadcasted_iota(jnp.int32, ext.shape, 0)
    acc = ext
    span = 1
    while span < window:
        acc = acc + jnp.where(row >= span, pltpu.roll(acc, span, axis=0), 0.0)
        span *= 2
    cnt = jnp.clip(row + (t_first + 1 - POOL_HALO), 1, window).astype(F32)
    pooled = (acc / cnt - ext)[POOL_HALO:, :].astype(BF16)
    return _dot(pooled, w) * scale


def _mixer_body(h_ref, gain_ref, w_ref, wgate_ref, wpool_ref, pscale_ref, wa_ref, ba_ref, gn_ref,
                wout_ref, o_ref,
                y_ref, qd_s, ki_s, kt_s, bc_s, v_s, sg_s, st_s, halo_s,
                *, nt, wc, d_pool, dk, dv):
    s = pl.program_id(0)
    tt = h_ref.shape[0]
    heads = GLA_HEADS
    group_dim = d_pool // N_POOL_GROUPS
    pool_chunks = d_pool // wc
    q_chunk = pool_chunks
    k_chunk = q_chunk + 1
    v_chunk0 = k_chunk + 1
    v_chunks = heads * dv // wc
    g_chunk0 = v_chunk0 + v_chunks

    def w_chunk(c):
        return w_ref[c * wc:(c + 1) * wc, :]

    t = lax.rem(s, nt)

    @pl.when(t == 0)
    def _reset():
        st_s[...] = jnp.zeros_like(st_s)
        halo_s[...] = jnp.zeros_like(halo_s)

    n = _rmsnorm(h_ref[...], gain_ref[...]).astype(BF16)

    per_chunk = wc // group_dim
    for c in range(pool_chunks):
        u_c = _dot_nt(n, w_chunk(c))
        for part in range(per_chunk):
            gi = c * per_chunk + part
            cols = slice(gi * group_dim, (gi + 1) * group_dim)
            x = u_c[:, part * group_dim:(part + 1) * group_dim]
            y = _pool_group(x, halo_s[:, cols], POOL_WINDOWS[gi], wpool_ref[gi].astype(BF16),
                            pscale_ref[:, cols], t * tt)
            halo_s[:, cols] = x[tt - POOL_HALO:, :]
            y_ref[:, cols] = y.astype(y_ref.dtype)

    gate = _dot_nt(n, wgate_ref[...])
    logits = _dot(gate.astype(BF16), wa_ref[...]) + ba_ref[...]
    log_alpha = _log_sigmoid(logits) * (1.0 / GATE_LOGIT_NORMALIZER)
    pos = lax.broadcasted_iota(jnp.int32, log_alpha.shape, 0) & (CHUNK - 1)
    bcum = log_alpha
    span = 1
    while span < CHUNK:
        bcum = bcum + jnp.where(pos >= span, pltpu.roll(bcum, span, axis=0), 0.0)
        span *= 2
    bc_s[...] = bcum
    n_chunks = tt // CHUNK
    b3 = bcum.reshape(n_chunks, CHUNK, heads * dk)
    b_last = jnp.broadcast_to(b3[:, CHUNK - 1:CHUNK, :], b3.shape).reshape(tt, heads * dk)
    q = _dot_nt(n, w_chunk(q_chunk))
    qd_s[...] = (q * (dk ** -0.5) * jnp.exp(bcum)).astype(BF16)
    k = _dot_nt(n, w_chunk(k_chunk))
    ki_s[...] = (k * jnp.exp(-bcum)).astype(BF16)
    kt_s[...] = (k * jnp.exp(b_last - bcum)).astype(BF16)
    for c in range(v_chunks):
        cols = slice(c * wc, (c + 1) * wc)
        v_s[:, cols] = _dot_nt(n, w_chunk(v_chunk0 + c)).astype(BF16)
        sg_s[:, cols] = _silu(_dot_nt(n, w_chunk(g_chunk0 + c)))

    sb_rows = SB_CHUNKS * CHUNK
    r_idx = lax.broadcasted_iota(jnp.int32, (sb_rows, sb_rows), 0)
    c_idx = lax.broadcasted_iota(jnp.int32, (sb_rows, sb_rows), 1)
    same_chunk_causal = ((r_idx // CHUNK) == (c_idx // CHUNK)) & (r_idx >= c_idx)

    def superblock(sb, carry):
        row0 = pl.multiple_of(sb * sb_rows, sb_rows)
        rows = pl.ds(row0, sb_rows)
        decays = [jnp.exp(bc_s[pl.ds(row0 + (c * CHUNK + CHUNK - 1), 1), :]) for c in range(SB_CHUNKS)]
        for hd in range(heads):
            kc = slice(hd * dk, (hd + 1) * dk)
            vc = slice(hd * dv, (hd + 1) * dv)
            qd = qd_s[rows, kc]
            kt = kt_s[rows, kc]
            v = v_s[rows, vc]
            scores = jnp.where(same_chunk_causal, _dot_nt(qd, ki_s[rows, kc]), 0.0).astype(BF16)
            o = _dot(scores, v)
            state_t = st_s[hd]
            inter = []
            for c in range(SB_CHUNKS):
                cr = slice(c * CHUNK, (c + 1) * CHUNK)
                inter.append(_dot_nt(qd[cr], state_t.astype(BF16)))
                state_t = state_t * decays[c][:, kc] + _dot_tn(v[cr], kt[cr])
            st_s[hd] = state_t
            o = _rmsnorm(o + jnp.concatenate(inter, axis=0), gn_ref[...])
            y_ref[rows, d_pool + hd * dv:d_pool + (hd + 1) * dv] = (o * sg_s[rows, vc]).astype(y_ref.dtype)
        return carry

    lax.fori_loop(0, tt // sb_rows, superblock, 0)

    o_ref[...] = h_ref[...] + _dot(y_ref[...], wout_ref[...])


def _mixer(h, gain, w_in_t_bf, w_gate_t, w_pool, pool_scale, w_alpha, b_alpha, gla_norm, w_out_bf, *,
           seq, d_pool, dk, dv, tt=512, wc=512):
    t, d = h.shape
    heads = GLA_HEADS
    d_gla = heads * dv
    n_main = d_pool + 2 * heads * dk + 2 * d_gla
    assert w_in_t_bf.shape == (n_main, d)
    nt = seq // tt
    n_tiles = t // tt
    group_dim = d_pool // N_POOL_GROUPS
    assert seq % tt == 0 and tt % (SB_CHUNKS * CHUNK) == 0 and n_main % wc == 0 and heads * dk == wc
    assert d_pool % wc == 0 and d_gla % wc == 0 and wc % group_dim == 0
    const = lambda s: (0, 0)
    return pl.pallas_call(
        functools.partial(_mixer_body, nt=nt, wc=wc, d_pool=d_pool, dk=dk, dv=dv),
        grid=(n_tiles,),
        in_specs=[
            pl.BlockSpec((tt, d), lambda s: (s, 0)),
            pl.BlockSpec((1, d), const),
            pl.BlockSpec((n_main, d), const, pipeline_mode=pl.Buffered(1)),
            pl.BlockSpec((LANES, d), const),
            pl.BlockSpec((N_POOL_GROUPS, group_dim, group_dim), lambda s: (0, 0, 0)),
            pl.BlockSpec((1, d_pool), const),
            pl.BlockSpec((LANES, heads * dk), const),
            pl.BlockSpec((1, heads * dk), const),
            pl.BlockSpec((1, dv), const),
            pl.BlockSpec((d_pool + d_gla, d), const, pipeline_mode=pl.Buffered(1)),
        ],
        out_specs=pl.BlockSpec((tt, d), lambda s: (s, 0)),
        out_shape=jax.ShapeDtypeStruct((t, d), F32),
        scratch_shapes=[
            pltpu.VMEM((tt, d_pool + d_gla), BF16),
            pltpu.VMEM((tt, heads * dk), BF16),
            pltpu.VMEM((tt, heads * dk), BF16),
            pltpu.VMEM((tt, heads * dk), BF16),
            pltpu.VMEM((tt, heads * dk), F32),
            pltpu.VMEM((tt, d_gla), BF16),
            pltpu.VMEM((tt, d_gla), F32),
            pltpu.VMEM((heads, dv, dk), F32),
            pltpu.VMEM((POOL_HALO, d_pool), F32),
        ],
        compiler_params=_params(("arbitrary",)),
        name="mixer",
    )(h, gain, w_in_t_bf, w_gate_t, w_pool, pool_scale, w_alpha, b_alpha, gla_norm, w_out_bf)


def kernel(x, ffn1_norm, ffn1_w_in, ffn1_w_out, mix_norm, w_in_mix, w_pool, pool_scale,
           w_alpha, b_alpha, gla_norm, w_out_mix, ffn2_norm, ffn2_w_in, ffn2_w_out, final_norm):
    b, s, d = x.shape
    depth = ffn1_norm.shape[0]
    d_pool = pool_scale.shape[1]
    dk_total = w_alpha.shape[2]
    dk = dk_total // GLA_HEADS
    dv = gla_norm.shape[1]
    d_gla = GLA_HEADS * dv
    n_main = d_pool + 2 * dk_total + 2 * d_gla
    assert w_in_mix.shape[2] == n_main + GLA_GATE_RANK and s % CHUNK == 0 and depth >= 1

    h = x.reshape(b * s, d)
    final_gain = final_norm.reshape(1, d)
    for l in range(depth):
        last = l == depth - 1
        w_mix_t = jnp.swapaxes(w_in_mix[l], 0, 1)
        h, wg2_bf, wu2_bf, (wo2_bf, w_mix_t_bf, w_out_mix_bf) = _ffn_first(
            h, ffn1_norm[l].reshape(1, d), ffn1_w_in[l], ffn1_w_out[l], ffn2_w_in[l],
            [(ffn2_w_out[l], ffn2_w_out.shape[1]), (w_mix_t, n_main), (w_out_mix[l], w_out_mix.shape[1])])
        w_gate_t = jnp.pad(w_mix_t[n_main:, :], ((0, LANES - GLA_GATE_RANK), (0, 0))).astype(BF16)
        w_alpha_pad = jnp.pad(w_alpha[l], ((0, LANES - GLA_GATE_RANK), (0, 0))).astype(BF16)
        h = _mixer(h, mix_norm[l].reshape(1, d), w_mix_t_bf, w_gate_t, w_pool[l],
                   pool_scale[l].reshape(1, d_pool), w_alpha_pad, b_alpha[l].reshape(1, dk_total),
                   gla_norm[l].reshape(1, dv), w_out_mix_bf, seq=s, d_pool=d_pool, dk=dk, dv=dv)

        h = _ffn_second(h, ffn2_norm[l].reshape(1, d), wg2_bf, wu2_bf, wo2_bf, final_gain,
                        final_norm=last)
    return h.reshape(b, s, d)
```

```python
import functools

import jax
import jax.numpy as jnp
from jax import lax
from jax.experimental import pallas as pl
from jax.experimental.pallas import tpu as pltpu

F32 = jnp.float32
BF16 = jnp.bfloat16

EPS = 1e-6
POOL_WINDOWS = (2, 4, 8, 16)
N_POOL_GROUPS = len(POOL_WINDOWS)
POOL_HALO = 16
GLA_HEADS = 4
GLA_GATE_RANK = 16
GATE_LOGIT_NORMALIZER = 16.0
CHUNK = 64
SB_CHUNKS = 4
HEAD_WEIGHT_SLOTS = 3

LANES = 128
BF16_SUBLANES = 16
VMEM_LIMIT_BYTES = 58 * 1024 * 1024


def _rmsnorm(x, gain):
    ms = jnp.mean(x * x, axis=-1, keepdims=True)
    return x * lax.rsqrt(ms + EPS) * gain


def _silu(x):
    return x * (1.0 / (1.0 + jnp.exp(-x)))


def _log_sigmoid(z):
    return jnp.minimum(z, 0.0) - jnp.log1p(jnp.exp(-jnp.abs(z)))


def _dot(a, b):
    return jnp.dot(a, b, preferred_element_type=F32)


def _dot_nt(a, b):
    return lax.dot_general(a, b, (((1,), (1,)), ((), ())), preferred_element_type=F32)


def _dot_tn(a, b):
    return lax.dot_general(a, b, (((0,), (0,)), ((), ())), preferred_element_type=F32)


def _params(semantics):
    return pltpu.CompilerParams(dimension_semantics=semantics, vmem_limit_bytes=VMEM_LIMIT_BYTES)


def _ffn_step(j, nj, x_ref, gain_ref, weights, fgain_ref, o_ref, n_ref, final_norm,
              last_weights=None):
    def slice_update(n, get_weights):
        wg, wu, wo = get_weights()
        gate = _dot(n, wg)
        up = _dot(n, wu)
        act = (0.5 * _silu(gate) * up).astype(BF16)
        return _dot(act, wo)

    @pl.when(j == 0)
    def _first():
        x = x_ref[...]
        n = _rmsnorm(x, gain_ref[...]).astype(BF16)
        n_ref[...] = n
        o_ref[...] = x + slice_update(n, weights)

    if last_weights is None:

        @pl.when(j > 0)
        def _rest():
            o_ref[...] += slice_update(n_ref[...], weights)

    else:

        @pl.when((j > 0) & (j < nj - 1))
        def _middle():
            o_ref[...] += slice_update(n_ref[...], weights)

        @pl.when(j == nj - 1)
        def _last():
            o_ref[...] += slice_update(n_ref[...], last_weights)

    if final_norm:

        @pl.when(j == nj - 1)
        def _finish():
            o_ref[...] = _rmsnorm(o_ref[...], fgain_ref[...])


def _ffn_head_body(x_ref, gain_ref, w_in_hbm, w_out_hbm,
                   o_ref, wg_bf_ref, wu_bf_ref, wo_bf_ref,
                   n_ref, wg_buf, wu_buf, wo_buf, sems):
    j = pl.program_id(0)
    nj = pl.num_programs(0)
    n_slots, _, tf = wg_buf.shape
    d_ff = w_out_hbm.shape[0]
    lookahead = n_slots - 1

    def slice_copies(step, slot):
        lo = pl.multiple_of(step * tf, tf)
        return (
            pltpu.make_async_copy(w_in_hbm.at[:, pl.ds(lo, tf)], wg_buf.at[slot], sems.at[0, slot]),
            pltpu.make_async_copy(w_in_hbm.at[:, pl.ds(d_ff + lo, tf)], wu_buf.at[slot], sems.at[1, slot]),
            pltpu.make_async_copy(w_out_hbm.at[pl.ds(lo, tf), :], wo_buf.at[slot], sems.at[2, slot]),
        )

    @pl.when(j == 0)
    def _prime():
        for step in range(lookahead):
            for c in slice_copies(step, step):
                c.start()

    @pl.when(j + lookahead < nj)
    def _prefetch():
        for c in slice_copies(j + lookahead, lax.rem(j + lookahead, n_slots)):
            c.start()

    slot = lax.rem(j, n_slots)
    for c in slice_copies(j, slot):
        c.wait()

    def weights():
        wg = wg_buf[slot].astype(BF16)
        wu = wu_buf[slot].astype(BF16)
        wo = wo_buf[slot].astype(BF16)
        wg_bf_ref[...] = wg
        wu_bf_ref[...] = wu
        wo_bf_ref[...] = wo
        return wg, wu, wo

    _ffn_step(j, nj, x_ref, gain_ref, weights, None, o_ref, n_ref, False)


def _ffn_tail_body(*refs, n_copy, n_casts):
    x_hbm, gain_ref, wg_ref, wu_ref, wo_ref, head_ref, nw_in_ref = refs[:7]
    cast_in_refs = refs[7:7 + n_casts]
    o_ref, nwg_bf_ref, nwu_bf_ref = refs[7 + n_casts:10 + n_casts]
    cast_out_refs = refs[10 + n_casts:10 + 2 * n_casts]
    n_ref, x_buf, x_sem = refs[10 + 2 * n_casts:]
    i = pl.program_id(0)
    j = pl.program_id(1)
    n_tiles = pl.num_programs(0)
    tm = x_buf.shape[0]
    rc = head_ref.shape[0]

    nd_ff = nwg_bf_ref.shape[1]
    nwg_bf_ref[...] = nw_in_ref[:, :nd_ff].astype(BF16)
    nwu_bf_ref[...] = nw_in_ref[:, nd_ff:].astype(BF16)
    for src_ref, dst_ref in zip(cast_in_refs, cast_out_refs):
        dst_ref[...] = src_ref[...].astype(BF16)

    def x_copy(tile):
        rows = pl.ds(pl.multiple_of(tile * tm, tm), tm)
        return pltpu.make_async_copy(x_hbm.at[rows, :], x_buf, x_sem)

    @pl.when((j == 1) & (i + 1 < n_tiles))
    def _prefetch_x():
        x_copy(i + 1).start()

    @pl.when((i == 0) & (j < n_copy))
    def _copy_head_rows():
        o_ref[pl.ds(pl.multiple_of(j * rc, rc), rc), :] = head_ref[...]

    @pl.when((i > 0) & (j == 0))
    def _wait_x():
        x_copy(i).wait()

    @pl.when(i > 0)
    def _compute():
        _ffn_step(j, pl.num_programs(1), x_buf, gain_ref,
                  lambda: (wg_ref[...], wu_ref[...], wo_ref[...]), None, o_ref, n_ref, False)


def _ffn_bf16_body(x_hbm, gain_ref, wg_ref, wu_ref, wo_ref, fgain_ref, o_ref, n_ref, x_buf, x_sem,
                   *, final_norm, last_width):
    i = pl.program_id(0)
    j = pl.program_id(1)
    tm = x_buf.shape[0]

    def x_copy(tile):
        rows = pl.ds(pl.multiple_of(tile * tm, tm), tm)
        return pltpu.make_async_copy(x_hbm.at[rows, :], x_buf, x_sem)

    @pl.when((i == 0) & (j == 0))
    def _first_x():
        x_copy(0).start()

    @pl.when((j == 1) & (i + 1 < pl.num_programs(0)))
    def _prefetch_x():
        x_copy(i + 1).start()

    @pl.when(j == 0)
    def _wait_x():
        x_copy(i).wait()

    last_weights = None
    if last_width is not None:
        last_weights = lambda: (wg_ref[:, :last_width], wu_ref[:, :last_width], wo_ref[:last_width, :])
    _ffn_step(j, pl.num_programs(1), x_buf, gain_ref,
              lambda: (wg_ref[...], wu_ref[...], wo_ref[...]), fgain_ref, o_ref, n_ref, final_norm,
              last_weights=last_weights)


def _cast_row_block(rows, steps):
    for rb in range(BF16_SUBLANES, rows + 1, BF16_SUBLANES):
        if rows % rb == 0 and rows // rb <= steps:
            return rb
    raise ValueError(f"cannot cast {rows} rows in {steps} steps")


def _ffn_first(x, gain, w_in, w_out, next_w_in, casts, *, tm=1024, tf_head=256, tf_tail=512,
               copy_rows=128):
    t, d = x.shape
    d_ff = w_out.shape[0]
    assert t % tm == 0 and t // tm >= 2 and w_in.shape == (d, 2 * d_ff)
    assert d_ff % tf_head == 0 and d_ff % tf_tail == 0 and d_ff // tf_tail >= 2
    nj = d_ff // tf_head
    vec = pl.BlockSpec((1, d), lambda j: (0, 0))
    h_head, wg_bf, wu_bf, wo_bf = pl.pallas_call(
        _ffn_head_body,
        grid=(nj,),
        in_specs=[
            pl.BlockSpec((tm, d), lambda j: (0, 0), pipeline_mode=pl.Buffered(1)),
            vec,
            pl.BlockSpec(memory_space=pl.ANY),
            pl.BlockSpec(memory_space=pl.ANY),
        ],
        out_specs=[
            pl.BlockSpec((tm, d), lambda j: (0, 0)),
            pl.BlockSpec((d, tf_head), lambda j: (0, j)),
            pl.BlockSpec((d, tf_head), lambda j: (0, j)),
            pl.BlockSpec((tf_head, d), lambda j: (j, 0)),
        ],
        out_shape=[
            jax.ShapeDtypeStruct((tm, d), F32),
            jax.ShapeDtypeStruct((d, d_ff), BF16),
            jax.ShapeDtypeStruct((d, d_ff), BF16),
            jax.ShapeDtypeStruct((d_ff, d), BF16),
        ],
        scratch_shapes=[
            pltpu.VMEM((tm, d), BF16),
            pltpu.VMEM((HEAD_WEIGHT_SLOTS, d, tf_head), F32),
            pltpu.VMEM((HEAD_WEIGHT_SLOTS, d, tf_head), F32),
            pltpu.VMEM((HEAD_WEIGHT_SLOTS, tf_head, d), F32),
            pltpu.SemaphoreType.DMA((3, HEAD_WEIGHT_SLOTS)),
        ],
        compiler_params=_params(("arbitrary",)),
        name="ffn_head",
    )(x, gain, w_in, w_out)

    n_tiles = t // tm
    nj_tail = d_ff // tf_tail
    n_copy = tm // copy_rows
    assert tm % copy_rows == 0 and n_copy <= nj_tail
    steps = n_tiles * nj_tail
    nd, nd_ff2 = next_w_in.shape
    nd_ff = nd_ff2 // 2
    assert nd_ff % LANES == 0
    rb_in = _cast_row_block(nd, steps)
    step_block = lambda n_blocks: (lambda i, j: (jnp.minimum(i * nj_tail + j, n_blocks - 1), 0))
    cast_specs, cast_shapes = [], []
    for mat, rows in casts:
        rb = _cast_row_block(rows, steps)
        cast_specs.append(pl.BlockSpec((rb, mat.shape[1]), step_block(rows // rb)))
        cast_shapes.append(jax.ShapeDtypeStruct((rows, mat.shape[1]), BF16))
    vec2 = pl.BlockSpec((1, d), lambda i, j: (0, 0))
    wcol = lambda i, j: (0, j * jnp.minimum(i, 1))
    wrow = lambda i, j: (j * jnp.minimum(i, 1), 0)
    head_rows = lambda i, j: (jnp.where(i == 0, jnp.minimum(j, n_copy - 1), n_copy - 1), 0)
    cast_in = step_block(nd // rb_in)
    outs = pl.pallas_call(
        functools.partial(_ffn_tail_body, n_copy=n_copy, n_casts=len(casts)),
        grid=(n_tiles, nj_tail),
        in_specs=[
            pl.BlockSpec(memory_space=pl.ANY),
            vec2,
            pl.BlockSpec((d, tf_tail), wcol),
            pl.BlockSpec((d, tf_tail), wcol),
            pl.BlockSpec((tf_tail, d), wrow),
            pl.BlockSpec((copy_rows, d), head_rows, pipeline_mode=pl.Buffered(1)),
            pl.BlockSpec((rb_in, nd_ff2), cast_in),
            *cast_specs,
        ],
        out_specs=[
            pl.BlockSpec((tm, d), lambda i, j: (i, 0)),
            pl.BlockSpec((rb_in, nd_ff), cast_in),
            pl.BlockSpec((rb_in, nd_ff), cast_in),
            *cast_specs,
        ],
        out_shape=[
            jax.ShapeDtypeStruct((t, d), F32),
            jax.ShapeDtypeStruct((nd, nd_ff), BF16),
            jax.ShapeDtypeStruct((nd, nd_ff), BF16),
            *cast_shapes,
        ],
        scratch_shapes=[
            pltpu.VMEM((tm, d), BF16),
            pltpu.VMEM((tm, d), F32),
            pltpu.SemaphoreType.DMA(()),
        ],
        compiler_params=_params(("arbitrary", "arbitrary")),
        name="ffn_tail",
    )(x, gain, wg_bf, wu_bf, wo_bf, h_head, next_w_in, *[mat for mat, _ in casts])
    return outs[0], outs[1], outs[2], list(outs[3:])


def _ffn_second(x, gain, wg_bf, wu_bf, wo_bf, final_gain, *, final_norm, tm=1024, tf=768):
    t, d = x.shape
    d_ff = wo_bf.shape[0]
    nj = pl.cdiv(d_ff, tf)
    last_width = d_ff - (nj - 1) * tf
    assert t % tm == 0 and nj >= 2 and wg_bf.shape == wu_bf.shape == (d, d_ff)
    assert last_width % LANES == 0
    vec = pl.BlockSpec((1, d), lambda i, j: (0, 0))
    step = functools.partial(_ffn_bf16_body, final_norm=final_norm,
                             last_width=None if last_width == tf else last_width)

    def body(x_hbm, gain_hbm, wg_hbm, wu_hbm, wo_hbm, fgain_hbm, o_hbm, n_ref, x_buf, x_sem):
        pltpu.emit_pipeline(
            lambda gain_ref, wg_ref, wu_ref, wo_ref, fgain_ref, o_ref: step(
                x_hbm, gain_ref, wg_ref, wu_ref, wo_ref, fgain_ref, o_ref, n_ref, x_buf, x_sem),
            grid=(t // tm, nj),
            in_specs=[
                vec,
                pl.BlockSpec((d, tf), lambda i, j: (0, j)),
                pl.BlockSpec((d, tf), lambda i, j: (0, j)),
                pl.BlockSpec((tf, d), lambda i, j: (j, 0)),
                vec,
            ],
            out_specs=[pl.BlockSpec((tm, d), lambda i, j: (i, 0))],
            dimension_semantics=(pltpu.ARBITRARY, pltpu.ARBITRARY),
        )(gain_hbm, wg_hbm, wu_hbm, wo_hbm, fgain_hbm, o_hbm)

    return pl.pallas_call(
        body,
        in_specs=[pl.BlockSpec(memory_space=pl.ANY)] * 6,
        out_specs=pl.BlockSpec(memory_space=pl.ANY),
        out_shape=jax.ShapeDtypeStruct((t, d), F32),
        scratch_shapes=[
            pltpu.VMEM((tm, d), BF16),
            pltpu.VMEM((tm, d), F32),
            pltpu.SemaphoreType.DMA(()),
        ],
        compiler_params=pltpu.CompilerParams(vmem_limit_bytes=VMEM_LIMIT_BYTES),
        name="ffn_second_final" if final_norm else "ffn_second",
    )(x, gain, wg_bf, wu_bf, wo_bf, final_gain)


def _pool_group(x, halo, window, w, scale, t_first):
    tt = x.shape[0]
    ext = jnp.concatenate([halo, x], axis=0)
    row = lax.broadcasted_iota(jnp.int32, ext.shape, 0)
    acc = ext
    span = 1
    while span < window:
        acc = acc + jnp.where(row >= span, pltpu.roll(acc, span, axis=0), 0.0)
        span *= 2
    cnt = jnp.clip(row + (t_first + 1 - POOL_HALO), 1, window).astype(F32)
    pooled = (acc / cnt - ext)[POOL_HALO:, :].astype(BF16)
    return _dot(pooled, w) * scale


def _mixer_body(h_ref, gain_ref, w_ref, wgate_ref, wpool_ref, pscale_ref, wa_ref, ba_ref, gn_ref,
                wout_ref, o_ref,
                y_ref, qd_s, ki_s, kt_s, bc_s, v_s, sg_s, st_s, halo_s,
                *, nt, wc, d_pool, dk, dv):
    s = pl.program_id(0)
    tt = h_ref.shape[0]
    heads = GLA_HEADS
    group_dim = d_pool // N_POOL_GROUPS
    pool_chunks = d_pool // wc
    q_chunk = pool_chunks
    k_chunk = q_chunk + 1
    v_chunk0 = k_chunk + 1
    v_chunks = heads * dv // wc
    g_chunk0 = v_chunk0 + v_chunks

    def w_chunk(c):
        return w_ref[c * wc:(c + 1) * wc, :]

    t = lax.rem(s, nt)

    @pl.when(t == 0)
    def _reset():
        st_s[...] = jnp.zeros_like(st_s)
        halo_s[...] = jnp.zeros_like(halo_s)

    n = _rmsnorm(h_ref[...], gain_ref[...]).astype(BF16)

    per_chunk = wc // group_dim
    for c in range(pool_chunks):
        u_c = _dot_nt(n, w_chunk(c))
        for part in range(per_chunk):
            gi = c * per_chunk + part
            cols = slice(gi * group_dim, (gi + 1) * group_dim)
            x = u_c[:, part * group_dim:(part + 1) * group_dim]
            y = _pool_group(x, halo_s[:, cols], POOL_WINDOWS[gi], wpool_ref[gi].astype(BF16),
                            pscale_ref[:, cols], t * tt)
            halo_s[:, cols] = x[tt - POOL_HALO:, :]
            y_ref[:, cols] = y.astype(y_ref.dtype)

    gate = _dot_nt(n, wgate_ref[...])
    logits = _dot(gate.astype(BF16), wa_ref[...]) + ba_ref[...]
    log_alpha = _log_sigmoid(logits) * (1.0 / GATE_LOGIT_NORMALIZER)
    pos = lax.broadcasted_iota(jnp.int32, log_alpha.shape, 0) & (CHUNK - 1)
    bcum = log_alpha
    span = 1
    while span < CHUNK:
        bcum = bcum + jnp.where(pos >= span, pltpu.roll(bcum, span, axis=0), 0.0)
        span *= 2
    bc_s[...] = bcum
    n_chunks = tt // CHUNK
    b3 = bcum.reshape(n_chunks, CHUNK, heads * dk)
    b_last = jnp.broadcast_to(b3[:, CHUNK - 1:CHUNK, :], b3.shape).reshape(tt, heads * dk)
    q = _dot_nt(n, w_chunk(q_chunk))
    qd_s[...] = (q * (dk ** -0.5) * jnp.exp(bcum)).astype(BF16)
    k = _dot_nt(n, w_chunk(k_chunk))
    ki_s[...] = (k * jnp.exp(-bcum)).astype(BF16)
    kt_s[...] = (k * jnp.exp(b_last - bcum)).astype(BF16)
    for c in range(v_chunks):
        cols = slice(c * wc, (c + 1) * wc)
        v_s[:, cols] = _dot_nt(n, w_chunk(v_chunk0 + c)).astype(BF16)
        sg_s[:, cols] = _silu(_dot_nt(n, w_chunk(g_chunk0 + c)))

    sb_rows = SB_CHUNKS * CHUNK
    r_idx = lax.broadcasted_iota(jnp.int32, (sb_rows, sb_rows), 0)
    c_idx = lax.broadcasted_iota(jnp.int32, (sb_rows, sb_rows), 1)
    same_chunk_causal = ((r_idx // CHUNK) == (c_idx // CHUNK)) & (r_idx >= c_idx)

    def superblock(sb, carry):
        row0 = pl.multiple_of(sb * sb_rows, sb_rows)
        rows = pl.ds(row0, sb_rows)
        decays = [jnp.exp(bc_s[pl.ds(row0 + (c * CHUNK + CHUNK - 1), 1), :]) for c in range(SB_CHUNKS)]
        for hd in range(heads):
            kc = slice(hd * dk, (hd + 1) * dk)
            vc = slice(hd * dv, (hd + 1) * dv)
            qd = qd_s[rows, kc]
            kt = kt_s[rows, kc]
            v = v_s[rows, vc]
            scores = jnp.where(same_chunk_causal, _dot_nt(qd, ki_s[rows, kc]), 0.0).astype(BF16)
            o = _dot(scores, v)
            state_t = st_s[hd]
            inter = []
            for c in range(SB_CHUNKS):
                cr = slice(c * CHUNK, (c + 1) * CHUNK)
                inter.append(_dot_nt(qd[cr], state_t.astype(BF16)))
                state_t = state_t * decays[c][:, kc] + _dot_tn(v[cr], kt[cr])
            st_s[hd] = state_t
            o = _rmsnorm(o + jnp.concatenate(inter, axis=0), gn_ref[...])
            y_ref[rows, d_pool + hd * dv:d_pool + (hd + 1) * dv] = (o * sg_s[rows, vc]).astype(y_ref.dtype)
        return carry

    lax.fori_loop(0, tt // sb_rows, superblock, 0)

    o_ref[...] = h_ref[...] + _dot(y_ref[...], wout_ref[...])


def _mixer(h, gain, w_in_t_bf, w_gate_t, w_pool, pool_scale, w_alpha, b_alpha, gla_norm, w_out_bf, *,
           seq, d_pool, dk, dv, tt=512, wc=512):
    t, d = h.shape
    heads = GLA_HEADS
    d_gla = heads * dv
    n_main = d_pool + 2 * heads * dk + 2 * d_gla
    assert w_in_t_bf.shape == (n_main, d)
    nt = seq // tt
    n_tiles = t // tt
    group_dim = d_pool // N_POOL_GROUPS
    assert seq % tt == 0 and tt % (SB_CHUNKS * CHUNK) == 0 and n_main % wc == 0 and heads * dk == wc
    assert d_pool % wc == 0 and d_gla % wc == 0 and wc % group_dim == 0
    const = lambda s: (0, 0)
    return pl.pallas_call(
        functools.partial(_mixer_body, nt=nt, wc=wc, d_pool=d_pool, dk=dk, dv=dv),
        grid=(n_tiles,),
        in_specs=[
            pl.BlockSpec((tt, d), lambda s: (s, 0)),
            pl.BlockSpec((1, d), const),
            pl.BlockSpec((n_main, d), const, pipeline_mode=pl.Buffered(1)),
            pl.BlockSpec((LANES, d), const),
            pl.BlockSpec((N_POOL_GROUPS, group_dim, group_dim), lambda s: (0, 0, 0)),
            pl.BlockSpec((1, d_pool), const),
            pl.BlockSpec((LANES, heads * dk), const),
            pl.BlockSpec((1, heads * dk), const),
            pl.BlockSpec((1, dv), const),
            pl.BlockSpec((d_pool + d_gla, d), const, pipeline_mode=pl.Buffered(1)),
        ],
        out_specs=pl.BlockSpec((tt, d), lambda s: (s, 0)),
        out_shape=jax.ShapeDtypeStruct((t, d), F32),
        scratch_shapes=[
            pltpu.VMEM((tt, d_pool + d_gla), BF16),
            pltpu.VMEM((tt, heads * dk), BF16),
            pltpu.VMEM((tt, heads * dk), BF16),
            pltpu.VMEM((tt, heads * dk), BF16),
            pltpu.VMEM((tt, heads * dk), F32),
            pltpu.VMEM((tt, d_gla), BF16),
            pltpu.VMEM((tt, d_gla), F32),
            pltpu.VMEM((heads, dv, dk), F32),
            pltpu.VMEM((POOL_HALO, d_pool), F32),
        ],
        compiler_params=_params(("arbitrary",)),
        name="mixer",
    )(h, gain, w_in_t_bf, w_gate_t, w_pool, pool_scale, w_alpha, b_alpha, gla_norm, w_out_bf)


def kernel(x, ffn1_norm, ffn1_w_in, ffn1_w_out, mix_norm, w_in_mix, w_pool, pool_scale,
           w_alpha, b_alpha, gla_norm, w_out_mix, ffn2_norm, ffn2_w_in, ffn2_w_out, final_norm):
    b, s, d = x.shape
    depth = ffn1_norm.shape[0]
    d_pool = pool_scale.shape[1]
    dk_total = w_alpha.shape[2]
    dk = dk_total // GLA_HEADS
    dv = gla_norm.shape[1]
    d_gla = GLA_HEADS * dv
    n_main = d_pool + 2 * dk_total + 2 * d_gla
    assert w_in_mix.shape[2] == n_main + GLA_GATE_RANK and s % CHUNK == 0 and depth >= 1

    h = x.reshape(b * s, d)
    final_gain = final_norm.reshape(1, d)
    for l in range(depth):
        last = l == depth - 1
        w_mix_t = jnp.swapaxes(w_in_mix[l], 0, 1)
        h, wg2_bf, wu2_bf, (wo2_bf, w_mix_t_bf, w_out_mix_bf) = _ffn_first(
            h, ffn1_norm[l].reshape(1, d), ffn1_w_in[l], ffn1_w_out[l], ffn2_w_in[l],
            [(ffn2_w_out[l], ffn2_w_out.shape[1]), (w_mix_t, n_main), (w_out_mix[l], w_out_mix.shape[1])])
        w_gate_t = jnp.pad(w_mix_t[n_main:, :], ((0, LANES - GLA_GATE_RANK), (0, 0))).astype(BF16)
        w_alpha_pad = jnp.pad(w_alpha[l], ((0, LANES - GLA_GATE_RANK), (0, 0))).astype(BF16)
        h = _mixer(h, mix_norm[l].reshape(1, d), w_mix_t_bf, w_gate_t, w_pool[l],
                   pool_scale[l].reshape(1, d_pool), w_alpha_pad, b_alpha[l].reshape(1, dk_total),
                   gla_norm[l].reshape(1, dv), w_out_mix_bf, seq=s, d_pool=d_pool, dk=dk, dv=dv)

        h = _ffn_second(h, ffn2_norm[l].reshape(1, d), wg2_bf, wu2_bf, wo2_bf, final_gain,
                        final_norm=last)
    return h.reshape(b, s, d)
```

```python
import functools

import jax
import jax.numpy as jnp
from jax import lax
from jax.experimental import pallas as pl
from jax.experimental.pallas import tpu as pltpu

F32 = jnp.float32
BF16 = jnp.bfloat16

EPS = 1e-6
POOL_WINDOWS = (2, 4, 8, 16)
N_POOL_GROUPS = len(POOL_WINDOWS)
POOL_HALO = 16
GLA_HEADS = 4
GLA_GATE_RANK = 16
GATE_LOGIT_NORMALIZER = 16.0
CHUNK = 64
SB_CHUNKS = 4
HEAD_WEIGHT_SLOTS = 3

LANES = 128
BF16_SUBLANES = 16
VMEM_LIMIT_BYTES = 58 * 1024 * 1024


def _rmsnorm(x, gain):
    ms = jnp.mean(x * x, axis=-1, keepdims=True)
    return x * lax.rsqrt(ms + EPS) * gain


def _silu(x):
    return x * (1.0 / (1.0 + jnp.exp(-x)))


def _log_sigmoid(z):
    return jnp.minimum(z, 0.0) - jnp.log1p(jnp.exp(-jnp.abs(z)))


def _dot(a, b):
    return jnp.dot(a, b, preferred_element_type=F32)


def _dot_nt(a, b):
    return lax.dot_general(a, b, (((1,), (1,)), ((), ())), preferred_element_type=F32)


def _dot_tn(a, b):
    return lax.dot_general(a, b, (((0,), (0,)), ((), ())), preferred_element_type=F32)


def _params(semantics):
    return pltpu.CompilerParams(dimension_semantics=semantics, vmem_limit_bytes=VMEM_LIMIT_BYTES)


def _ffn_step(j, nj, x_ref, gain_ref, weights, fgain_ref, o_ref, n_ref, final_norm,
              last_weights=None):
    def slice_update(n, get_weights):
        wg, wu, wo = get_weights()
        gate = _dot(n, wg)
        up = _dot(n, wu)
        act = (0.5 * _silu(gate) * up).astype(BF16)
        return _dot(act, wo)

    @pl.when(j == 0)
    def _first():
        x = x_ref[...]
        n = _rmsnorm(x, gain_ref[...]).astype(BF16)
        n_ref[...] = n
        o_ref[...] = x + slice_update(n, weights)

    if last_weights is None:

        @pl.when(j > 0)
        def _rest():
            o_ref[...] += slice_update(n_ref[...], weights)

    else:

        @pl.when((j > 0) & (j < nj - 1))
        def _middle():
            o_ref[...] += slice_update(n_ref[...], weights)

        @pl.when(j == nj - 1)
        def _last():
            o_ref[...] += slice_update(n_ref[...], last_weights)

    if final_norm:

        @pl.when(j == nj - 1)
        def _finish():
            o_ref[...] = _rmsnorm(o_ref[...], fgain_ref[...])


def _ffn_head_body(x_ref, gain_ref, w_in_hbm, w_out_hbm,
                   o_ref, wg_bf_ref, wu_bf_ref, wo_bf_ref,
                   n_ref, wg_buf, wu_buf, wo_buf, sems):
    j = pl.program_id(0)
    nj = pl.num_programs(0)
    n_slots, _, tf = wg_buf.shape
    d_ff = w_out_hbm.shape[0]
    lookahead = n_slots - 1

    def slice_copies(step, slot):
        lo = pl.multiple_of(step * tf, tf)
        return (
            pltpu.make_async_copy(w_in_hbm.at[:, pl.ds(lo, tf)], wg_buf.at[slot], sems.at[0, slot]),
            pltpu.make_async_copy(w_in_hbm.at[:, pl.ds(d_ff + lo, tf)], wu_buf.at[slot], sems.at[1, slot]),
            pltpu.make_async_copy(w_out_hbm.at[pl.ds(lo, tf), :], wo_buf.at[slot], sems.at[2, slot]),
        )

    @pl.when(j == 0)
    def _prime():
        for step in range(lookahead):
            for c in slice_copies(step, step):
                c.start()

    @pl.when(j + lookahead < nj)
    def _prefetch():
        for c in slice_copies(j + lookahead, lax.rem(j + lookahead, n_slots)):
            c.start()

    slot = lax.rem(j, n_slots)
    for c in slice_copies(j, slot):
        c.wait()

    def weights():
        wg = wg_buf[slot].astype(BF16)
        wu = wu_buf[slot].astype(BF16)
        wo = wo_buf[slot].astype(BF16)
        wg_bf_ref[...] = wg
        wu_bf_ref[...] = wu
        wo_bf_ref[...] = wo
        return wg, wu, wo

    _ffn_step(j, nj, x_ref, gain_ref, weights, None, o_ref, n_ref, False)


def _ffn_tail_step(x_hbm, n_ref, x_buf, x_sem, n_casts, *refs):
    gain_ref, wg_ref, wu_ref, wo_ref, nw_in_ref = refs[:5]
    cast_in_refs = refs[5:5 + n_casts]
    o_ref, nwg_bf_ref, nwu_bf_ref = refs[5 + n_casts:8 + n_casts]
    cast_out_refs = refs[8 + n_casts:]
    i = pl.program_id(0)
    j = pl.program_id(1)

    nd_ff = nwg_bf_ref.shape[1]
    nwg_bf_ref[...] = nw_in_ref[:, :nd_ff].astype(BF16)
    nwu_bf_ref[...] = nw_in_ref[:, nd_ff:].astype(BF16)
    for src_ref, dst_ref in zip(cast_in_refs, cast_out_refs):
        dst_ref[...] = src_ref[...].astype(BF16)

    @pl.when((j == 1) & (i + 1 < pl.num_programs(0)))
    def _prefetch_x():
        _x_tile_copy(x_hbm, x_buf, x_sem, i + 2).start()

    @pl.when(j == 0)
    def _wait_x():
        _x_tile_copy(x_hbm, x_buf, x_sem, i + 1).wait()

    _ffn_step(j, pl.num_programs(1), x_buf, gain_ref,
              lambda: (wg_ref[...], wu_ref[...], wo_ref[...]), None, o_ref, n_ref, False)


def _x_tile_copy(x_hbm, x_buf, x_sem, tile):
    tm = x_buf.shape[0]
    rows = pl.ds(pl.multiple_of(tile * tm, tm), tm)
    return pltpu.make_async_copy(x_hbm.at[rows, :], x_buf, x_sem)


def _ffn_bf16_body(x_hbm, gain_ref, wg_ref, wu_ref, wo_ref, fgain_ref, o_ref, n_ref, x_buf, x_sem,
                   *, final_norm, last_width):
    i = pl.program_id(0)
    j = pl.program_id(1)
    tm = x_buf.shape[0]

    def x_copy(tile):
        rows = pl.ds(pl.multiple_of(tile * tm, tm), tm)
        return pltpu.make_async_copy(x_hbm.at[rows, :], x_buf, x_sem)

    @pl.when((i == 0) & (j == 0))
    def _first_x():
        x_copy(0).start()

    @pl.when((j == 1) & (i + 1 < pl.num_programs(0)))
    def _prefetch_x():
        x_copy(i + 1).start()

    @pl.when(j == 0)
    def _wait_x():
        x_copy(i).wait()

    last_weights = None
    if last_width is not None:
        last_weights = lambda: (wg_ref[:, :last_width], wu_ref[:, :last_width], wo_ref[:last_width, :])
    _ffn_step(j, pl.num_programs(1), x_buf, gain_ref,
              lambda: (wg_ref[...], wu_ref[...], wo_ref[...]), fgain_ref, o_ref, n_ref, final_norm,
              last_weights=last_weights)


def _cast_row_block(rows, steps):
    for rb in range(BF16_SUBLANES, rows + 1, BF16_SUBLANES):
        if rows % rb == 0 and rows // rb <= steps:
            return rb
    raise ValueError(f"cannot cast {rows} rows in {steps} steps")


def _ffn_first(x, gain, w_in, w_out, next_w_in, casts, *, tm=1024, tf_head=256, tf_tail=512):
    t, d = x.shape
    d_ff = w_out.shape[0]
    assert t % tm == 0 and t // tm >= 2 and w_in.shape == (d, 2 * d_ff)
    assert d_ff % tf_head == 0 and d_ff % tf_tail == 0 and d_ff // tf_tail >= 2
    nj = d_ff // tf_head
    vec = pl.BlockSpec((1, d), lambda j: (0, 0))
    h_head, wg_bf, wu_bf, wo_bf = pl.pallas_call(
        _ffn_head_body,
        grid=(nj,),
        in_specs=[
            pl.BlockSpec((tm, d), lambda j: (0, 0), pipeline_mode=pl.Buffered(1)),
            vec,
            pl.BlockSpec(memory_space=pl.ANY),
            pl.BlockSpec(memory_space=pl.ANY),
        ],
        out_specs=[
            pl.BlockSpec((tm, d), lambda j: (0, 0)),
            pl.BlockSpec((d, tf_head), lambda j: (0, j)),
            pl.BlockSpec((d, tf_head), lambda j: (0, j)),
            pl.BlockSpec((tf_head, d), lambda j: (j, 0)),
        ],
        out_shape=[
            jax.ShapeDtypeStruct((tm, d), F32),
            jax.ShapeDtypeStruct((d, d_ff), BF16),
            jax.ShapeDtypeStruct((d, d_ff), BF16),
            jax.ShapeDtypeStruct((d_ff, d), BF16),
        ],
        scratch_shapes=[
            pltpu.VMEM((tm, d), BF16),
            pltpu.VMEM((HEAD_WEIGHT_SLOTS, d, tf_head), F32),
            pltpu.VMEM((HEAD_WEIGHT_SLOTS, d, tf_head), F32),
            pltpu.VMEM((HEAD_WEIGHT_SLOTS, tf_head, d), F32),
            pltpu.SemaphoreType.DMA((3, HEAD_WEIGHT_SLOTS)),
        ],
        compiler_params=_params(("arbitrary",)),
        name="ffn_head",
    )(x, gain, w_in, w_out)

    n_tiles = t // tm - 1
    nj_tail = d_ff // tf_tail
    steps = n_tiles * nj_tail
    nd, nd_ff2 = next_w_in.shape
    nd_ff = nd_ff2 // 2
    assert nd_ff % LANES == 0
    rb_in = _cast_row_block(nd, steps)
    step_block = lambda n_blocks: (lambda i, j: (jnp.minimum(i * nj_tail + j, n_blocks - 1), 0))
    cast_specs, cast_shapes = [], []
    for mat, rows in casts:
        rb = _cast_row_block(rows, steps)
        cast_specs.append(pl.BlockSpec((rb, mat.shape[1]), step_block(rows // rb)))
        cast_shapes.append(jax.ShapeDtypeStruct((rows, mat.shape[1]), BF16))
    cast_in = step_block(nd // rb_in)
    n_casts = len(casts)
    in_specs = [
        pl.BlockSpec((1, d), lambda i, j: (0, 0)),
        pl.BlockSpec((d, tf_tail), lambda i, j: (0, j)),
        pl.BlockSpec((d, tf_tail), lambda i, j: (0, j)),
        pl.BlockSpec((tf_tail, d), lambda i, j: (j, 0)),
        pl.BlockSpec((rb_in, nd_ff2), cast_in),
        *cast_specs,
    ]
    out_specs = [
        pl.BlockSpec((tm, d), lambda i, j: (i + 1, 0)),
        pl.BlockSpec((rb_in, nd_ff), cast_in),
        pl.BlockSpec((rb_in, nd_ff), cast_in),
        *cast_specs,
    ]

    def tail_body(x_hbm, gain_hbm, wg_hbm, wu_hbm, wo_hbm, head_hbm, *rest):
        streamed_in = rest[:1 + n_casts]
        o_hbm = rest[1 + n_casts]
        streamed_out = rest[1 + n_casts:4 + 2 * n_casts]
        n_ref, x_buf, x_sem, head_sem = rest[4 + 2 * n_casts:]
        head_rows = pltpu.make_async_copy(head_hbm, o_hbm.at[pl.ds(0, tm), :], head_sem)
        _x_tile_copy(x_hbm, x_buf, x_sem, 1).start()
        head_rows.start()
        pltpu.emit_pipeline(
            functools.partial(_ffn_tail_step, x_hbm, n_ref, x_buf, x_sem, n_casts),
            grid=(n_tiles, nj_tail),
            in_specs=in_specs,
            out_specs=out_specs,
            dimension_semantics=(pltpu.ARBITRARY, pltpu.ARBITRARY),
        )(gain_hbm, wg_hbm, wu_hbm, wo_hbm, *streamed_in, *streamed_out)
        head_rows.wait()

    outs = pl.pallas_call(
        tail_body,
        in_specs=[pl.BlockSpec(memory_space=pl.ANY)] * (7 + n_casts),
        out_specs=[pl.BlockSpec(memory_space=pl.ANY)] * (3 + n_casts),
        out_shape=[
            jax.ShapeDtypeStruct((t, d), F32),
            jax.ShapeDtypeStruct((nd, nd_ff), BF16),
            jax.ShapeDtypeStruct((nd, nd_ff), BF16),
            *cast_shapes,
        ],
        scratch_shapes=[
            pltpu.VMEM((tm, d), BF16),
            pltpu.VMEM((tm, d), F32),
            pltpu.SemaphoreType.DMA(()),
            pltpu.SemaphoreType.DMA(()),
        ],
        compiler_params=pltpu.CompilerParams(vmem_limit_bytes=VMEM_LIMIT_BYTES),
        name="ffn_tail",
    )(x, gain, wg_bf, wu_bf, wo_bf, h_head, next_w_in, *[mat for mat, _ in casts])
    return outs[0], outs[1], outs[2], list(outs[3:])


def _ffn_second(x, gain, wg_bf, wu_bf, wo_bf, final_gain, *, final_norm, tm=1024, tf=768):
    t, d = x.shape
    d_ff = wo_bf.shape[0]
    nj = pl.cdiv(d_ff, tf)
    last_width = d_ff - (nj - 1) * tf
    assert t % tm == 0 and nj >= 2 and wg_bf.shape == wu_bf.shape == (d, d_ff)
    assert last_width % LANES == 0
    vec = pl.BlockSpec((1, d), lambda i, j: (0, 0))
    return pl.pallas_call(
        functools.partial(_ffn_bf16_body, final_norm=final_norm,
                          last_width=None if last_width == tf else last_width),
        grid=(t // tm, nj),
        in_specs=[
            pl.BlockSpec(memory_space=pl.ANY),
            vec,
            pl.BlockSpec((d, tf), lambda i, j: (0, j)),
            pl.BlockSpec((d, tf), lambda i, j: (0, j)),
            pl.BlockSpec((tf, d), lambda i, j: (j, 0)),
            vec,
        ],
        out_specs=pl.BlockSpec((tm, d), lambda i, j: (i, 0)),
        out_shape=jax.ShapeDtypeStruct((t, d), F32),
        scratch_shapes=[
            pltpu.VMEM((tm, d), BF16),
            pltpu.VMEM((tm, d), F32),
            pltpu.SemaphoreType.DMA(()),
        ],
        compiler_params=_params(("arbitrary", "arbitrary")),
        name="ffn_second_final" if final_norm else "ffn_second",
    )(x, gain, wg_bf, wu_bf, wo_bf, final_gain)


def _pool_group(x, halo, window, w, scale, t_first):
    tt = x.shape[0]
    ext = jnp.concatenate([halo, x], axis=0)
    row = lax.broadcasted_iota(jnp.int32, ext.shape, 0)
    acc = ext
    span = 1
    while span < window:
        acc = acc + jnp.where(row >= span, pltpu.roll(acc, span, axis=0), 0.0)
        span *= 2
    cnt = jnp.clip(row + (t_first + 1 - POOL_HALO), 1, window).astype(F32)
    pooled = (acc / cnt - ext)[POOL_HALO:, :].astype(BF16)
    return _dot(pooled, w) * scale


def _mixer_body(h_ref, gain_ref, w_ref, wgate_ref, wpool_ref, pscale_ref, wa_ref, ba_ref, gn_ref,
                wout_ref, o_ref,
                y_ref, qd_s, ki_s, kt_s, bc_s, v_s, sg_s, st_s, halo_s,
                *, nt, wc, d_pool, dk, dv):
    s = pl.program_id(0)
    tt = h_ref.shape[0]
    heads = GLA_HEADS
    group_dim = d_pool // N_POOL_GROUPS
    pool_chunks = d_pool // wc
    q_chunk = pool_chunks
    k_chunk = q_chunk + 1
    v_chunk0 = k_chunk + 1
    v_chunks = heads * dv // wc
    g_chunk0 = v_chunk0 + v_chunks

    def w_chunk(c):
        return w_ref[c * wc:(c + 1) * wc, :]

    t = lax.rem(s, nt)

    @pl.when(t == 0)
    def _reset():
        st_s[...] = jnp.zeros_like(st_s)
        halo_s[...] = jnp.zeros_like(halo_s)

    n = _rmsnorm(h_ref[...], gain_ref[...]).astype(BF16)

    per_chunk = wc // group_dim
    for c in range(pool_chunks):
        u_c = _dot_nt(n, w_chunk(c))
        for part in range(per_chunk):
            gi = c * per_chunk + part
            cols = slice(gi * group_dim, (gi + 1) * group_dim)
            x = u_c[:, part * group_dim:(part + 1) * group_dim]
            y = _pool_group(x, halo_s[:, cols], POOL_WINDOWS[gi], wpool_ref[gi].astype(BF16),
                            pscale_ref[:, cols], t * tt)
            halo_s[:, cols] = x[tt - POOL_HALO:, :]
            y_ref[:, cols] = y.astype(y_ref.dtype)

    gate = _dot_nt(n, wgate_ref[...])
    logits = _dot(gate.astype(BF16), wa_ref[...]) + ba_ref[...]
    log_alpha = _log_sigmoid(logits) * (1.0 / GATE_LOGIT_NORMALIZER)
    pos = lax.broadcasted_iota(jnp.int32, log_alpha.shape, 0) & (CHUNK - 1)
    bcum = log_alpha
    span = 1
    while span < CHUNK:
        bcum = bcum + jnp.where(pos >= span, pltpu.roll(bcum, span, axis=0), 0.0)
        span *= 2
    bc_s[...] = bcum
    n_chunks = tt // CHUNK
    b3 = bcum.reshape(n_chunks, CHUNK, heads * dk)
    b_last = jnp.broadcast_to(b3[:, CHUNK - 1:CHUNK, :], b3.shape).reshape(tt, heads * dk)
    q = _dot_nt(n, w_chunk(q_chunk))
    qd_s[...] = (q * (dk ** -0.5) * jnp.exp(bcum)).astype(BF16)
    k = _dot_nt(n, w_chunk(k_chunk))
    ki_s[...] = (k * jnp.exp(-bcum)).astype(BF16)
    kt_s[...] = (k * jnp.exp(b_last - bcum)).astype(BF16)
    for c in range(v_chunks):
        cols = slice(c * wc, (c + 1) * wc)
        v_s[:, cols] = _dot_nt(n, w_chunk(v_chunk0 + c)).astype(BF16)
        sg_s[:, cols] = _silu(_dot_nt(n, w_chunk(g_chunk0 + c)))

    sb_rows = SB_CHUNKS * CHUNK
    r_idx = lax.broadcasted_iota(jnp.int32, (sb_rows, sb_rows), 0)
    c_idx = lax.broadcasted_iota(jnp.int32, (sb_rows, sb_rows), 1)
    same_chunk_causal = ((r_idx // CHUNK) == (c_idx // CHUNK)) & (r_idx >= c_idx)

    def superblock(sb, carry):
        row0 = pl.multiple_of(sb * sb_rows, sb_rows)
        rows = pl.ds(row0, sb_rows)
        decays = [jnp.exp(bc_s[pl.ds(row0 + (c * CHUNK + CHUNK - 1), 1), :]) for c in range(SB_CHUNKS)]
        for hd in range(heads):
            kc = slice(hd * dk, (hd + 1) * dk)
            vc = slice(hd * dv, (hd + 1) * dv)
            qd = qd_s[rows, kc]
            kt = kt_s[rows, kc]
            v = v_s[rows, vc]
            scores = jnp.where(same_chunk_causal, _dot_nt(qd, ki_s[rows, kc]), 0.0).astype(BF16)
            o = _dot(scores, v)
            state_t = st_s[hd]
            inter = []
            for c in range(SB_CHUNKS):
                cr = slice(c * CHUNK, (c + 1) * CHUNK)
                inter.append(_dot_nt(qd[cr], state_t.astype(BF16)))
                state_t = state_t * decays[c][:, kc] + _dot_tn(v[cr], kt[cr])
            st_s[hd] = state_t
            o = _rmsnorm(o + jnp.concatenate(inter, axis=0), gn_ref[...])
            y_ref[rows, d_pool + hd * dv:d_pool + (hd + 1) * dv] = (o * sg_s[rows, vc]).astype(y_ref.dtype)
        return carry

    lax.fori_loop(0, tt // sb_rows, superblock, 0)

    o_ref[...] = h_ref[...] + _dot(y_ref[...], wout_ref[...])


def _mixer(h, gain, w_in_t_bf, w_gate_t, w_pool, pool_scale, w_alpha, b_alpha, gla_norm, w_out_bf, *,
           seq, d_pool, dk, dv, tt=512, wc=512):
    t, d = h.shape
    heads = GLA_HEADS
    d_gla = heads * dv
    n_main = d_pool + 2 * heads * dk + 2 * d_gla
    assert w_in_t_bf.shape == (n_main, d)
    nt = seq // tt
    n_tiles = t // tt
    group_dim = d_pool // N_POOL_GROUPS
    assert seq % tt == 0 and tt % (SB_CHUNKS * CHUNK) == 0 and n_main % wc == 0 and heads * dk == wc
    assert d_pool % wc == 0 and d_gla % wc == 0 and wc % group_dim == 0
    const = lambda s: (0, 0)
    in_specs = [
        pl.BlockSpec((tt, d), lambda s: (s, 0)),
        pl.BlockSpec((1, d), const),
        pl.BlockSpec((n_main, d), const, pipeline_mode=pl.Buffered(1)),
        pl.BlockSpec((LANES, d), const),
        pl.BlockSpec((N_POOL_GROUPS, group_dim, group_dim), lambda s: (0, 0, 0)),
        pl.BlockSpec((1, d_pool), const),
        pl.BlockSpec((LANES, heads * dk), const),
        pl.BlockSpec((1, heads * dk), const),
        pl.BlockSpec((1, dv), const),
        pl.BlockSpec((d_pool + d_gla, d), const, pipeline_mode=pl.Buffered(1)),
    ]
    n_in = len(in_specs)

    def body(*refs):
        streamed, scratch = refs[:n_in + 1], refs[n_in + 1:]
        pltpu.emit_pipeline(
            lambda *blocks: _mixer_body(*blocks, *scratch, nt=nt, wc=wc, d_pool=d_pool, dk=dk, dv=dv),
            grid=(n_tiles,),
            in_specs=in_specs,
            out_specs=[pl.BlockSpec((tt, d), lambda s: (s, 0))],
            dimension_semantics=(pltpu.ARBITRARY,),
        )(*streamed)

    return pl.pallas_call(
        body,
        in_specs=[pl.BlockSpec(memory_space=pl.ANY)] * n_in,
        out_specs=pl.BlockSpec(memory_space=pl.ANY),
        out_shape=jax.ShapeDtypeStruct((t, d), F32),
        scratch_shapes=[
            pltpu.VMEM((tt, d_pool + d_gla), BF16),
            pltpu.VMEM((tt, heads * dk), BF16),
            pltpu.VMEM((tt, heads * dk), BF16),
            pltpu.VMEM((tt, heads * dk), BF16),
            pltpu.VMEM((tt, heads * dk), F32),
            pltpu.VMEM((tt, d_gla), BF16),
            pltpu.VMEM((tt, d_gla), F32),
            pltpu.VMEM((heads, dv, dk), F32),
            pltpu.VMEM((POOL_HALO, d_pool), F32),
        ],
        compiler_params=pltpu.CompilerParams(vmem_limit_bytes=VMEM_LIMIT_BYTES),
        name="mixer",
    )(h, gain, w_in_t_bf, w_gate_t, w_pool, pool_scale, w_alpha, b_alpha, gla_norm, w_out_bf)


def kernel(x, ffn1_norm, ffn1_w_in, ffn1_w_out, mix_norm, w_in_mix, w_pool, pool_scale,
           w_alpha, b_alpha, gla_norm, w_out_mix, ffn2_norm, ffn2_w_in, ffn2_w_out, final_norm):
    b, s, d = x.shape
    depth = ffn1_norm.shape[0]
    d_pool = pool_scale.shape[1]
    dk_total = w_alpha.shape[2]
    dk = dk_total // GLA_HEADS
    dv = gla_norm.shape[1]
    d_gla = GLA_HEADS * dv
    n_main = d_pool + 2 * dk_total + 2 * d_gla
    assert w_in_mix.shape[2] == n_main + GLA_GATE_RANK and s % CHUNK == 0 and depth >= 1

    h = x.reshape(b * s, d)
    final_gain = final_norm.reshape(1, d)
    for l in range(depth):
        last = l == depth - 1
        w_mix_t = jnp.swapaxes(w_in_mix[l], 0, 1)
        h, wg2_bf, wu2_bf, (wo2_bf, w_mix_t_bf, w_out_mix_bf) = _ffn_first(
            h, ffn1_norm[l].reshape(1, d), ffn1_w_in[l], ffn1_w_out[l], ffn2_w_in[l],
            [(ffn2_w_out[l], ffn2_w_out.shape[1]), (w_mix_t, n_main), (w_out_mix[l], w_out_mix.shape[1])])
        w_gate_t = jnp.pad(w_mix_t[n_main:, :], ((0, LANES - GLA_GATE_RANK), (0, 0))).astype(BF16)
        w_alpha_pad = jnp.pad(w_alpha[l], ((0, LANES - GLA_GATE_RANK), (0, 0))).astype(BF16)
        h = _mixer(h, mix_norm[l].reshape(1, d), w_mix_t_bf, w_gate_t, w_pool[l],
                   pool_scale[l].reshape(1, d_pool), w_alpha_pad, b_alpha[l].reshape(1, dk_total),
                   gla_norm[l].reshape(1, dv), w_out_mix_bf, seq=s, d_pool=d_pool, dk=dk, dv=dv)

        h = _ffn_second(h, ffn2_norm[l].reshape(1, d), wg2_bf, wu2_bf, wo2_bf, final_gain,
                        final_norm=last)
    return h.reshape(b, s, d)
```

```python
import functools

import jax
import jax.numpy as jnp
from jax import lax
from jax.experimental import pallas as pl
from jax.experimental.pallas import tpu as pltpu

F32 = jnp.float32
BF16 = jnp.bfloat16

EPS = 1e-6
POOL_WINDOWS = (2, 4, 8, 16)
N_POOL_GROUPS = len(POOL_WINDOWS)
POOL_HALO = 16
GLA_HEADS = 4
GLA_GATE_RANK = 16
GATE_LOGIT_NORMALIZER = 16.0
CHUNK = 64
SB_CHUNKS = 4
HEAD_WEIGHT_SLOTS = 3

LANES = 128
BF16_SUBLANES = 16
VMEM_LIMIT_BYTES = 58 * 1024 * 1024


def _rmsnorm(x, gain):
    ms = jnp.mean(x * x, axis=-1, keepdims=True)
    return x * lax.rsqrt(ms + EPS) * gain


def _silu(x):
    return x * (1.0 / (1.0 + jnp.exp(-x)))


def _log_sigmoid(z):
    return jnp.minimum(z, 0.0) - jnp.log1p(jnp.exp(-jnp.abs(z)))


def _dot(a, b):
    return jnp.dot(a, b, preferred_element_type=F32)


def _dot_nt(a, b):
    return lax.dot_general(a, b, (((1,), (1,)), ((), ())), preferred_element_type=F32)


def _dot_tn(a, b):
    return lax.dot_general(a, b, (((0,), (0,)), ((), ())), preferred_element_type=F32)


def _params(semantics):
    return pltpu.CompilerParams(dimension_semantics=semantics, vmem_limit_bytes=VMEM_LIMIT_BYTES)


def _ffn_step(j, nj, x_ref, gain_ref, weights, fgain_ref, o_ref, n_ref, final_norm,
              last_weights=None):
    def slice_update(n, get_weights):
        wg, wu, wo = get_weights()
        gate = _dot(n, wg)
        up = _dot(n, wu)
        act = (0.5 * _silu(gate) * up).astype(BF16)
        return _dot(act, wo)

    @pl.when(j == 0)
    def _first():
        x = x_ref[...]
        n = _rmsnorm(x, gain_ref[...]).astype(BF16)
        n_ref[...] = n
        o_ref[...] = x + slice_update(n, weights)

    if last_weights is None:

        @pl.when(j > 0)
        def _rest():
            o_ref[...] += slice_update(n_ref[...], weights)

    else:

        @pl.when((j > 0) & (j < nj - 1))
        def _middle():
            o_ref[...] += slice_update(n_ref[...], weights)

        @pl.when(j == nj - 1)
        def _last():
            o_ref[...] += slice_update(n_ref[...], last_weights)

    if final_norm:

        @pl.when(j == nj - 1)
        def _finish():
            o_ref[...] = _rmsnorm(o_ref[...], fgain_ref[...])


def _ffn_head_body(x_ref, gain_ref, w_in_hbm, w_out_hbm,
                   o_ref, wg_bf_ref, wu_bf_ref, wo_bf_ref,
                   n_ref, wg_buf, wu_buf, wo_buf, sems):
    j = pl.program_id(0)
    nj = pl.num_programs(0)
    n_slots, _, tf = wg_buf.shape
    d_ff = w_out_hbm.shape[0]
    lookahead = n_slots - 1

    def slice_copies(step, slot):
        lo = pl.multiple_of(step * tf, tf)
        return (
            pltpu.make_async_copy(w_in_hbm.at[:, pl.ds(lo, tf)], wg_buf.at[slot], sems.at[0, slot]),
            pltpu.make_async_copy(w_in_hbm.at[:, pl.ds(d_ff + lo, tf)], wu_buf.at[slot], sems.at[1, slot]),
            pltpu.make_async_copy(w_out_hbm.at[pl.ds(lo, tf), :], wo_buf.at[slot], sems.at[2, slot]),
        )

    @pl.when(j == 0)
    def _prime():
        for step in range(lookahead):
            for c in slice_copies(step, step):
                c.start()

    @pl.when(j + lookahead < nj)
    def _prefetch():
        for c in slice_copies(j + lookahead, lax.rem(j + lookahead, n_slots)):
            c.start()

    slot = lax.rem(j, n_slots)
    for c in slice_copies(j, slot):
        c.wait()

    def weights():
        wg = wg_buf[slot].astype(BF16)
        wu = wu_buf[slot].astype(BF16)
        wo = wo_buf[slot].astype(BF16)
        wg_bf_ref[...] = wg
        wu_bf_ref[...] = wu
        wo_bf_ref[...] = wo
        return wg, wu, wo

    _ffn_step(j, nj, x_ref, gain_ref, weights, None, o_ref, n_ref, False)


def _ffn_tail_step(x_hbm, n_ref, x_buf, x_sem, n_casts, *refs):
    gain_ref, wg_ref, wu_ref, wo_ref, nw_in_ref = refs[:5]
    cast_in_refs = refs[5:5 + n_casts]
    o_ref, nwg_bf_ref, nwu_bf_ref = refs[5 + n_casts:8 + n_casts]
    cast_out_refs = refs[8 + n_casts:]
    i = pl.program_id(0)
    j = pl.program_id(1)

    nd_ff = nwg_bf_ref.shape[1]
    nwg_bf_ref[...] = nw_in_ref[:, :nd_ff].astype(BF16)
    nwu_bf_ref[...] = nw_in_ref[:, nd_ff:].astype(BF16)
    for src_ref, dst_ref in zip(cast_in_refs, cast_out_refs):
        dst_ref[...] = src_ref[...].astype(BF16)

    @pl.when((j == 1) & (i + 1 < pl.num_programs(0)))
    def _prefetch_x():
        _x_tile_copy(x_hbm, x_buf, x_sem, i + 2).start()

    @pl.when(j == 0)
    def _wait_x():
        _x_tile_copy(x_hbm, x_buf, x_sem, i + 1).wait()

    _ffn_step(j, pl.num_programs(1), x_buf, gain_ref,
              lambda: (wg_ref[...], wu_ref[...], wo_ref[...]), None, o_ref, n_ref, False)


def _x_tile_copy(x_hbm, x_buf, x_sem, tile):
    tm = x_buf.shape[0]
    rows = pl.ds(pl.multiple_of(tile * tm, tm), tm)
    return pltpu.make_async_copy(x_hbm.at[rows, :], x_buf, x_sem)


def _ffn_bf16_body(x_hbm, gain_ref, wg_ref, wu_ref, wo_ref, fgain_ref, o_ref, n_ref, x_buf, x_sem,
                   *, final_norm, last_width):
    i = pl.program_id(0)
    j = pl.program_id(1)
    tm = x_buf.shape[0]

    def x_copy(tile):
        rows = pl.ds(pl.multiple_of(tile * tm, tm), tm)
        return pltpu.make_async_copy(x_hbm.at[rows, :], x_buf, x_sem)

    @pl.when((i == 0) & (j == 0))
    def _first_x():
        x_copy(0).start()

    @pl.when((j == 1) & (i + 1 < pl.num_programs(0)))
    def _prefetch_x():
        x_copy(i + 1).start()

    @pl.when(j == 0)
    def _wait_x():
        x_copy(i).wait()

    last_weights = None
    if last_width is not None:
        last_weights = lambda: (wg_ref[:, :last_width], wu_ref[:, :last_width], wo_ref[:last_width, :])
    _ffn_step(j, pl.num_programs(1), x_buf, gain_ref,
              lambda: (wg_ref[...], wu_ref[...], wo_ref[...]), fgain_ref, o_ref, n_ref, final_norm,
              last_weights=last_weights)


def _cast_row_block(rows, steps):
    for rb in range(BF16_SUBLANES, rows + 1, BF16_SUBLANES):
        if rows % rb == 0 and rows // rb <= steps:
            return rb
    raise ValueError(f"cannot cast {rows} rows in {steps} steps")


def _ffn_first(x, gain, w_in, w_out, next_w_in, casts, *, tm=1024, tf_head=256, tf_tail=512):
    t, d = x.shape
    d_ff = w_out.shape[0]
    assert t % tm == 0 and t // tm >= 2 and w_in.shape == (d, 2 * d_ff)
    assert d_ff % tf_head == 0 and d_ff % tf_tail == 0 and d_ff // tf_tail >= 2
    nj_head = d_ff // tf_head
    head_in_specs = [pl.BlockSpec((1, d), lambda j: (0, 0))]
    head_out_specs = [
        pl.BlockSpec((tm, d), lambda j: (0, 0)),
        pl.BlockSpec((d, tf_head), lambda j: (0, j)),
        pl.BlockSpec((d, tf_head), lambda j: (0, j)),
        pl.BlockSpec((tf_head, d), lambda j: (j, 0)),
    ]

    n_tiles = t // tm - 1
    nj_tail = d_ff // tf_tail
    steps = n_tiles * nj_tail
    nd, nd_ff2 = next_w_in.shape
    nd_ff = nd_ff2 // 2
    assert nd_ff % LANES == 0
    rb_in = _cast_row_block(nd, steps)
    step_block = lambda n_blocks: (lambda i, j: (jnp.minimum(i * nj_tail + j, n_blocks - 1), 0))
    cast_specs, cast_shapes = [], []
    for mat, rows in casts:
        rb = _cast_row_block(rows, steps)
        cast_specs.append(pl.BlockSpec((rb, mat.shape[1]), step_block(rows // rb)))
        cast_shapes.append(jax.ShapeDtypeStruct((rows, mat.shape[1]), BF16))
    cast_in = step_block(nd // rb_in)
    n_casts = len(casts)
    in_specs = [
        pl.BlockSpec((1, d), lambda i, j: (0, 0)),
        pl.BlockSpec((d, tf_tail), lambda i, j: (0, j)),
        pl.BlockSpec((d, tf_tail), lambda i, j: (0, j)),
        pl.BlockSpec((tf_tail, d), lambda i, j: (j, 0)),
        pl.BlockSpec((rb_in, nd_ff2), cast_in),
        *cast_specs,
    ]
    out_specs = [
        pl.BlockSpec((tm, d), lambda i, j: (i + 1, 0)),
        pl.BlockSpec((rb_in, nd_ff), cast_in),
        pl.BlockSpec((rb_in, nd_ff), cast_in),
        *cast_specs,
    ]

    def body(x_hbm, gain_hbm, w_in_hbm, w_out_hbm, *rest):
        streamed_in = rest[:1 + n_casts]
        o_hbm, wg_hbm, wu_hbm, wo_hbm = rest[1 + n_casts:5 + n_casts]
        streamed_out = rest[5 + n_casts:7 + 2 * n_casts]
        n_ref, x_buf, x_sem = rest[7 + 2 * n_casts:]

        def head_phase(wg_buf, wu_buf, wo_buf, sems):
            def head_step(gain_ref, o_ref, wg_bf_ref, wu_bf_ref, wo_bf_ref):
                @pl.when(pl.program_id(0) == 1)
                def _prefetch_x():
                    _x_tile_copy(x_hbm, x_buf, x_sem, 1).start()

                _ffn_head_body(x_buf, gain_ref, w_in_hbm, w_out_hbm,
                               o_ref, wg_bf_ref, wu_bf_ref, wo_bf_ref,
                               n_ref, wg_buf, wu_buf, wo_buf, sems)

            pltpu.emit_pipeline(
                head_step,
                grid=(nj_head,),
                in_specs=head_in_specs,
                out_specs=head_out_specs,
                dimension_semantics=(pltpu.ARBITRARY,),
            )(gain_hbm, o_hbm, wg_hbm, wu_hbm, wo_hbm)

        tile0 = _x_tile_copy(x_hbm, x_buf, x_sem, 0)
        tile0.start()
        tile0.wait()
        pl.run_scoped(
            head_phase,
            pltpu.VMEM((HEAD_WEIGHT_SLOTS, d, tf_head), F32),
            pltpu.VMEM((HEAD_WEIGHT_SLOTS, d, tf_head), F32),
            pltpu.VMEM((HEAD_WEIGHT_SLOTS, tf_head, d), F32),
            pltpu.SemaphoreType.DMA((3, HEAD_WEIGHT_SLOTS)),
        )
        pltpu.emit_pipeline(
            functools.partial(_ffn_tail_step, x_hbm, n_ref, x_buf, x_sem, n_casts),
            grid=(n_tiles, nj_tail),
            in_specs=in_specs,
            out_specs=out_specs,
            dimension_semantics=(pltpu.ARBITRARY, pltpu.ARBITRARY),
        )(gain_hbm, wg_hbm, wu_hbm, wo_hbm, *streamed_in, o_hbm, *streamed_out)

    outs = pl.pallas_call(
        body,
        in_specs=[pl.BlockSpec(memory_space=pl.ANY)] * (5 + n_casts),
        out_specs=[pl.BlockSpec(memory_space=pl.ANY)] * (6 + n_casts),
        out_shape=[
            jax.ShapeDtypeStruct((t, d), F32),
            jax.ShapeDtypeStruct((d, d_ff), BF16),
            jax.ShapeDtypeStruct((d, d_ff), BF16),
            jax.ShapeDtypeStruct((d_ff, d), BF16),
            jax.ShapeDtypeStruct((nd, nd_ff), BF16),
            jax.ShapeDtypeStruct((nd, nd_ff), BF16),
            *cast_shapes,
        ],
        scratch_shapes=[
            pltpu.VMEM((tm, d), BF16),
            pltpu.VMEM((tm, d), F32),
            pltpu.SemaphoreType.DMA(()),
        ],
        compiler_params=pltpu.CompilerParams(vmem_limit_bytes=VMEM_LIMIT_BYTES),
        name="ffn_first",
    )(x, gain, w_in, w_out, next_w_in, *[mat for mat, _ in casts])
    return outs[0], outs[4], outs[5], list(outs[6:])


def _ffn_second(x, gain, wg_bf, wu_bf, wo_bf, final_gain, *, final_norm, tm=1024, tf=768):
    t, d = x.shape
    d_ff = wo_bf.shape[0]
    nj = pl.cdiv(d_ff, tf)
    last_width = d_ff - (nj - 1) * tf
    assert t % tm == 0 and nj >= 2 and wg_bf.shape == wu_bf.shape == (d, d_ff)
    assert last_width % LANES == 0
    vec = pl.BlockSpec((1, d), lambda i, j: (0, 0))
    return pl.pallas_call(
        functools.partial(_ffn_bf16_body, final_norm=final_norm,
                          last_width=None if last_width == tf else last_width),
        grid=(t // tm, nj),
        in_specs=[
            pl.BlockSpec(memory_space=pl.ANY),
            vec,
            pl.BlockSpec((d, tf), lambda i, j: (0, j)),
            pl.BlockSpec((d, tf), lambda i, j: (0, j)),
            pl.BlockSpec((tf, d), lambda i, j: (j, 0)),
            vec,
        ],
        out_specs=pl.BlockSpec((tm, d), lambda i, j: (i, 0)),
        out_shape=jax.ShapeDtypeStruct((t, d), F32),
        scratch_shapes=[
            pltpu.VMEM((tm, d), BF16),
            pltpu.VMEM((tm, d), F32),
            pltpu.SemaphoreType.DMA(()),
        ],
        compiler_params=_params(("arbitrary", "arbitrary")),
        name="ffn_second_final" if final_norm else "ffn_second",
    )(x, gain, wg_bf, wu_bf, wo_bf, final_gain)


def _pool_group(x, halo, window, w, scale, t_first):
    tt = x.shape[0]
    ext = jnp.concatenate([halo, x], axis=0)
    row = lax.broadcasted_iota(jnp.int32, ext.shape, 0)
    acc = ext
    span = 1
    while span < window:
        acc = acc + jnp.where(row >= span, pltpu.roll(acc, span, axis=0), 0.0)
        span *= 2
    cnt = jnp.clip(row + (t_first + 1 - POOL_HALO), 1, window).astype(F32)
    pooled = (acc / cnt - ext)[POOL_HALO:, :].astype(BF16)
    return _dot(pooled, w) * scale


def _mixer_body(h_ref, gain_ref, w_ref, wgate_ref, wpool_ref, pscale_ref, wa_ref, ba_ref, gn_ref,
                wout_ref, o_ref,
                y_ref, qd_s, ki_s, kt_s, bc_s, v_s, sg_s, st_s, halo_s,
                *, nt, wc, d_pool, dk, dv):
    s = pl.program_id(0)
    tt = h_ref.shape[0]
    heads = GLA_HEADS
    group_dim = d_pool // N_POOL_GROUPS
    pool_chunks = d_pool // wc
    q_chunk = pool_chunks
    k_chunk = q_chunk + 1
    v_chunk0 = k_chunk + 1
    v_chunks = heads * dv // wc
    g_chunk0 = v_chunk0 + v_chunks

    def w_chunk(c):
        return w_ref[c * wc:(c + 1) * wc, :]

    t = lax.rem(s, nt)

    @pl.when(t == 0)
    def _reset():
        st_s[...] = jnp.zeros_like(st_s)
        halo_s[...] = jnp.zeros_like(halo_s)

    n = _rmsnorm(h_ref[...], gain_ref[...]).astype(BF16)

    per_chunk = wc // group_dim
    for c in range(pool_chunks):
        u_c = _dot_nt(n, w_chunk(c))
        for part in range(per_chunk):
            gi = c * per_chunk + part
            cols = slice(gi * group_dim, (gi + 1) * group_dim)
            x = u_c[:, part * group_dim:(part + 1) * group_dim]
            y = _pool_group(x, halo_s[:, cols], POOL_WINDOWS[gi], wpool_ref[gi].astype(BF16),
                            pscale_ref[:, cols], t * tt)
            halo_s[:, cols] = x[tt - POOL_HALO:, :]
            y_ref[:, cols] = y.astype(y_ref.dtype)

    gate = _dot_nt(n, wgate_ref[...])
    logits = _dot(gate.astype(BF16), wa_ref[...]) + ba_ref[...]
    log_alpha = _log_sigmoid(logits) * (1.0 / GATE_LOGIT_NORMALIZER)
    pos = lax.broadcasted_iota(jnp.int32, log_alpha.shape, 0) & (CHUNK - 1)
    bcum = log_alpha
    span = 1
    while span < CHUNK:
        bcum = bcum + jnp.where(pos >= span, pltpu.roll(bcum, span, axis=0), 0.0)
        span *= 2
    bc_s[...] = bcum
    n_chunks = tt // CHUNK
    b3 = bcum.reshape(n_chunks, CHUNK, heads * dk)
    b_last = jnp.broadcast_to(b3[:, CHUNK - 1:CHUNK, :], b3.shape).reshape(tt, heads * dk)
    q = _dot_nt(n, w_chunk(q_chunk))
    qd_s[...] = (q * (dk ** -0.5) * jnp.exp(bcum)).astype(BF16)
    k = _dot_nt(n, w_chunk(k_chunk))
    ki_s[...] = (k * jnp.exp(-bcum)).astype(BF16)
    kt_s[...] = (k * jnp.exp(b_last - bcum)).astype(BF16)
    for c in range(v_chunks):
        cols = slice(c * wc, (c + 1) * wc)
        v_s[:, cols] = _dot_nt(n, w_chunk(v_chunk0 + c)).astype(BF16)
        sg_s[:, cols] = _silu(_dot_nt(n, w_chunk(g_chunk0 + c)))

    sb_rows = SB_CHUNKS * CHUNK
    r_idx = lax.broadcasted_iota(jnp.int32, (sb_rows, sb_rows), 0)
    c_idx = lax.broadcasted_iota(jnp.int32, (sb_rows, sb_rows), 1)
    same_chunk_causal = ((r_idx // CHUNK) == (c_idx // CHUNK)) & (r_idx >= c_idx)

    def superblock(sb, carry):
        row0 = pl.multiple_of(sb * sb_rows, sb_rows)
        rows = pl.ds(row0, sb_rows)
        decays = [jnp.exp(bc_s[pl.ds(row0 + (c * CHUNK + CHUNK - 1), 1), :]) for c in range(SB_CHUNKS)]
        for hd in range(heads):
            kc = slice(hd * dk, (hd + 1) * dk)
            vc = slice(hd * dv, (hd + 1) * dv)
            qd = qd_s[rows, kc]
            kt = kt_s[rows, kc]
            v = v_s[rows, vc]
            scores = jnp.where(same_chunk_causal, _dot_nt(qd, ki_s[rows, kc]), 0.0).astype(BF16)
            o = _dot(scores, v)
            state_t = st_s[hd]
            inter = []
            for c in range(SB_CHUNKS):
                cr = slice(c * CHUNK, (c + 1) * CHUNK)
                inter.append(_dot_nt(qd[cr], state_t.astype(BF16)))
                state_t = state_t * decays[c][:, kc] + _dot_tn(v[cr], kt[cr])
            st_s[hd] = state_t
            o = _rmsnorm(o + jnp.concatenate(inter, axis=0), gn_ref[...])
            y_ref[rows, d_pool + hd * dv:d_pool + (hd + 1) * dv] = (o * sg_s[rows, vc]).astype(y_ref.dtype)
        return carry

    lax.fori_loop(0, tt // sb_rows, superblock, 0)

    o_ref[...] = h_ref[...] + _dot(y_ref[...], wout_ref[...])


def _mixer(h, gain, w_in_t_bf, w_gate_t, w_pool, pool_scale, w_alpha, b_alpha, gla_norm, w_out_bf, *,
           seq, d_pool, dk, dv, tt=512, wc=512):
    t, d = h.shape
    heads = GLA_HEADS
    d_gla = heads * dv
    n_main = d_pool + 2 * heads * dk + 2 * d_gla
    assert w_in_t_bf.shape == (n_main, d)
    nt = seq // tt
    n_tiles = t // tt
    group_dim = d_pool // N_POOL_GROUPS
    assert seq % tt == 0 and tt % (SB_CHUNKS * CHUNK) == 0 and n_main % wc == 0 and heads * dk == wc
    assert d_pool % wc == 0 and d_gla % wc == 0 and wc % group_dim == 0
    const = lambda s: (0, 0)
    return pl.pallas_call(
        functools.partial(_mixer_body, nt=nt, wc=wc, d_pool=d_pool, dk=dk, dv=dv),
        grid=(n_tiles,),
        in_specs=[
            pl.BlockSpec((tt, d), lambda s: (s, 0)),
            pl.BlockSpec((1, d), const),
            pl.BlockSpec((n_main, d), const, pipeline_mode=pl.Buffered(1)),
            pl.BlockSpec((LANES, d), const),
            pl.BlockSpec((N_POOL_GROUPS, group_dim, group_dim), lambda s: (0, 0, 0)),
            pl.BlockSpec((1, d_pool), const),
            pl.BlockSpec((LANES, heads * dk), const),
            pl.BlockSpec((1, heads * dk), const),
            pl.BlockSpec((1, dv), const),
            pl.BlockSpec((d_pool + d_gla, d), const, pipeline_mode=pl.Buffered(1)),
        ],
        out_specs=pl.BlockSpec((tt, d), lambda s: (s, 0)),
        out_shape=jax.ShapeDtypeStruct((t, d), F32),
        scratch_shapes=[
            pltpu.VMEM((tt, d_pool + d_gla), BF16),
            pltpu.VMEM((tt, heads * dk), BF16),
            pltpu.VMEM((tt, heads * dk), BF16),
            pltpu.VMEM((tt, heads * dk), BF16),
            pltpu.VMEM((tt, heads * dk), F32),
            pltpu.VMEM((tt, d_gla), BF16),
            pltpu.VMEM((tt, d_gla), F32),
            pltpu.VMEM((heads, dv, dk), F32),
            pltpu.VMEM((POOL_HALO, d_pool), F32),
        ],
        compiler_params=_params(("arbitrary",)),
        name="mixer",
    )(h, gain, w_in_t_bf, w_gate_t, w_pool, pool_scale, w_alpha, b_alpha, gla_norm, w_out_bf)


def kernel(x, ffn1_norm, ffn1_w_in, ffn1_w_out, mix_norm, w_in_mix, w_pool, pool_scale,
           w_alpha, b_alpha, gla_norm, w_out_mix, ffn2_norm, ffn2_w_in, ffn2_w_out, final_norm):
    b, s, d = x.shape
    depth = ffn1_norm.shape[0]
    d_pool = pool_scale.shape[1]
    dk_total = w_alpha.shape[2]
    dk = dk_total // GLA_HEADS
    dv = gla_norm.shape[1]
    d_gla = GLA_HEADS * dv
    n_main = d_pool + 2 * dk_total + 2 * d_gla
    assert w_in_mix.shape[2] == n_main + GLA_GATE_RANK and s % CHUNK == 0 and depth >= 1

    h = x.reshape(b * s, d)
    final_gain = final_norm.reshape(1, d)
    for l in range(depth):
        last = l == depth - 1
        w_mix_t = jnp.swapaxes(w_in_mix[l], 0, 1)
        h, wg2_bf, wu2_bf, (wo2_bf, w_mix_t_bf, w_out_mix_bf) = _ffn_first(
            h, ffn1_norm[l].reshape(1, d), ffn1_w_in[l], ffn1_w_out[l], ffn2_w_in[l],
            [(ffn2_w_out[l], ffn2_w_out.shape[1]), (w_mix_t, n_main), (w_out_mix[l], w_out_mix.shape[1])])
        w_gate_t = jnp.pad(w_mix_t[n_main:, :], ((0, LANES - GLA_GATE_RANK), (0, 0))).astype(BF16)
        w_alpha_pad = jnp.pad(w_alpha[l], ((0, LANES - GLA_GATE_RANK), (0, 0))).astype(BF16)
        h = _mixer(h, mix_norm[l].reshape(1, d), w_mix_t_bf, w_gate_t, w_pool[l],
                   pool_scale[l].reshape(1, d_pool), w_alpha_pad, b_alpha[l].reshape(1, dk_total),
                   gla_norm[l].reshape(1, dv), w_out_mix_bf, seq=s, d_pool=d_pool, dk=dk, dv=dv)

        h = _ffn_second(h, ffn2_norm[l].reshape(1, d), wg2_bf, wu2_bf, wo2_bf, final_gain,
                        final_norm=last)
    return h.reshape(b, s, d)
```

```python
import functools

import jax
import jax.numpy as jnp
from jax import lax
from jax.experimental import pallas as pl
from jax.experimental.pallas import tpu as pltpu

F32 = jnp.float32
BF16 = jnp.bfloat16

EPS = 1e-6
POOL_WINDOWS = (2, 4, 8, 16)
N_POOL_GROUPS = len(POOL_WINDOWS)
POOL_HALO = 16
GLA_HEADS = 4
GLA_GATE_RANK = 16
GATE_LOGIT_NORMALIZER = 16.0
CHUNK = 64
SB_CHUNKS = 4
HEAD_WEIGHT_SLOTS = 3

LANES = 128
BF16_SUBLANES = 16
VMEM_LIMIT_BYTES = 58 * 1024 * 1024


def _rmsnorm(x, gain):
    ms = jnp.mean(x * x, axis=-1, keepdims=True)
    return x * lax.rsqrt(ms + EPS) * gain


def _silu(x):
    return x * (1.0 / (1.0 + jnp.exp(-x)))


def _log_sigmoid(z):
    return jnp.minimum(z, 0.0) - jnp.log1p(jnp.exp(-jnp.abs(z)))


def _dot(a, b):
    return jnp.dot(a, b, preferred_element_type=F32)


def _dot_nt(a, b):
    return lax.dot_general(a, b, (((1,), (1,)), ((), ())), preferred_element_type=F32)


def _dot_tn(a, b):
    return lax.dot_general(a, b, (((0,), (0,)), ((), ())), preferred_element_type=F32)


def _params(semantics):
    return pltpu.CompilerParams(dimension_semantics=semantics, vmem_limit_bytes=VMEM_LIMIT_BYTES)


def _ffn_step(j, nj, x_ref, gain_ref, weights, fgain_ref, o_ref, n_ref, final_norm,
              last_weights=None):
    def slice_update(n, get_weights):
        wg, wu, wo = get_weights()
        gate = _dot(n, wg)
        up = _dot(n, wu)
        act = (0.5 * _silu(gate) * up).astype(BF16)
        return _dot(act, wo)

    @pl.when(j == 0)
    def _first():
        x = x_ref[...]
        n = _rmsnorm(x, gain_ref[...]).astype(BF16)
        n_ref[...] = n
        o_ref[...] = x + slice_update(n, weights)

    if last_weights is None:

        @pl.when(j > 0)
        def _rest():
            o_ref[...] += slice_update(n_ref[...], weights)

    else:

        @pl.when((j > 0) & (j < nj - 1))
        def _middle():
            o_ref[...] += slice_update(n_ref[...], weights)

        @pl.when(j == nj - 1)
        def _last():
            o_ref[...] += slice_update(n_ref[...], last_weights)

    if final_norm:

        @pl.when(j == nj - 1)
        def _finish():
            o_ref[...] = _rmsnorm(o_ref[...], fgain_ref[...])


def _ffn_head_body(x_ref, gain_ref, w_in_hbm, w_out_hbm,
                   o_ref, wg_bf_ref, wu_bf_ref, wo_bf_ref,
                   n_ref, wg_buf, wu_buf, wo_buf, sems, x_ready):
    j = pl.program_id(0)
    nj = pl.num_programs(0)
    n_slots, _, tf = wg_buf.shape
    d_ff = w_out_hbm.shape[0]
    lookahead = n_slots - 1

    def slice_copies(step, slot):
        lo = pl.multiple_of(step * tf, tf)
        return (
            pltpu.make_async_copy(w_in_hbm.at[:, pl.ds(lo, tf)], wg_buf.at[slot], sems.at[0, slot]),
            pltpu.make_async_copy(w_in_hbm.at[:, pl.ds(d_ff + lo, tf)], wu_buf.at[slot], sems.at[1, slot]),
            pltpu.make_async_copy(w_out_hbm.at[pl.ds(lo, tf), :], wo_buf.at[slot], sems.at[2, slot]),
        )

    @pl.when(j == 0)
    def _prime():
        for step in range(lookahead):
            for c in slice_copies(step, step):
                c.start()
        x_ready()

    @pl.when(j + lookahead < nj)
    def _prefetch():
        for c in slice_copies(j + lookahead, lax.rem(j + lookahead, n_slots)):
            c.start()

    slot = lax.rem(j, n_slots)
    for c in slice_copies(j, slot):
        c.wait()

    def weights():
        wg = wg_buf[slot].astype(BF16)
        wu = wu_buf[slot].astype(BF16)
        wo = wo_buf[slot].astype(BF16)
        wg_bf_ref[...] = wg
        wu_bf_ref[...] = wu
        wo_bf_ref[...] = wo
        return wg, wu, wo

    _ffn_step(j, nj, x_ref, gain_ref, weights, None, o_ref, n_ref, False)


def _ffn_tail_step(x_hbm, n_ref, x_buf, x_sem, n_casts, *refs):
    gain_ref, wg_ref, wu_ref, wo_ref, nw_in_ref = refs[:5]
    cast_in_refs = refs[5:5 + n_casts]
    o_ref, nwg_bf_ref, nwu_bf_ref = refs[5 + n_casts:8 + n_casts]
    cast_out_refs = refs[8 + n_casts:]
    i = pl.program_id(0)
    j = pl.program_id(1)

    nd_ff = nwg_bf_ref.shape[1]
    nwg_bf_ref[...] = nw_in_ref[:, :nd_ff].astype(BF16)
    nwu_bf_ref[...] = nw_in_ref[:, nd_ff:].astype(BF16)
    for src_ref, dst_ref in zip(cast_in_refs, cast_out_refs):
        dst_ref[...] = src_ref[...].astype(BF16)

    @pl.when((j == 1) & (i + 1 < pl.num_programs(0)))
    def _prefetch_x():
        _x_tile_copy(x_hbm, x_buf, x_sem, i + 2).start()

    @pl.when(j == 0)
    def _wait_x():
        _x_tile_copy(x_hbm, x_buf, x_sem, i + 1).wait()

    _ffn_step(j, pl.num_programs(1), x_buf, gain_ref,
              lambda: (wg_ref[...], wu_ref[...], wo_ref[...]), None, o_ref, n_ref, False)


def _x_tile_copy(x_hbm, x_buf, x_sem, tile):
    tm = x_buf.shape[0]
    rows = pl.ds(pl.multiple_of(tile * tm, tm), tm)
    return pltpu.make_async_copy(x_hbm.at[rows, :], x_buf, x_sem)


def _ffn_bf16_body(x_hbm, gain_ref, wg_ref, wu_ref, wo_ref, fgain_ref, o_ref, n_ref, x_buf, x_sem,
                   *, final_norm, last_width):
    i = pl.program_id(0)
    j = pl.program_id(1)
    tm = x_buf.shape[0]

    def x_copy(tile):
        rows = pl.ds(pl.multiple_of(tile * tm, tm), tm)
        return pltpu.make_async_copy(x_hbm.at[rows, :], x_buf, x_sem)

    @pl.when((i == 0) & (j == 0))
    def _first_x():
        x_copy(0).start()

    @pl.when((j == 1) & (i + 1 < pl.num_programs(0)))
    def _prefetch_x():
        x_copy(i + 1).start()

    @pl.when(j == 0)
    def _wait_x():
        x_copy(i).wait()

    last_weights = None
    if last_width is not None:
        last_weights = lambda: (wg_ref[:, :last_width], wu_ref[:, :last_width], wo_ref[:last_width, :])
    _ffn_step(j, pl.num_programs(1), x_buf, gain_ref,
              lambda: (wg_ref[...], wu_ref[...], wo_ref[...]), fgain_ref, o_ref, n_ref, final_norm,
              last_weights=last_weights)


def _cast_row_block(rows, steps):
    for rb in range(BF16_SUBLANES, rows + 1, BF16_SUBLANES):
        if rows % rb == 0 and rows // rb <= steps:
            return rb
    raise ValueError(f"cannot cast {rows} rows in {steps} steps")


def _ffn_first(x, gain, w_in, w_out, next_w_in, casts, *, tm=1024, tf_head=256, tf_tail=512):
    t, d = x.shape
    d_ff = w_out.shape[0]
    assert t % tm == 0 and t // tm >= 2 and w_in.shape == (d, 2 * d_ff)
    assert d_ff % tf_head == 0 and d_ff % tf_tail == 0 and d_ff // tf_tail >= 2
    nj_head = d_ff // tf_head
    head_in_specs = [pl.BlockSpec((1, d), lambda j: (0, 0))]
    head_out_specs = [
        pl.BlockSpec((tm, d), lambda j: (0, 0)),
        pl.BlockSpec((d, tf_head), lambda j: (0, j)),
        pl.BlockSpec((d, tf_head), lambda j: (0, j)),
        pl.BlockSpec((tf_head, d), lambda j: (j, 0)),
    ]

    n_tiles = t // tm - 1
    nj_tail = d_ff // tf_tail
    steps = n_tiles * nj_tail
    nd, nd_ff2 = next_w_in.shape
    nd_ff = nd_ff2 // 2
    assert nd_ff % LANES == 0
    rb_in = _cast_row_block(nd, steps)
    step_block = lambda n_blocks: (lambda i, j: (jnp.minimum(i * nj_tail + j, n_blocks - 1), 0))
    cast_specs, cast_shapes = [], []
    for mat, rows in casts:
        rb = _cast_row_block(rows, steps)
        cast_specs.append(pl.BlockSpec((rb, mat.shape[1]), step_block(rows // rb)))
        cast_shapes.append(jax.ShapeDtypeStruct((rows, mat.shape[1]), BF16))
    cast_in = step_block(nd // rb_in)
    n_casts = len(casts)
    in_specs = [
        pl.BlockSpec((1, d), lambda i, j: (0, 0)),
        pl.BlockSpec((d, tf_tail), lambda i, j: (0, j)),
        pl.BlockSpec((d, tf_tail), lambda i, j: (0, j)),
        pl.BlockSpec((tf_tail, d), lambda i, j: (j, 0)),
        pl.BlockSpec((rb_in, nd_ff2), cast_in),
        *cast_specs,
    ]
    out_specs = [
        pl.BlockSpec((tm, d), lambda i, j: (i + 1, 0)),
        pl.BlockSpec((rb_in, nd_ff), cast_in),
        pl.BlockSpec((rb_in, nd_ff), cast_in),
        *cast_specs,
    ]

    def body(x_hbm, gain_hbm, w_in_hbm, w_out_hbm, *rest):
        streamed_in = rest[:1 + n_casts]
        o_hbm, wg_hbm, wu_hbm, wo_hbm = rest[1 + n_casts:5 + n_casts]
        streamed_out = rest[5 + n_casts:7 + 2 * n_casts]
        n_ref, x_buf, x_sem = rest[7 + 2 * n_casts:]

        tile0 = _x_tile_copy(x_hbm, x_buf, x_sem, 0)

        def head_phase(wg_buf, wu_buf, wo_buf, sems):
            def head_step(gain_ref, o_ref, wg_bf_ref, wu_bf_ref, wo_bf_ref):
                @pl.when(pl.program_id(0) == 1)
                def _prefetch_x():
                    _x_tile_copy(x_hbm, x_buf, x_sem, 1).start()

                _ffn_head_body(x_buf, gain_ref, w_in_hbm, w_out_hbm,
                               o_ref, wg_bf_ref, wu_bf_ref, wo_bf_ref,
                               n_ref, wg_buf, wu_buf, wo_buf, sems, x_ready=tile0.wait)

            pltpu.emit_pipeline(
                head_step,
                grid=(nj_head,),
                in_specs=head_in_specs,
                out_specs=head_out_specs,
                dimension_semantics=(pltpu.ARBITRARY,),
            )(gain_hbm, o_hbm, wg_hbm, wu_hbm, wo_hbm)

        tile0.start()
        pl.run_scoped(
            head_phase,
            pltpu.VMEM((HEAD_WEIGHT_SLOTS, d, tf_head), F32),
            pltpu.VMEM((HEAD_WEIGHT_SLOTS, d, tf_head), F32),
            pltpu.VMEM((HEAD_WEIGHT_SLOTS, tf_head, d), F32),
            pltpu.SemaphoreType.DMA((3, HEAD_WEIGHT_SLOTS)),
        )
        pltpu.emit_pipeline(
            functools.partial(_ffn_tail_step, x_hbm, n_ref, x_buf, x_sem, n_casts),
            grid=(n_tiles, nj_tail),
            in_specs=in_specs,
            out_specs=out_specs,
            dimension_semantics=(pltpu.ARBITRARY, pltpu.ARBITRARY),
        )(gain_hbm, wg_hbm, wu_hbm, wo_hbm, *streamed_in, o_hbm, *streamed_out)

    outs = pl.pallas_call(
        body,
        in_specs=[pl.BlockSpec(memory_space=pl.ANY)] * (5 + n_casts),
        out_specs=[pl.BlockSpec(memory_space=pl.ANY)] * (6 + n_casts),
        out_shape=[
            jax.ShapeDtypeStruct((t, d), F32),
            jax.ShapeDtypeStruct((d, d_ff), BF16),
            jax.ShapeDtypeStruct((d, d_ff), BF16),
            jax.ShapeDtypeStruct((d_ff, d), BF16),
            jax.ShapeDtypeStruct((nd, nd_ff), BF16),
            jax.ShapeDtypeStruct((nd, nd_ff), BF16),
            *cast_shapes,
        ],
        scratch_shapes=[
            pltpu.VMEM((tm, d), BF16),
            pltpu.VMEM((tm, d), F32),
            pltpu.SemaphoreType.DMA(()),
        ],
        compiler_params=pltpu.CompilerParams(vmem_limit_bytes=VMEM_LIMIT_BYTES),
        name="ffn_first",
    )(x, gain, w_in, w_out, next_w_in, *[mat for mat, _ in casts])
    return outs[0], outs[4], outs[5], list(outs[6:])


def _ffn_second(x, gain, wg_bf, wu_bf, wo_bf, final_gain, *, final_norm, tm=1024, tf=768):
    t, d = x.shape
    d_ff = wo_bf.shape[0]
    nj = pl.cdiv(d_ff, tf)
    last_width = d_ff - (nj - 1) * tf
    assert t % tm == 0 and nj >= 2 and wg_bf.shape == wu_bf.shape == (d, d_ff)
    assert last_width % LANES == 0
    vec = pl.BlockSpec((1, d), lambda i, j: (0, 0))
    return pl.pallas_call(
        functools.partial(_ffn_bf16_body, final_norm=final_norm,
                          last_width=None if last_width == tf else last_width),
        grid=(t // tm, nj),
        in_specs=[
            pl.BlockSpec(memory_space=pl.ANY),
            vec,
            pl.BlockSpec((d, tf), lambda i, j: (0, j)),
            pl.BlockSpec((d, tf), lambda i, j: (0, j)),
            pl.BlockSpec((tf, d), lambda i, j: (j, 0)),
            vec,
        ],
        out_specs=pl.BlockSpec((tm, d), lambda i, j: (i, 0)),
        out_shape=jax.ShapeDtypeStruct((t, d), F32),
        scratch_shapes=[
            pltpu.VMEM((tm, d), BF16),
            pltpu.VMEM((tm, d), F32),
            pltpu.SemaphoreType.DMA(()),
        ],
        compiler_params=_params(("arbitrary", "arbitrary")),
        name="ffn_second_final" if final_norm else "ffn_second",
    )(x, gain, wg_bf, wu_bf, wo_bf, final_gain)


def _pool_group(x, halo, window, w, scale, t_first):
    tt = x.shape[0]
    ext = jnp.concatenate([halo, x], axis=0)
    row = lax.broadcasted_iota(jnp.int32, ext.shape, 0)
    acc = ext
    span = 1
    while span < window:
        acc = acc + jnp.where(row >= span, pltpu.roll(acc, span, axis=0), 0.0)
        span *= 2
    cnt = jnp.clip(row + (t_first + 1 - POOL_HALO), 1, window).astype(F32)
    pooled = (acc / cnt - ext)[POOL_HALO:, :].astype(BF16)
    return _dot(pooled, w) * scale


def _mixer_body(h_ref, gain_ref, w_ref, wgate_ref, wpool_ref, pscale_ref, wa_ref, ba_ref, gn_ref,
                wout_ref, o_ref,
                y_ref, qd_s, ki_s, kt_s, bc_s, v_s, sg_s, st_s, halo_s,
                *, nt, wc, d_pool, dk, dv):
    s = pl.program_id(0)
    tt = h_ref.shape[0]
    heads = GLA_HEADS
    group_dim = d_pool // N_POOL_GROUPS
    pool_chunks = d_pool // wc
    q_chunk = pool_chunks
    k_chunk = q_chunk + 1
    v_chunk0 = k_chunk + 1
    v_chunks = heads * dv // wc
    g_chunk0 = v_chunk0 + v_chunks

    def w_chunk(c):
        return w_ref[c * wc:(c + 1) * wc, :]

    t = lax.rem(s, nt)

    @pl.when(t == 0)
    def _reset():
        st_s[...] = jnp.zeros_like(st_s)
        halo_s[...] = jnp.zeros_like(halo_s)

    n = _rmsnorm(h_ref[...], gain_ref[...]).astype(BF16)

    per_chunk = wc // group_dim
    for c in range(pool_chunks):
        u_c = _dot_nt(n, w_chunk(c))
        for part in range(per_chunk):
            gi = c * per_chunk + part
            cols = slice(gi * group_dim, (gi + 1) * group_dim)
            x = u_c[:, part * group_dim:(part + 1) * group_dim]
            y = _pool_group(x, halo_s[:, cols], POOL_WINDOWS[gi], wpool_ref[gi].astype(BF16),
                            pscale_ref[:, cols], t * tt)
            halo_s[:, cols] = x[tt - POOL_HALO:, :]
            y_ref[:, cols] = y.astype(y_ref.dtype)

    gate = _dot_nt(n, wgate_ref[...])
    logits = _dot(gate.astype(BF16), wa_ref[...]) + ba_ref[...]
    log_alpha = _log_sigmoid(logits) * (1.0 / GATE_LOGIT_NORMALIZER)
    pos = lax.broadcasted_iota(jnp.int32, log_alpha.shape, 0) & (CHUNK - 1)
    bcum = log_alpha
    span = 1
    while span < CHUNK:
        bcum = bcum + jnp.where(pos >= span, pltpu.roll(bcum, span, axis=0), 0.0)
        span *= 2
    bc_s[...] = bcum
    n_chunks = tt // CHUNK
    b3 = bcum.reshape(n_chunks, CHUNK, heads * dk)
    b_last = jnp.broadcast_to(b3[:, CHUNK - 1:CHUNK, :], b3.shape).reshape(tt, heads * dk)
    q = _dot_nt(n, w_chunk(q_chunk))
    qd_s[...] = (q * (dk ** -0.5) * jnp.exp(bcum)).astype(BF16)
    k = _dot_nt(n, w_chunk(k_chunk))
    ki_s[...] = (k * jnp.exp(-bcum)).astype(BF16)
    kt_s[...] = (k * jnp.exp(b_last - bcum)).astype(BF16)
    for c in range(v_chunks):
        cols = slice(c * wc, (c + 1) * wc)
        v_s[:, cols] = _dot_nt(n, w_chunk(v_chunk0 + c)).astype(BF16)
        sg_s[:, cols] = _silu(_dot_nt(n, w_chunk(g_chunk0 + c)))

    sb_rows = SB_CHUNKS * CHUNK
    r_idx = lax.broadcasted_iota(jnp.int32, (sb_rows, sb_rows), 0)
    c_idx = lax.broadcasted_iota(jnp.int32, (sb_rows, sb_rows), 1)
    same_chunk_causal = ((r_idx // CHUNK) == (c_idx // CHUNK)) & (r_idx >= c_idx)

    def superblock(sb, carry):
        row0 = pl.multiple_of(sb * sb_rows, sb_rows)
        rows = pl.ds(row0, sb_rows)
        decays = [jnp.exp(bc_s[pl.ds(row0 + (c * CHUNK + CHUNK - 1), 1), :]) for c in range(SB_CHUNKS)]
        for hd in range(heads):
            kc = slice(hd * dk, (hd + 1) * dk)
            vc = slice(hd * dv, (hd + 1) * dv)
            qd = qd_s[rows, kc]
            kt = kt_s[rows, kc]
            v = v_s[rows, vc]
            scores = jnp.where(same_chunk_causal, _dot_nt(qd, ki_s[rows, kc]), 0.0).astype(BF16)
            o = _dot(scores, v)
            state_t = st_s[hd]
            inter = []
            for c in range(SB_CHUNKS):
                cr = slice(c * CHUNK, (c + 1) * CHUNK)
                inter.append(_dot_nt(qd[cr], state_t.astype(BF16)))
                state_t = state_t * decays[c][:, kc] + _dot_tn(v[cr], kt[cr])
            st_s[hd] = state_t
            o = _rmsnorm(o + jnp.concatenate(inter, axis=0), gn_ref[...])
            y_ref[rows, d_pool + hd * dv:d_pool + (hd + 1) * dv] = (o * sg_s[rows, vc]).astype(y_ref.dtype)
        return carry

    lax.fori_loop(0, tt // sb_rows, superblock, 0)

    o_ref[...] = h_ref[...] + _dot(y_ref[...], wout_ref[...])


def _mixer(h, gain, w_in_t_bf, w_gate_t, w_pool, pool_scale, w_alpha, b_alpha, gla_norm, w_out_bf, *,
           seq, d_pool, dk, dv, tt=512, wc=512):
    t, d = h.shape
    heads = GLA_HEADS
    d_gla = heads * dv
    n_main = d_pool + 2 * heads * dk + 2 * d_gla
    assert w_in_t_bf.shape == (n_main, d)
    nt = seq // tt
    n_tiles = t // tt
    group_dim = d_pool // N_POOL_GROUPS
    assert seq % tt == 0 and tt % (SB_CHUNKS * CHUNK) == 0 and n_main % wc == 0 and heads * dk == wc
    assert d_pool % wc == 0 and d_gla % wc == 0 and wc % group_dim == 0
    const = lambda s: (0, 0)
    return pl.pallas_call(
        functools.partial(_mixer_body, nt=nt, wc=wc, d_pool=d_pool, dk=dk, dv=dv),
        grid=(n_tiles,),
        in_specs=[
            pl.BlockSpec((tt, d), lambda s: (s, 0)),
            pl.BlockSpec((1, d), const),
            pl.BlockSpec((n_main, d), const, pipeline_mode=pl.Buffered(1)),
            pl.BlockSpec((LANES, d), const),
            pl.BlockSpec((N_POOL_GROUPS, group_dim, group_dim), lambda s: (0, 0, 0)),
            pl.BlockSpec((1, d_pool), const),
            pl.BlockSpec((LANES, heads * dk), const),
            pl.BlockSpec((1, heads * dk), const),
            pl.BlockSpec((1, dv), const),
            pl.BlockSpec((d_pool + d_gla, d), const, pipeline_mode=pl.Buffered(1)),
        ],
        out_specs=pl.BlockSpec((tt, d), lambda s: (s, 0)),
        out_shape=jax.ShapeDtypeStruct((t, d), F32),
        scratch_shapes=[
            pltpu.VMEM((tt, d_pool + d_gla), BF16),
            pltpu.VMEM((tt, heads * dk), BF16),
            pltpu.VMEM((tt, heads * dk), BF16),
            pltpu.VMEM((tt, heads * dk), BF16),
            pltpu.VMEM((tt, heads * dk), F32),
            pltpu.VMEM((tt, d_gla), BF16),
            pltpu.VMEM((tt, d_gla), F32),
            pltpu.VMEM((heads, dv, dk), F32),
            pltpu.VMEM((POOL_HALO, d_pool), F32),
        ],
        compiler_params=_params(("arbitrary",)),
        name="mixer",
    )(h, gain, w_in_t_bf, w_gate_t, w_pool, pool_scale, w_alpha, b_alpha, gla_norm, w_out_bf)


def kernel(x, ffn1_norm, ffn1_w_in, ffn1_w_out, mix_norm, w_in_mix, w_pool, pool_scale,
           w_alpha, b_alpha, gla_norm, w_out_mix, ffn2_norm, ffn2_w_in, ffn2_w_out, final_norm):
    b, s, d = x.shape
    depth = ffn1_norm.shape[0]
    d_pool = pool_scale.shape[1]
    dk_total = w_alpha.shape[2]
    dk = dk_total // GLA_HEADS
    dv = gla_norm.shape[1]
    d_gla = GLA_HEADS * dv
    n_main = d_pool + 2 * dk_total + 2 * d_gla
    assert w_in_mix.shape[2] == n_main + GLA_GATE_RANK and s % CHUNK == 0 and depth >= 1

    h = x.reshape(b * s, d)
    final_gain = final_norm.reshape(1, d)
    for l in range(depth):
        last = l == depth - 1
        w_mix_t = jnp.swapaxes(w_in_mix[l], 0, 1)
        h, wg2_bf, wu2_bf, (wo2_bf, w_mix_t_bf, w_out_mix_bf) = _ffn_first(
            h, ffn1_norm[l].reshape(1, d), ffn1_w_in[l], ffn1_w_out[l], ffn2_w_in[l],
            [(ffn2_w_out[l], ffn2_w_out.shape[1]), (w_mix_t, n_main), (w_out_mix[l], w_out_mix.shape[1])])
        w_gate_t = jnp.pad(w_mix_t[n_main:, :], ((0, LANES - GLA_GATE_RANK), (0, 0))).astype(BF16)
        w_alpha_pad = jnp.pad(w_alpha[l], ((0, LANES - GLA_GATE_RANK), (0, 0))).astype(BF16)
        h = _mixer(h, mix_norm[l].reshape(1, d), w_mix_t_bf, w_gate_t, w_pool[l],
                   pool_scale[l].reshape(1, d_pool), w_alpha_pad, b_alpha[l].reshape(1, dk_total),
                   gla_norm[l].reshape(1, dv), w_out_mix_bf, seq=s, d_pool=d_pool, dk=dk, dv=dv)

        h = _ffn_second(h, ffn2_norm[l].reshape(1, d), wg2_bf, wu2_bf, wo2_bf, final_gain,
                        final_norm=last)
    return h.reshape(b, s, d)
```

```python
import functools

import jax
import jax.numpy as jnp
from jax import lax
from jax.experimental import pallas as pl
from jax.experimental.pallas import tpu as pltpu

F32 = jnp.float32
BF16 = jnp.bfloat16

EPS = 1e-6
POOL_WINDOWS = (2, 4, 8, 16)
N_POOL_GROUPS = len(POOL_WINDOWS)
POOL_HALO = 16
GLA_HEADS = 4
GLA_GATE_RANK = 16
GATE_LOGIT_NORMALIZER = 16.0
CHUNK = 64
SB_CHUNKS = 4
HEAD_WEIGHT_SLOTS = 3

LANES = 128
BF16_SUBLANES = 16
VMEM_LIMIT_BYTES = 58 * 1024 * 1024


def _rmsnorm(x, gain):
    ms = jnp.mean(x * x, axis=-1, keepdims=True)
    return x * lax.rsqrt(ms + EPS) * gain


def _silu(x):
    return x * (1.0 / (1.0 + jnp.exp(-x)))


def _log_sigmoid(z):
    return jnp.minimum(z, 0.0) - jnp.log1p(jnp.exp(-jnp.abs(z)))


def _dot(a, b):
    return jnp.dot(a, b, preferred_element_type=F32)


def _dot_nt(a, b):
    return lax.dot_general(a, b, (((1,), (1,)), ((), ())), preferred_element_type=F32)


def _dot_tn(a, b):
    return lax.dot_general(a, b, (((0,), (0,)), ((), ())), preferred_element_type=F32)


def _params(semantics):
    return pltpu.CompilerParams(dimension_semantics=semantics, vmem_limit_bytes=VMEM_LIMIT_BYTES)


def _ffn_step(j, nj, x_ref, gain_ref, weights, fgain_ref, o_ref, n_ref, final_norm,
              last_weights=None):
    def slice_update(n, get_weights):
        wg, wu, wo = get_weights()
        gate = _dot(n, wg)
        up = _dot(n, wu)
        act = (0.5 * _silu(gate) * up).astype(BF16)
        return _dot(act, wo)

    @pl.when(j == 0)
    def _first():
        x = x_ref[...]
        n = _rmsnorm(x, gain_ref[...]).astype(BF16)
        n_ref[...] = n
        o_ref[...] = x + slice_update(n, weights)

    if last_weights is None:

        @pl.when(j > 0)
        def _rest():
            o_ref[...] += slice_update(n_ref[...], weights)

    else:

        @pl.when((j > 0) & (j < nj - 1))
        def _middle():
            o_ref[...] += slice_update(n_ref[...], weights)

        @pl.when(j == nj - 1)
        def _last():
            o_ref[...] += slice_update(n_ref[...], last_weights)

    if final_norm:

        @pl.when(j == nj - 1)
        def _finish():
            o_ref[...] = _rmsnorm(o_ref[...], fgain_ref[...])


def _ffn_head_body(x_ref, gain_ref, w_in_hbm, w_out_hbm,
                   o_ref, wg_bf_ref, wu_bf_ref, wo_bf_ref,
                   n_ref, wg_buf, wu_buf, wo_buf, sems, x_ready):
    j = pl.program_id(0)
    nj = pl.num_programs(0)
    n_slots, _, tf = wg_buf.shape
    d_ff = w_out_hbm.shape[0]
    lookahead = n_slots - 1

    def slice_copies(step, slot):
        lo = pl.multiple_of(step * tf, tf)
        return (
            pltpu.make_async_copy(w_in_hbm.at[:, pl.ds(lo, tf)], wg_buf.at[slot], sems.at[0, slot]),
            pltpu.make_async_copy(w_in_hbm.at[:, pl.ds(d_ff + lo, tf)], wu_buf.at[slot], sems.at[1, slot]),
            pltpu.make_async_copy(w_out_hbm.at[pl.ds(lo, tf), :], wo_buf.at[slot], sems.at[2, slot]),
        )

    @pl.when(j == 0)
    def _prime():
        for step in range(lookahead):
            for c in slice_copies(step, step):
                c.start()
        x_ready()

    @pl.when(j + lookahead < nj)
    def _prefetch():
        for c in slice_copies(j + lookahead, lax.rem(j + lookahead, n_slots)):
            c.start()

    slot = lax.rem(j, n_slots)
    for c in slice_copies(j, slot):
        c.wait()

    def weights():
        wg = wg_buf[slot].astype(BF16)
        wu = wu_buf[slot].astype(BF16)
        wo = wo_buf[slot].astype(BF16)
        wg_bf_ref[...] = wg
        wu_bf_ref[...] = wu
        wo_bf_ref[...] = wo
        return wg, wu, wo

    _ffn_step(j, nj, x_ref, gain_ref, weights, None, o_ref, n_ref, False)


def _ffn_tail_step(x_hbm, n_ref, x_buf, x_sem, n_casts, *refs):
    gain_ref, wg_ref, wu_ref, wo_ref, nw_in_ref = refs[:5]
    cast_in_refs = refs[5:5 + n_casts]
    o_ref, nwg_bf_ref, nwu_bf_ref = refs[5 + n_casts:8 + n_casts]
    cast_out_refs = refs[8 + n_casts:]
    i = pl.program_id(0)
    j = pl.program_id(1)

    @pl.when((j == 1) & (i + 1 < pl.num_programs(0)))
    def _prefetch_x():
        _x_tile_copy(x_hbm, x_buf, x_sem, i + 2).start()

    @pl.when(j == 0)
    def _wait_x():
        _x_tile_copy(x_hbm, x_buf, x_sem, i + 1).wait()

    nd_ff = nwg_bf_ref.shape[1]
    nwg_bf_ref[...] = nw_in_ref[:, :nd_ff].astype(BF16)
    nwu_bf_ref[...] = nw_in_ref[:, nd_ff:].astype(BF16)
    for src_ref, dst_ref in zip(cast_in_refs, cast_out_refs):
        dst_ref[...] = src_ref[...].astype(BF16)

    _ffn_step(j, pl.num_programs(1), x_buf, gain_ref,
              lambda: (wg_ref[...], wu_ref[...], wo_ref[...]), None, o_ref, n_ref, False)


def _x_tile_copy(x_hbm, x_buf, x_sem, tile):
    tm = x_buf.shape[0]
    rows = pl.ds(pl.multiple_of(tile * tm, tm), tm)
    return pltpu.make_async_copy(x_hbm.at[rows, :], x_buf, x_sem)


def _ffn_bf16_body(x_hbm, gain_ref, wg_ref, wu_ref, wo_ref, fgain_ref, o_ref, n_ref, x_buf, x_sem,
                   *, final_norm, last_width):
    i = pl.program_id(0)
    j = pl.program_id(1)
    tm = x_buf.shape[0]

    def x_copy(tile):
        rows = pl.ds(pl.multiple_of(tile * tm, tm), tm)
        return pltpu.make_async_copy(x_hbm.at[rows, :], x_buf, x_sem)

    @pl.when((i == 0) & (j == 0))
    def _first_x():
        x_copy(0).start()

    @pl.when((j == 1) & (i + 1 < pl.num_programs(0)))
    def _prefetch_x():
        x_copy(i + 1).start()

    @pl.when(j == 0)
    def _wait_x():
        x_copy(i).wait()

    last_weights = None
    if last_width is not None:
        last_weights = lambda: (wg_ref[:, :last_width], wu_ref[:, :last_width], wo_ref[:last_width, :])
    _ffn_step(j, pl.num_programs(1), x_buf, gain_ref,
              lambda: (wg_ref[...], wu_ref[...], wo_ref[...]), fgain_ref, o_ref, n_ref, final_norm,
              last_weights=last_weights)


def _cast_row_block(rows, steps):
    for rb in range(BF16_SUBLANES, rows + 1, BF16_SUBLANES):
        if rows % rb == 0 and rows // rb <= steps:
            return rb
    raise ValueError(f"cannot cast {rows} rows in {steps} steps")


def _ffn_first(x, gain, w_in, w_out, next_w_in, casts, *, tm=1024, tf_head=256, tf_tail=512):
    t, d = x.shape
    d_ff = w_out.shape[0]
    assert t % tm == 0 and t // tm >= 2 and w_in.shape == (d, 2 * d_ff)
    assert d_ff % tf_head == 0 and d_ff % tf_tail == 0 and d_ff // tf_tail >= 2
    nj_head = d_ff // tf_head
    head_in_specs = [pl.BlockSpec((1, d), lambda j: (0, 0))]
    head_out_specs = [
        pl.BlockSpec((tm, d), lambda j: (0, 0)),
        pl.BlockSpec((d, tf_head), lambda j: (0, j)),
        pl.BlockSpec((d, tf_head), lambda j: (0, j)),
        pl.BlockSpec((tf_head, d), lambda j: (j, 0)),
    ]

    n_tiles = t // tm - 1
    nj_tail = d_ff // tf_tail
    steps = n_tiles * nj_tail
    nd, nd_ff2 = next_w_in.shape
    nd_ff = nd_ff2 // 2
    assert nd_ff % LANES == 0
    rb_in = _cast_row_block(nd, steps)
    step_block = lambda n_blocks: (lambda i, j: (jnp.minimum(i * nj_tail + j, n_blocks - 1), 0))
    cast_specs, cast_shapes = [], []
    for mat, rows in casts:
        rb = _cast_row_block(rows, steps)
        cast_specs.append(pl.BlockSpec((rb, mat.shape[1]), step_block(rows // rb)))
        cast_shapes.append(jax.ShapeDtypeStruct((rows, mat.shape[1]), BF16))
    cast_in = step_block(nd // rb_in)
    n_casts = len(casts)
    in_specs = [
        pl.BlockSpec((1, d), lambda i, j: (0, 0)),
        pl.BlockSpec((d, tf_tail), lambda i, j: (0, j)),
        pl.BlockSpec((d, tf_tail), lambda i, j: (0, j)),
        pl.BlockSpec((tf_tail, d), lambda i, j: (j, 0)),
        pl.BlockSpec((rb_in, nd_ff2), cast_in),
        *cast_specs,
    ]
    out_specs = [
        pl.BlockSpec((tm, d), lambda i, j: (i + 1, 0)),
        pl.BlockSpec((rb_in, nd_ff), cast_in),
        pl.BlockSpec((rb_in, nd_ff), cast_in),
        *cast_specs,
    ]

    def body(x_hbm, gain_hbm, w_in_hbm, w_out_hbm, *rest):
        streamed_in = rest[:1 + n_casts]
        o_hbm, wg_hbm, wu_hbm, wo_hbm = rest[1 + n_casts:5 + n_casts]
        streamed_out = rest[5 + n_casts:7 + 2 * n_casts]
        n_ref, x_buf, x_sem = rest[7 + 2 * n_casts:]

        tile0 = _x_tile_copy(x_hbm, x_buf, x_sem, 0)

        def head_phase(wg_buf, wu_buf, wo_buf, sems):
            def head_step(gain_ref, o_ref, wg_bf_ref, wu_bf_ref, wo_bf_ref):
                @pl.when(pl.program_id(0) == 1)
                def _prefetch_x():
                    _x_tile_copy(x_hbm, x_buf, x_sem, 1).start()

                _ffn_head_body(x_buf, gain_ref, w_in_hbm, w_out_hbm,
                               o_ref, wg_bf_ref, wu_bf_ref, wo_bf_ref,
                               n_ref, wg_buf, wu_buf, wo_buf, sems, x_ready=tile0.wait)

            pltpu.emit_pipeline(
                head_step,
                grid=(nj_head,),
                in_specs=head_in_specs,
                out_specs=head_out_specs,
                dimension_semantics=(pltpu.ARBITRARY,),
            )(gain_hbm, o_hbm, wg_hbm, wu_hbm, wo_hbm)

        tile0.start()
        pl.run_scoped(
            head_phase,
            pltpu.VMEM((HEAD_WEIGHT_SLOTS, d, tf_head), F32),
            pltpu.VMEM((HEAD_WEIGHT_SLOTS, d, tf_head), F32),
            pltpu.VMEM((HEAD_WEIGHT_SLOTS, tf_head, d), F32),
            pltpu.SemaphoreType.DMA((3, HEAD_WEIGHT_SLOTS)),
        )
        pltpu.emit_pipeline(
            functools.partial(_ffn_tail_step, x_hbm, n_ref, x_buf, x_sem, n_casts),
            grid=(n_tiles, nj_tail),
            in_specs=in_specs,
            out_specs=out_specs,
            dimension_semantics=(pltpu.ARBITRARY, pltpu.ARBITRARY),
        )(gain_hbm, wg_hbm, wu_hbm, wo_hbm, *streamed_in, o_hbm, *streamed_out)

    outs = pl.pallas_call(
        body,
        in_specs=[pl.BlockSpec(memory_space=pl.ANY)] * (5 + n_casts),
        out_specs=[pl.BlockSpec(memory_space=pl.ANY)] * (6 + n_casts),
        out_shape=[
            jax.ShapeDtypeStruct((t, d), F32),
            jax.ShapeDtypeStruct((d, d_ff), BF16),
            jax.ShapeDtypeStruct((d, d_ff), BF16),
            jax.ShapeDtypeStruct((d_ff, d), BF16),
            jax.ShapeDtypeStruct((nd, nd_ff), BF16),
            jax.ShapeDtypeStruct((nd, nd_ff), BF16),
            *cast_shapes,
        ],
        scratch_shapes=[
            pltpu.VMEM((tm, d), BF16),
            pltpu.VMEM((tm, d), F32),
            pltpu.SemaphoreType.DMA(()),
        ],
        compiler_params=pltpu.CompilerParams(vmem_limit_bytes=VMEM_LIMIT_BYTES),
        name="ffn_first",
    )(x, gain, w_in, w_out, next_w_in, *[mat for mat, _ in casts])
    return outs[0], outs[4], outs[5], list(outs[6:])


def _ffn_second(x, gain, wg_bf, wu_bf, wo_bf, final_gain, *, final_norm, tm=1024, tf=768):
    t, d = x.shape
    d_ff = wo_bf.shape[0]
    nj = pl.cdiv(d_ff, tf)
    last_width = d_ff - (nj - 1) * tf
    assert t % tm == 0 and nj >= 2 and wg_bf.shape == wu_bf.shape == (d, d_ff)
    assert last_width % LANES == 0
    vec = pl.BlockSpec((1, d), lambda i, j: (0, 0))
    return pl.pallas_call(
        functools.partial(_ffn_bf16_body, final_norm=final_norm,
                          last_width=None if last_width == tf else last_width),
        grid=(t // tm, nj),
        in_specs=[
            pl.BlockSpec(memory_space=pl.ANY),
            vec,
            pl.BlockSpec((d, tf), lambda i, j: (0, j)),
            pl.BlockSpec((d, tf), lambda i, j: (0, j)),
            pl.BlockSpec((tf, d), lambda i, j: (j, 0)),
            vec,
        ],
        out_specs=pl.BlockSpec((tm, d), lambda i, j: (i, 0)),
        out_shape=jax.ShapeDtypeStruct((t, d), F32),
        scratch_shapes=[
            pltpu.VMEM((tm, d), BF16),
            pltpu.VMEM((tm, d), F32),
            pltpu.SemaphoreType.DMA(()),
        ],
        compiler_params=_params(("arbitrary", "arbitrary")),
        name="ffn_second_final" if final_norm else "ffn_second",
    )(x, gain, wg_bf, wu_bf, wo_bf, final_gain)


def _pool_group(x, halo, window, w, scale, t_first):
    tt = x.shape[0]
    ext = jnp.concatenate([halo, x], axis=0)
    row = lax.broadcasted_iota(jnp.int32, ext.shape, 0)
    acc = ext
    span = 1
    while span < window:
        acc = acc + jnp.where(row >= span, pltpu.roll(acc, span, axis=0), 0.0)
        span *= 2
    cnt = jnp.clip(row + (t_first + 1 - POOL_HALO), 1, window).astype(F32)
    pooled = (acc / cnt - ext)[POOL_HALO:, :].astype(BF16)
    return _dot(pooled, w) * scale


def _mixer_body(h_ref, gain_ref, w_ref, wgate_ref, wpool_ref, pscale_ref, wa_ref, ba_ref, gn_ref,
                wout_ref, o_ref,
                y_ref, qd_s, ki_s, kt_s, bc_s, v_s, sg_s, st_s, halo_s,
                *, nt, wc, d_pool, dk, dv):
    s = pl.program_id(0)
    tt = h_ref.shape[0]
    heads = GLA_HEADS
    group_dim = d_pool // N_POOL_GROUPS
    pool_chunks = d_pool // wc
    q_chunk = pool_chunks
    k_chunk = q_chunk + 1
    v_chunk0 = k_chunk + 1
    v_chunks = heads * dv // wc
    g_chunk0 = v_chunk0 + v_chunks

    def w_chunk(c):
        return w_ref[c * wc:(c + 1) * wc, :]

    t = lax.rem(s, nt)

    @pl.when(t == 0)
    def _reset():
        st_s[...] = jnp.zeros_like(st_s)
        halo_s[...] = jnp.zeros_like(halo_s)

    n = _rmsnorm(h_ref[...], gain_ref[...]).astype(BF16)

    per_chunk = wc // group_dim
    for c in range(pool_chunks):
        u_c = _dot_nt(n, w_chunk(c))
        for part in range(per_chunk):
            gi = c * per_chunk + part
            cols = slice(gi * group_dim, (gi + 1) * group_dim)
            x = u_c[:, part * group_dim:(part + 1) * group_dim]
            y = _pool_group(x, halo_s[:, cols], POOL_WINDOWS[gi], wpool_ref[gi].astype(BF16),
                            pscale_ref[:, cols], t * tt)
            halo_s[:, cols] = x[tt - POOL_HALO:, :]
            y_ref[:, cols] = y.astype(y_ref.dtype)

    gate = _dot_nt(n, wgate_ref[...])
    logits = _dot(gate.astype(BF16), wa_ref[...]) + ba_ref[...]
    log_alpha = _log_sigmoid(logits) * (1.0 / GATE_LOGIT_NORMALIZER)
    pos = lax.broadcasted_iota(jnp.int32, log_alpha.shape, 0) & (CHUNK - 1)
    bcum = log_alpha
    span = 1
    while span < CHUNK:
        bcum = bcum + jnp.where(pos >= span, pltpu.roll(bcum, span, axis=0), 0.0)
        span *= 2
    bc_s[...] = bcum
    n_chunks = tt // CHUNK
    b3 = bcum.reshape(n_chunks, CHUNK, heads * dk)
    b_last = jnp.broadcast_to(b3[:, CHUNK - 1:CHUNK, :], b3.shape).reshape(tt, heads * dk)
    q = _dot_nt(n, w_chunk(q_chunk))
    qd_s[...] = (q * (dk ** -0.5) * jnp.exp(bcum)).astype(BF16)
    k = _dot_nt(n, w_chunk(k_chunk))
    ki_s[...] = (k * jnp.exp(-bcum)).astype(BF16)
    kt_s[...] = (k * jnp.exp(b_last - bcum)).astype(BF16)
    for c in range(v_chunks):
        cols = slice(c * wc, (c + 1) * wc)
        v_s[:, cols] = _dot_nt(n, w_chunk(v_chunk0 + c)).astype(BF16)
        sg_s[:, cols] = _silu(_dot_nt(n, w_chunk(g_chunk0 + c)))

    sb_rows = SB_CHUNKS * CHUNK
    r_idx = lax.broadcasted_iota(jnp.int32, (sb_rows, sb_rows), 0)
    c_idx = lax.broadcasted_iota(jnp.int32, (sb_rows, sb_rows), 1)
    same_chunk_causal = ((r_idx // CHUNK) == (c_idx // CHUNK)) & (r_idx >= c_idx)

    def superblock(sb, carry):
        row0 = pl.multiple_of(sb * sb_rows, sb_rows)
        rows = pl.ds(row0, sb_rows)
        decays = [jnp.exp(bc_s[pl.ds(row0 + (c * CHUNK + CHUNK - 1), 1), :]) for c in range(SB_CHUNKS)]
        for hd in range(heads):
            kc = slice(hd * dk, (hd + 1) * dk)
            vc = slice(hd * dv, (hd + 1) * dv)
            qd = qd_s[rows, kc]
            kt = kt_s[rows, kc]
            v = v_s[rows, vc]
            scores = jnp.where(same_chunk_causal, _dot_nt(qd, ki_s[rows, kc]), 0.0).astype(BF16)
            o = _dot(scores, v)
            state_t = st_s[hd]
            inter = []
            for c in range(SB_CHUNKS):
                cr = slice(c * CHUNK, (c + 1) * CHUNK)
                inter.append(_dot_nt(qd[cr], state_t.astype(BF16)))
                state_t = state_t * decays[c][:, kc] + _dot_tn(v[cr], kt[cr])
            st_s[hd] = state_t
            o = _rmsnorm(o + jnp.concatenate(inter, axis=0), gn_ref[...])
            y_ref[rows, d_pool + hd * dv:d_pool + (hd + 1) * dv] = (o * sg_s[rows, vc]).astype(y_ref.dtype)
        return carry

    lax.fori_loop(0, tt // sb_rows, superblock, 0)

    o_ref[...] = h_ref[...] + _dot(y_ref[...], wout_ref[...])


def _mixer(h, gain, w_in_t_bf, w_gate_t, w_pool, pool_scale, w_alpha, b_alpha, gla_norm, w_out_bf, *,
           seq, d_pool, dk, dv, tt=512, wc=512):
    t, d = h.shape
    heads = GLA_HEADS
    d_gla = heads * dv
    n_main = d_pool + 2 * heads * dk + 2 * d_gla
    assert w_in_t_bf.shape == (n_main, d)
    nt = seq // tt
    n_tiles = t // tt
    group_dim = d_pool // N_POOL_GROUPS
    assert seq % tt == 0 and tt % (SB_CHUNKS * CHUNK) == 0 and n_main % wc == 0 and heads * dk == wc
    assert d_pool % wc == 0 and d_gla % wc == 0 and wc % group_dim == 0
    const = lambda s: (0, 0)
    return pl.pallas_call(
        functools.partial(_mixer_body, nt=nt, wc=wc, d_pool=d_pool, dk=dk, dv=dv),
        grid=(n_tiles,),
        in_specs=[
            pl.BlockSpec((tt, d), lambda s: (s, 0)),
            pl.BlockSpec((1, d), const),
            pl.BlockSpec((n_main, d), const, pipeline_mode=pl.Buffered(1)),
            pl.BlockSpec((LANES, d), const),
            pl.BlockSpec((N_POOL_GROUPS, group_dim, group_dim), lambda s: (0, 0, 0)),
            pl.BlockSpec((1, d_pool), const),
            pl.BlockSpec((LANES, heads * dk), const),
            pl.BlockSpec((1, heads * dk), const),
            pl.BlockSpec((1, dv), const),
            pl.BlockSpec((d_pool + d_gla, d), const, pipeline_mode=pl.Buffered(1)),
        ],
        out_specs=pl.BlockSpec((tt, d), lambda s: (s, 0)),
        out_shape=jax.ShapeDtypeStruct((t, d), F32),
        scratch_shapes=[
            pltpu.VMEM((tt, d_pool + d_gla), BF16),
            pltpu.VMEM((tt, heads * dk), BF16),
            pltpu.VMEM((tt, heads * dk), BF16),
            pltpu.VMEM((tt, heads * dk), BF16),
            pltpu.VMEM((tt, heads * dk), F32),
            pltpu.VMEM((tt, d_gla), BF16),
            pltpu.VMEM((tt, d_gla), F32),
            pltpu.VMEM((heads, dv, dk), F32),
            pltpu.VMEM((POOL_HALO, d_pool), F32),
        ],
        compiler_params=_params(("arbitrary",)),
        name="mixer",
    )(h, gain, w_in_t_bf, w_gate_t, w_pool, pool_scale, w_alpha, b_alpha, gla_norm, w_out_bf)


def kernel(x, ffn1_norm, ffn1_w_in, ffn1_w_out, mix_norm, w_in_mix, w_pool, pool_scale,
           w_alpha, b_alpha, gla_norm, w_out_mix, ffn2_norm, ffn2_w_in, ffn2_w_out, final_norm):
    b, s, d = x.shape
    depth = ffn1_norm.shape[0]
    d_pool = pool_scale.shape[1]
    dk_total = w_alpha.shape[2]
    dk = dk_total // GLA_HEADS
    dv = gla_norm.shape[1]
    d_gla = GLA_HEADS * dv
    n_main = d_pool + 2 * dk_total + 2 * d_gla
    assert w_in_mix.shape[2] == n_main + GLA_GATE_RANK and s % CHUNK == 0 and depth >= 1

    h = x.reshape(b * s, d)
    final_gain = final_norm.reshape(1, d)
    for l in range(depth):
        last = l == depth - 1
        w_mix_t = jnp.swapaxes(w_in_mix[l], 0, 1)
        h, wg2_bf, wu2_bf, (wo2_bf, w_mix_t_bf, w_out_mix_bf) = _ffn_first(
            h, ffn1_norm[l].reshape(1, d), ffn1_w_in[l], ffn1_w_out[l], ffn2_w_in[l],
            [(ffn2_w_out[l], ffn2_w_out.shape[1]), (w_mix_t, n_main), (w_out_mix[l], w_out_mix.shape[1])])
        w_gate_t = jnp.pad(w_mix_t[n_main:, :], ((0, LANES - GLA_GATE_RANK), (0, 0))).astype(BF16)
        w_alpha_pad = jnp.pad(w_alpha[l], ((0, LANES - GLA_GATE_RANK), (0, 0))).astype(BF16)
        h = _mixer(h, mix_norm[l].reshape(1, d), w_mix_t_bf, w_gate_t, w_pool[l],
                   pool_scale[l].reshape(1, d_pool), w_alpha_pad, b_alpha[l].reshape(1, dk_total),
                   gla_norm[l].reshape(1, dv), w_out_mix_bf, seq=s, d_pool=d_pool, dk=dk, dv=dv)

        h = _ffn_second(h, ffn2_norm[l].reshape(1, d), wg2_bf, wu2_bf, wo2_bf, final_gain,
                        final_norm=last)
    return h.reshape(b, s, d)
```
